```python
import math
import jax, jax.numpy as jnp
from jax import lax
import numpy as np

D_MODEL = 2048
BATCH = 4
SEQ = 2048
DEPTH = 4

PLE_DIM = 256
GM_CHUNK = 128
GM_GROUPS = 4
GM_GROUP_CH = 128
GM_WIDTH = GM_GROUPS * GM_GROUP_CH
RW_HEADS = 16
RW_HEAD_DIM = 64
RW_WIDTH = RW_HEADS * RW_HEAD_DIM
RW_LORA_W = 64
RW_LORA_A = 64
RW_LORA_G = 160
RW_GN_EPS = 64e-5
FX_HEADS = 8
FX_HEAD_DIM = 64
FX_WIDTH = FX_HEADS * FX_HEAD_DIM
FX_Q_BLOCK = 128
N_BRANCHES = 3
FFN_DIM = 5632
MOE_EXPERTS = 8
MOE_TOP_K = 2
MOE_FFN_DIM = 4096
N_DENSE = (DEPTH + 1) // 2
N_MOE = DEPTH // 2
DEEPNORM_ALPHA = (2.0 * DEPTH) ** 0.25
DEEPNORM_BETA = (8.0 * DEPTH) ** -0.25
LN_EPS = 1e-5

A_COLS = 2 * GM_WIDTH
RW_COLS = 3 * RW_WIDTH + RW_LORA_W + RW_LORA_A + RW_LORA_G
FX_COLS = 3 * FX_WIDTH + FX_HEADS
GATE_COLS = N_BRANCHES * D_MODEL
IN_COLS = A_COLS + RW_COLS + FX_COLS + GATE_COLS
IN_SPLITS = (A_COLS, A_COLS + RW_COLS, A_COLS + RW_COLS + FX_COLS)
RW_SPLITS = (RW_WIDTH, 2 * RW_WIDTH, 3 * RW_WIDTH, 3 * RW_WIDTH + RW_LORA_W,
             3 * RW_WIDTH + RW_LORA_W + RW_LORA_A)
FX_SPLITS = (FX_WIDTH, 2 * FX_WIDTH, 3 * FX_WIDTH)

kernel_name = "hybrid_sgu_rwkv7_fox_moe_deepnorm"


def layer_norm(h, g, b, eps=LN_EPS):
    hf = h.astype(jnp.float32)
    mu = jnp.mean(hf, axis=-1, keepdims=True)
    var = jnp.mean(jnp.square(hf - mu), axis=-1, keepdims=True)
    return ((hf - mu) * lax.rsqrt(var + eps)).astype(h.dtype) * g + b


def chunked_spatial_gating(z, ln_g, ln_b, w_s, b_s):
    bsz, seq, _ = z.shape
    u, v = jnp.split(z, 2, axis=-1)
    v = layer_norm(v, ln_g, ln_b)
    v = v.reshape(bsz, seq // GM_CHUNK, GM_CHUNK, GM_GROUPS, GM_GROUP_CH)
    causal = jnp.tril(jnp.ones((GM_CHUNK, GM_CHUNK), dtype=bool))
    w = jnp.where(causal[None], w_s, jnp.zeros_like(w_s))
    s = jnp.einsum('gts,bcsgd->bctgd', w, v) + b_s.T[None, None, :, :, None]
    return u * s.reshape(bsz, seq, GM_WIDTH)


def token_shift(h, mu):
    h_prev = jnp.pad(h, ((0, 0), (1, 0), (0, 0)))[:, :-1]
    return h + (h_prev - h) * mu


def rwkv7_recurrence(r, w, k, v, kk, a):
    bsz, _, heads, n = r.shape
    xs = tuple(jnp.moveaxis(t, 1, 0) for t in (r, w, k, v, kk, a))

    def step(state, inp):
        r_t, w_t, k_t, v_t, kk_t, a_t = inp
        sa = jnp.einsum('bhvk,bhk->bhv', state, kk_t)
        state = (state * w_t[:, :, None, :]
                 - sa[..., None] * (kk_t * a_t)[:, :, None, :]
                 + v_t[..., None] * k_t[:, :, None, :])
        y_t = jnp.einsum('bhvk,bhk->bhv', state, r_t)
        return state, y_t

    state0 = jnp.zeros((bsz, heads, n, n), jnp.float32)
    _, ys = lax.scan(step, state0, xs)
    return jnp.moveaxis(ys, 0, 1)


def rwkv7_time_mix(zb, mu, w0, w_b, a0, a_b, g_b, k_k, k_a, r_k, lnx_g, lnx_b):
    bsz, seq, _ = zb.shape
    zb = token_shift(zb, mu)
    r, k, v, xw, xa, xg = jnp.split(zb, RW_SPLITS, axis=-1)
    w_log = -jax.nn.softplus(-(w0 + jnp.tanh(xw) @ w_b)) - 0.5
    decay = jnp.exp(-jnp.exp(w_log.astype(jnp.float32)))
    a = jax.nn.sigmoid(a0 + xa @ a_b)
    g = jax.nn.sigmoid(xg) @ g_b

    def heads(t):
        return t.reshape(bsz, seq, RW_HEADS, RW_HEAD_DIM).astype(jnp.float32)

    def head_param(t):
        return t.reshape(RW_HEADS, RW_HEAD_DIM).astype(jnp.float32)

    r_h, k_h, v_h, w_h, a_h = heads(r), heads(k), heads(v), heads(decay), heads(a)
    kk = k_h * head_param(k_k)
    kk = kk / jnp.maximum(jnp.sqrt(jnp.sum(jnp.square(kk), axis=-1, keepdims=True)), 1e-12)
    k_h = k_h * (1.0 + (a_h - 1.0) * head_param(k_a))
    y = rwkv7_recurrence(r_h, w_h, k_h, v_h, kk, a_h)
    mu_y = jnp.mean(y, axis=-1, keepdims=True)
    var_y = jnp.mean(jnp.square(y - mu_y), axis=-1, keepdims=True)
    y = ((y - mu_y) * lax.rsqrt(var_y + RW_GN_EPS)).reshape(bsz, seq, RW_WIDTH)
    y = y.astype(zb.dtype) * lnx_g + lnx_b
    bonus = jnp.sum(r_h * k_h * r_k.astype(jnp.float32), axis=-1, keepdims=True) * v_h
    y = y + bonus.reshape(bsz, seq, RW_WIDTH).astype(zb.dtype)
    return y * g


def forgetting_attention(zc, f_bias):
    bsz, seq, _ = zc.shape
    q, k, v, f_logit = jnp.split(zc, FX_SPLITS, axis=-1)
    shape4 = (bsz, seq, FX_HEADS, FX_HEAD_DIM)
    q = q.reshape(shape4).astype(jnp.float32) * (FX_HEAD_DIM ** -0.5)
    k = k.reshape(shape4).astype(jnp.float32)
    v = v.reshape(shape4).astype(jnp.float32)
    log_f = jax.nn.log_sigmoid((f_logit + f_bias).astype(jnp.float32))
    cum = jnp.cumsum(log_f, axis=1).transpose(0, 2, 1)
    outs = []
    for blk in range(seq // FX_Q_BLOCK):
        q0, q1 = blk * FX_Q_BLOCK, (blk + 1) * FX_Q_BLOCK
        s = jnp.einsum('bqhd,bkhd->bhqk', q[:, q0:q1], k[:, :q1])
        s = s + cum[:, :, q0:q1, None] - cum[:, :, None, :q1]
        causal = jnp.arange(q0, q1)[:, None] >= jnp.arange(q1)[None, :]
        s = jnp.where(causal, s, -jnp.inf)
        probs = jax.nn.softmax(s, axis=-1)
        outs.append(jnp.einsum('bhqk,bkhd->bqhd', probs, v[:, :q1]))
    o = jnp.concatenate(outs, axis=1)
    return o.reshape(bsz, seq, FX_WIDTH).astype(zc.dtype)


def swiglu(h, w_gate, w_up, w_down):
    return (jax.nn.silu(h @ w_gate) * (h @ w_up)) @ w_down


def top2_moe(h, w_router, b_router, w_gate, w_up, w_down):
    logits = (h @ w_router).astype(jnp.float32) + b_router.astype(jnp.float32)
    top_logit, top_idx = lax.top_k(logits, MOE_TOP_K)
    top_w = jax.nn.softmax(top_logit, axis=-1)
    combine = jnp.sum(jax.nn.one_hot(top_idx, MOE_EXPERTS, dtype=jnp.float32) * top_w[..., None],
                      axis=-2).astype(h.dtype)
    out = jnp.zeros_like(h)
    for e in range(MOE_EXPERTS):
        out = out + combine[..., e:e + 1] * swiglu(h, w_gate[e], w_up[e], w_down[e])
    return out


def setup_inputs(seed: int = 0) -> dict:
    key = jax.random.key(seed)
    keys = jax.random.split(key, 64)
    counter = [0]
    f32 = jnp.float32

    def next_key():
        kk = keys[counter[0]]
        counter[0] += 1
        return kk

    def nrm(shape, scale):
        return scale * jax.random.normal(next_key(), shape, f32)

    def unif(shape, lo, hi):
        return jax.random.uniform(next_key(), shape, f32, lo, hi)

    L, D, beta = DEPTH, D_MODEL, DEEPNORM_BETA
    return {
        "x": nrm((BATCH, SEQ, D), 1.0),
        "p": nrm((DEPTH, BATCH, SEQ, PLE_DIM), 1.0),
        "w_in": nrm((L, D, IN_COLS), D ** -0.5),
        "gate_b": nrm((L, GATE_COLS), 0.1),
        "gm_ln_g": 1.0 + nrm((L, GM_WIDTH), 0.1),
        "gm_ln_b": nrm((L, GM_WIDTH), 0.02),
        "gm_ws": nrm((L, GM_GROUPS, GM_CHUNK, GM_CHUNK), GM_CHUNK ** -0.5),
        "gm_bs": 1.0 + nrm((L, GM_GROUPS, GM_CHUNK), 0.1),
        "rw_mu": unif((L, RW_COLS), 0.0, 1.0),
        "rw_w0": unif((L, RW_WIDTH), -6.0, 1.0),
        "rw_wb": nrm((L, RW_LORA_W, RW_WIDTH), 0.5 * RW_LORA_W ** -0.5),
        "rw_a0": nrm((L, RW_WIDTH), 0.5),
        "rw_ab": nrm((L, RW_LORA_A, RW_WIDTH), 0.5 * RW_LORA_A ** -0.5),
        "rw_gb": nrm((L, RW_LORA_G, RW_WIDTH), RW_LORA_G ** -0.5),
        "rw_kk": 0.85 + nrm((L, RW_WIDTH), 0.1),
        "rw_ka": 1.0 + nrm((L, RW_WIDTH), 0.1),
        "rw_rk": nrm((L, RW_HEADS, RW_HEAD_DIM), 0.1),
        "rw_lnx_g": 1.0 + nrm((L, RW_WIDTH), 0.1),
        "rw_lnx_b": nrm((L, RW_WIDTH), 0.02),
        "fx_fb": unif((L, FX_HEADS), 1.0, 4.0),
        "proj_a": nrm((L, GM_WIDTH, D), GM_WIDTH ** -0.5),
        "proj_b": nrm((L, RW_WIDTH, D), RW_WIDTH ** -0.5),
        "proj_c": nrm((L, FX_WIDTH, D), FX_WIDTH ** -0.5),
        "w_out": nrm((L, D, D), beta * D ** -0.5),
        "ln1_g": 1.0 + nrm((L, D), 0.1),
        "ln1_b": nrm((L, D), 0.02),
        "ffn_w1": nrm((N_DENSE, D, FFN_DIM), D ** -0.5),
        "ffn_w3": nrm((N_DENSE, D, FFN_DIM), D ** -0.5),
        "ffn_w2": nrm((N_DENSE, FFN_DIM, D), beta * FFN_DIM ** -0.5),
        "moe_router": nrm((N_MOE, D, MOE_EXPERTS), D ** -0.5),
        "moe_router_b": nrm((N_MOE, MOE_EXPERTS), 0.01),
        "moe_w1": nrm((N_MOE, MOE_EXPERTS, D, MOE_FFN_DIM), D ** -0.5),
        "moe_w3": nrm((N_MOE, MOE_EXPERTS, D, MOE_FFN_DIM), D ** -0.5),
        "moe_w2": nrm((N_MOE, MOE_EXPERTS, MOE_FFN_DIM, D), beta * MOE_FFN_DIM ** -0.5),
        "ple_gate_w": nrm((L, D, D), D ** -0.5),
        "ple_gate_b": nrm((L, D), 0.1),
        "ple_proj": nrm((L, PLE_DIM, D), beta * PLE_DIM ** -0.5),
        "ln2_g": 1.0 + nrm((L, D), 0.1),
        "ln2_b": nrm((L, D), 0.02),
    }


def reference(x, p, w_in, gate_b, gm_ln_g, gm_ln_b, gm_ws, gm_bs, rw_mu, rw_w0, rw_wb,
              rw_a0, rw_ab, rw_gb, rw_kk, rw_ka, rw_rk, rw_lnx_g, rw_lnx_b, fx_fb,
              proj_a, proj_b, proj_c, w_out, ln1_g, ln1_b, ffn_w1, ffn_w3, ffn_w2,
              moe_router, moe_router_b, moe_w1, moe_w3, moe_w2, ple_gate_w, ple_gate_b,
              ple_proj, ln2_g, ln2_b):
    for i in range(DEPTH):
        z = x @ w_in[i]
        za, zb, zc, zg = jnp.split(z, IN_SPLITS, axis=-1)
        y_a = chunked_spatial_gating(jax.nn.gelu(za), gm_ln_g[i], gm_ln_b[i], gm_ws[i], gm_bs[i])
        y_b = rwkv7_time_mix(zb, rw_mu[i], rw_w0[i], rw_wb[i], rw_a0[i], rw_ab[i], rw_gb[i],
                             rw_kk[i], rw_ka[i], rw_rk[i], rw_lnx_g[i], rw_lnx_b[i])
        y_c = forgetting_attention(zc, fx_fb[i])
        gates = jax.nn.sigmoid(zg + gate_b[i]).reshape(zg.shape[0], zg.shape[1], N_BRANCHES, D_MODEL)
        merged = (gates[..., 0, :] * (y_a @ proj_a[i])
                  + gates[..., 1, :] * (y_b @ proj_b[i])
                  + gates[..., 2, :] * (y_c @ proj_c[i]))
        x1 = layer_norm(DEEPNORM_ALPHA * x + merged @ w_out[i], ln1_g[i], ln1_b[i])
        j = i // 2
        if i % 2 == 0:
            ff = swiglu(x1, ffn_w1[j], ffn_w3[j], ffn_w2[j])
        else:
            ff = top2_moe(x1, moe_router[j], moe_router_b[j], moe_w1[j], moe_w3[j], moe_w2[j])
        ple = jax.nn.sigmoid(x1 @ ple_gate_w[i] + ple_gate_b[i]) * (p[i] @ ple_proj[i])
        x = layer_norm(DEEPNORM_ALPHA * x1 + ff + ple, ln2_g[i], ln2_b[i])
    return x
```

```python
import functools
import math

import jax
import jax.numpy as jnp
from jax import lax
from jax.experimental import pallas as pl
from jax.experimental.pallas import tpu as pltpu

F32 = jnp.float32
BF16 = jnp.bfloat16

D_MODEL = 2048
DEPTH = 4
PLE_DIM = 256
GM_CHUNK = 128
GM_GROUPS = 4
GM_WIDTH = 512
RW_HEADS = 16
RW_HEAD_DIM = 64
RW_WIDTH = 1024
RW_LORA_W = 64
RW_LORA_A = 64
RW_LORA_G = 160
RW_GN_EPS = 64e-5
FX_HEADS = 8
FX_HEAD_DIM = 64
FX_WIDTH = 512
MOE_EXPERTS = 8
DEEPNORM_ALPHA = (2.0 * DEPTH) ** 0.25
LN_EPS = 1e-5

COL_A = 0
COL_R = 1024
COL_MISC = 4096
COL_FX = 4608
COL_GATE = 6144
IN_COLS_PACKED = 12288
MISC_W = 512
MISC_F_LANE = 128

RW_CHUNK = 64
RW_QUAD = 4
RW_TB = 256
LOGW_SCALE = math.exp(-0.5)

VMEM_LIMIT = 56 * 1024 * 1024

NT_DIMS = (((1,), (1,)), ((), ()))
TN_DIMS = (((0,), (0,)), ((), ()))


def _params(*sem):
    return pltpu.CompilerParams(dimension_semantics=sem, vmem_limit_bytes=VMEM_LIMIT)


def _dot(a, b):
    return jnp.dot(a, b, preferred_element_type=F32)


def _dg(a, b, dims):
    return lax.dot_general(a, b, dims, preferred_element_type=F32)


def _sigmoid(x):
    return 1.0 / (1.0 + jnp.exp(-x))


def _layer_norm(h, g, b, eps):
    mu = jnp.mean(h, axis=-1, keepdims=True)
    d = h - mu
    var = jnp.mean(d * d, axis=-1, keepdims=True)
    return d * lax.rsqrt(var + eps) * g + b


def _mm_kernel(x_ref, w_ref, o_ref):
    o_ref[...] = _dot(x_ref[...], w_ref[...]).astype(o_ref.dtype)


def _matmul(x, w, out_dtype, tm, tn):
    m, k = x.shape
    n = w.shape[1]
    return pl.pallas_call(
        _mm_kernel,
        grid=(m // tm, n // tn),
        in_specs=[pl.BlockSpec((tm, k), lambda i, j: (i, 0)),
                  pl.BlockSpec((k, tn), lambda i, j: (0, j))],
        out_specs=pl.BlockSpec((tm, tn), lambda i, j: (i, j)),
        out_shape=jax.ShapeDtypeStruct((m, n), out_dtype),
        compiler_params=_params("parallel", "parallel"),
        name="in_proj",
    )(x, w)


def _gmlp_kernel(z_ref, g_ref, b_ref, ws_ref, bs_ref, o_ref, *, rows):
    z = z_ref[...]
    za = 0.5 * z * (1.0 + jnp.tanh(math.sqrt(2.0 / math.pi) * (z + 0.044715 * (z * z * z))))
    u = za[:, :GM_WIDTH]
    v = _layer_norm(za[:, GM_WIDTH:], g_ref[...], b_ref[...], LN_EPS).astype(BF16)
    t_idx = lax.broadcasted_iota(jnp.int32, (GM_CHUNK, GM_CHUNK), 0)
    s_idx = lax.broadcasted_iota(jnp.int32, (GM_CHUNK, GM_CHUNK), 1)
    causal = t_idx >= s_idx
    for g in range(GM_GROUPS):
        w = jnp.where(causal, ws_ref[g], 0.0).astype(BF16)
        cs = slice(g * GM_CHUNK, (g + 1) * GM_CHUNK)
        for c in range(rows // GM_CHUNK):
            rs = slice(c * GM_CHUNK, (c + 1) * GM_CHUNK)
            s = _dot(w, v[rs, cs]) + bs_ref[g]
            o_ref[rs, cs] = (u[rs, cs] * s).astype(o_ref.dtype)


def _gmlp(z, ln_g, ln_b, ws, bs, rows=512):
    m = z.shape[0]
    return pl.pallas_call(
        functools.partial(_gmlp_kernel, rows=rows),
        grid=(m // rows,),
        in_specs=[pl.BlockSpec((rows, 2 * GM_WIDTH), lambda i: (i, COL_A // (2 * GM_WIDTH))),
                  pl.BlockSpec((1, GM_WIDTH), lambda i: (0, 0)),
                  pl.BlockSpec((1, GM_WIDTH), lambda i: (0, 0)),
                  pl.BlockSpec((GM_GROUPS, GM_CHUNK, GM_CHUNK), lambda i: (0, 0, 0)),
                  pl.BlockSpec((GM_GROUPS, GM_CHUNK, 1), lambda i: (0, 0, 0))],
        out_specs=pl.BlockSpec((rows, GM_WIDTH), lambda i: (i, 0)),
        out_shape=jax.ShapeDtypeStruct((m, GM_WIDTH), BF16),
        compiler_params=_params("parallel"),
        name="gmlp",
    )(z, ln_g, ln_b, ws, bs)


def _token_shift(z_ref, p_ref, mu_ref, first):
    h = z_ref[0]
    prev_row = jnp.where(first, 0.0, p_ref[0, 7:8, :])
    hp = pltpu.roll(h, 1, axis=0)
    row = lax.broadcasted_iota(jnp.int32, h.shape, 0)
    hp = jnp.where(row == 0, prev_row, hp)
    return h + (hp - h) * mu_ref[...]


def _stage(dst_ref, val, n_chunks):
    for c in range(n_chunks):
        for h in range(RW_HEADS):
            dst_ref[c, h * RW_CHUNK:(h + 1) * RW_CHUNK, :] = (
                val[c * RW_CHUNK:(c + 1) * RW_CHUNK, h * RW_HEAD_DIM:(h + 1) * RW_HEAD_DIM])


def _rwkv_kernel(zr_ref, zk_ref, zv_ref, zm_ref, pr_ref, pk_ref, pv_ref, pm_ref,
                 mur_ref, muk_ref, muv_ref, mum_ref, w0_ref, wb_ref, a0_ref, ab_ref, gb_ref,
                 kk_ref, ka_ref, rk_ref, lng_ref, lnb_ref, o_ref,
                 r_s, k_s, v_s, kk_s, a_s, lw_s, g_s, y_s, st_s, *, n_chunks):
    first = pl.program_id(1) == 0
    qrows = RW_QUAD * RW_CHUNK

    @pl.when(first)
    def _():
        st_s[...] = jnp.zeros_like(st_s)

    m = _token_shift(zm_ref, pm_ref, mum_ref, first)
    m01 = m[:, 0:128]
    lw_lin = _dot(jnp.tanh(m01).astype(BF16), wb_ref[...]) + w0_ref[...]
    _stage(lw_s, -LOGW_SCALE * _sigmoid(lw_lin), n_chunks)
    a = _sigmoid(a0_ref[...] + _dot(m01.astype(BF16), ab_ref[...]))
    _stage(a_s, a, n_chunks)
    _stage(g_s, _dot(_sigmoid(m[:, 256:512]).astype(BF16), gb_ref[...]), n_chunks)
    k = _token_shift(zk_ref, pk_ref, muk_ref, first)
    _stage(kk_s, k * kk_ref[...], n_chunks)
    _stage(k_s, k * (1.0 + (a - 1.0) * ka_ref[...]), n_chunks)
    _stage(r_s, _token_shift(zr_ref, pr_ref, mur_ref, first), n_chunks)
    _stage(v_s, _token_shift(zv_ref, pv_ref, muv_ref, first), n_chunks)

    ri = lax.broadcasted_iota(jnp.int32, (qrows, qrows), 0)
    ci = lax.broadcasted_iota(jnp.int32, (qrows, qrows), 1)
    shift = RW_CHUNK.bit_length() - 1
    same = (ri >> shift) == (ci >> shift)
    strict = same & (ci < ri)
    incl = same & (ci <= ri)
    tri_bd = jnp.where(incl, 1.0, 0.0).astype(BF16)
    ones_bd = jnp.where(same, 1.0, 0.0).astype(BF16)

    def chunk_body(c, carry):
        for q in range(RW_HEADS // RW_QUAD):
            rows = slice(q * qrows, (q + 1) * qrows)
            r = r_s[c, rows, :]
            kq = k_s[c, rows, :]
            v = v_s[c, rows, :]
            kkr = kk_s[c, rows, :]
            a_q = a_s[c, rows, :]
            lw = lw_s[c, rows, :]
            nrm = jnp.sqrt(jnp.sum(kkr * kkr, axis=-1, keepdims=True))
            kk = kkr / jnp.maximum(nrm, 1e-12)
            bt0 = kk * a_q
            hi = lw.astype(BF16)
            lo = (lw - hi.astype(F32)).astype(BF16)
            cum = _dot(tri_bd, hi) + _dot(tri_bd, lo)
            tot = _dot(ones_bd, hi) + _dot(ones_bd, lo)
            cum_ex = cum - (hi.astype(F32) + lo.astype(F32))
            e_neg = jnp.exp(-cum)
            al = kk * jnp.exp(cum_ex)
            rb = r * jnp.exp(cum)
            kh = (kq * e_neg).astype(BF16)
            bh = (bt0 * e_neg).astype(BF16)
            e_tail = jnp.exp(tot - cum)
            kt = (kq * e_tail).astype(BF16)
            btl = (bt0 * e_tail).astype(BF16)
            pc = jnp.exp(tot)
            alb = al.astype(BF16)
            rbb = rb.astype(BF16)
            vb = v.astype(BF16)
            lmat = jnp.where(strict, _dg(alb, bh, NT_DIMS), 0.0).astype(BF16)
            a_ak = jnp.where(strict, _dg(alb, kh, NT_DIMS), 0.0).astype(BF16)
            a_rb = jnp.where(incl, _dg(rbb, bh, NT_DIMS), 0.0).astype(BF16)
            a_rk = jnp.where(incl, _dg(rbb, kh, NT_DIMS), 0.0).astype(BF16)
            xu = _dot(a_ak, vb)
            xw = al
            xu = xu - _dot(lmat, xu.astype(BF16))
            xw = xw - _dot(lmat, xw.astype(BF16))
            pw = lmat
            step = 2
            while step < RW_CHUNK:
                pw = _dot(pw, pw).astype(BF16)
                xu = xu + _dot(pw, xu.astype(BF16))
                xw = xw + _dot(pw, xw.astype(BF16))
                step *= 2
            u0b = xu.astype(BF16)
            w1b = xw.astype(BF16)
            y0 = _dot(a_rk, vb) - _dot(a_rb, u0b)
            r1 = (rb - _dot(a_rb, w1b)).astype(BF16)
            for hh in range(RW_QUAD):
                head = q * RW_QUAD + hh
                hs = slice(hh * RW_CHUNK, (hh + 1) * RW_CHUNK)
                srow = slice(head * RW_HEAD_DIM, (head + 1) * RW_HEAD_DIM)
                st = st_s[srow, :]
                stb = st.astype(BF16)
                y = _dg(r1[hs], stb, NT_DIMS) + y0[hs]
                u = _dg(w1b[hs], stb, NT_DIMS) + xu[hs]
                st_s[srow, :] = (st * pc[hs][0:1, :]
                                 + _dg(vb[hs], kt[hs], TN_DIMS)
                                 - _dg(u.astype(BF16), btl[hs], TN_DIMS))
                mu = jnp.mean(y, axis=-1, keepdims=True)
                d = y - mu
                var = jnp.mean(d * d, axis=-1, keepdims=True)
                yn = d * lax.rsqrt(var + RW_GN_EPS) * lng_ref[head] + lnb_ref[head]
                bonus = jnp.sum(r[hs] * kq[hs] * rk_ref[head], axis=-1, keepdims=True) * v[hs]
                orow = slice(q * qrows + hh * RW_CHUNK, q * qrows + (hh + 1) * RW_CHUNK)
                y_s[c, orow, :] = (yn + bonus) * g_s[c, orow, :]
        return carry

    lax.fori_loop(0, n_chunks, chunk_body, 0)

    for c in range(n_chunks):
        for h in range(RW_HEADS):
            o_ref[0, c * RW_CHUNK:(c + 1) * RW_CHUNK, h * RW_HEAD_DIM:(h + 1) * RW_HEAD_DIM] = (
                y_s[c, h * RW_CHUNK:(h + 1) * RW_CHUNK, :].astype(o_ref.dtype))


def _rwkv(z3, mu_r, mu_k, mu_v, mu_m, w0, wb, a0, ab, gb, kk_p, ka_p, rk_p, lng, lnb):
    bsz, seq, _ = z3.shape
    tb = RW_TB
    n_chunks = tb // RW_CHUNK
    cb = COL_R // RW_WIDTH

    def zspec(col_blk, width):
        return pl.BlockSpec((1, tb, width), lambda b, i: (b, i, col_blk))

    def pspec(col_blk, width):
        return pl.BlockSpec((1, 8, width), lambda b, i: (b, jnp.maximum(i * (tb // 8) - 1, 0), col_blk))

    def full(shape):
        return pl.BlockSpec(shape, lambda b, i: (0,) * len(shape))

    stage = pltpu.VMEM((n_chunks, RW_HEADS * RW_CHUNK, RW_HEAD_DIM), F32)
    return pl.pallas_call(
        functools.partial(_rwkv_kernel, n_chunks=n_chunks),
        grid=(bsz, seq // tb),
        in_specs=[zspec(cb, RW_WIDTH), zspec(cb + 1, RW_WIDTH), zspec(cb + 2, RW_WIDTH),
                  zspec(COL_MISC // MISC_W, MISC_W),
                  pspec(cb, RW_WIDTH), pspec(cb + 1, RW_WIDTH), pspec(cb + 2, RW_WIDTH),
                  pspec(COL_MISC // MISC_W, MISC_W),
                  full((1, RW_WIDTH)), full((1, RW_WIDTH)), full((1, RW_WIDTH)), full((1, MISC_W)),
                  full((1, RW_WIDTH)), full((128, RW_WIDTH)), full((1, RW_WIDTH)), full((128, RW_WIDTH)),
                  full((256, RW_WIDTH)),
                  full((1, RW_WIDTH)), full((1, RW_WIDTH)),
                  full((RW_HEADS, 1, RW_HEAD_DIM)), full((RW_HEADS, 1, RW_HEAD_DIM)),
                  full((RW_HEADS, 1, RW_HEAD_DIM))],
        out_specs=pl.BlockSpec((1, tb, RW_WIDTH), lambda b, i: (b, i, 0)),
        out_shape=jax.ShapeDtypeStruct((bsz, seq, RW_WIDTH), BF16),
        scratch_shapes=[stage] * 8 + [pltpu.VMEM((RW_HEADS * RW_HEAD_DIM, RW_HEAD_DIM), F32)],
        compiler_params=_params("parallel", "arbitrary"),
        name="rwkv7",
    )(z3, z3, z3, z3, z3, z3, z3, z3, mu_r, mu_k, mu_v, mu_m, w0, wb, a0, ab, gb,
      kk_p, ka_p, rk_p, lng, lnb)


FX_CUM_BLOCK = 256


def _fox_cum_kernel(zf_ref, fb_ref, col_ref, row_ref, *, seq):
    n = FX_CUM_BLOCK
    ri = lax.broadcasted_iota(jnp.int32, (n, n), 0)
    ci = lax.broadcasted_iota(jnp.int32, (n, n), 1)
    tri = jnp.where(ci <= ri, 1.0, 0.0).astype(BF16)
    carry = jnp.zeros((1, 128), F32)
    for j in range(seq // n):
        rs = slice(j * n, (j + 1) * n)
        x = zf_ref[0, rs, :] + fb_ref[...]
        lf = jnp.minimum(x, 0.0) - jnp.log(1.0 + jnp.exp(-jnp.abs(x)))
        hi = lf.astype(BF16)
        r1 = lf - hi.astype(F32)
        mid = r1.astype(BF16)
        lo = (r1 - mid.astype(F32)).astype(BF16)
        c = _dot(tri, hi) + _dot(tri, mid) + _dot(tri, lo) + carry
        col_ref[0, rs, :] = c
        row_ref[0, :, rs] = c.T[0:FX_HEADS, :]
        carry = c[n - 1:n, :]


def _fox_cum(z3, fb_pad):
    bsz, seq, _ = z3.shape
    return pl.pallas_call(
        functools.partial(_fox_cum_kernel, seq=seq),
        grid=(bsz,),
        in_specs=[pl.BlockSpec((1, seq, 128), lambda b: (b, 0, (COL_MISC + MISC_F_LANE) // 128)),
                  pl.BlockSpec((1, 128), lambda b: (0, 0))],
        out_specs=[pl.BlockSpec((1, seq, 128), lambda b: (b, 0, 0)),
                   pl.BlockSpec((1, FX_HEADS, seq), lambda b: (b, 0, 0))],
        out_shape=[jax.ShapeDtypeStruct((bsz, seq, 128), F32),
                   jax.ShapeDtypeStruct((bsz, FX_HEADS, seq), F32)],
        compiler_params=_params("parallel"),
        name="fox_cum",
    )(z3, fb_pad)


def _fox_kernel(q_ref, k_ref, v_ref, cc_ref, cr_ref, o_ref, *, tq, tk):
    i = pl.program_id(1)
    scale = FX_HEAD_DIM ** -0.5
    rows = i * tq + lax.broadcasted_iota(jnp.int32, (tq, tk), 0)
    col0 = lax.broadcasted_iota(jnp.int32, (tq, tk), 1)
    n_kv = (i * tq + tq + tk - 1) // tk
    for h in range(FX_HEADS):
        ls = slice(h * FX_HEAD_DIM, (h + 1) * FX_HEAD_DIM)
        qh = (q_ref[0, :, ls] * scale).astype(BF16)
        cq = cc_ref[0, :, h:h + 1]

        def body(j, carry, ls=ls, qh=qh, cq=cq, h=h):
            m_prev, l_prev, acc = carry
            ks = pl.multiple_of(j * tk, tk)
            kh = k_ref[0, pl.ds(ks, tk), ls].astype(BF16)
            vh = v_ref[0, pl.ds(ks, tk), ls].astype(BF16)
            ck = cr_ref[0, h, pl.ds(j, 1), :]
            s = _dg(qh, kh, NT_DIMS) + cq - ck
            s = jnp.where(rows >= ks + col0, s, -1e30)
            m_new = jnp.maximum(m_prev, jnp.max(s, axis=-1, keepdims=True))
            p = jnp.exp(s - m_new)
            alpha = jnp.exp(m_prev - m_new)
            l_new = alpha * l_prev + jnp.sum(p, axis=-1, keepdims=True)
            acc = alpha * acc + _dot(p.astype(BF16), vh)
            return m_new, l_new, acc

        init = (jnp.full((tq, 1), -1e30, F32), jnp.zeros((tq, 1), F32), jnp.zeros((tq, FX_HEAD_DIM), F32))
        _, l_fin, acc = lax.fori_loop(0, n_kv, body, init)
        o_ref[0, :, ls] = (acc / l_fin).astype(o_ref.dtype)


def _fox(z3, cum_col, cum_row4, tq=256, tk=256):
    bsz, seq, _ = z3.shape
    cb = COL_FX // FX_WIDTH
    return pl.pallas_call(
        functools.partial(_fox_kernel, tq=tq, tk=tk),
        grid=(bsz, seq // tq),
        in_specs=[pl.BlockSpec((1, tq, FX_WIDTH), lambda b, i: (b, i, cb)),
                  pl.BlockSpec((1, seq, FX_WIDTH), lambda b, i: (b, 0, cb + 1)),
                  pl.BlockSpec((1, seq, FX_WIDTH), lambda b, i: (b, 0, cb + 2)),
                  pl.BlockSpec((1, tq, 128), lambda b, i: (b, i, 0)),
                  pl.BlockSpec((1, FX_HEADS, seq // tk, tk), lambda b, i: (b, 0, 0, 0))],
        out_specs=pl.BlockSpec((1, tq, FX_WIDTH), lambda b, i: (b, i, 0)),
        out_shape=jax.ShapeDtypeStruct((bsz, seq, FX_WIDTH), BF16),
        compiler_params=_params("parallel", "parallel"),
        name="fox_attn",
    )(z3, z3, z3, cum_col, cum_row4)


def _merge_kernel(ya_ref, yb_ref, yc_ref, pa_ref, pb_ref, pc_ref, g0_ref, g1_ref, g2_ref,
                  b0_ref, b1_ref, b2_ref, o_ref):
    acc = _sigmoid(g0_ref[...] + b0_ref[...]) * _dot(ya_ref[...], pa_ref[...])
    acc = acc + _sigmoid(g1_ref[...] + b1_ref[...]) * _dot(yb_ref[...], pb_ref[...])
    acc = acc + _sigmoid(g2_ref[...] + b2_ref[...]) * _dot(yc_ref[...], pc_ref[...])
    o_ref[...] = acc.astype(o_ref.dtype)


def _merge(ya, yb, yc, pa, pb, pc, z, gate_b, tm=1024, tn=512):
    m = ya.shape[0]
    gb0 = COL_GATE // tn
    per = D_MODEL // tn

    def yspec(width):
        return pl.BlockSpec((tm, width), lambda i, j: (i, 0))

    def wspec(width):
        return pl.BlockSpec((width, tn), lambda i, j: (0, j))

    def gspec(k):
        return pl.BlockSpec((tm, tn), lambda i, j: (i, gb0 + k * per + j))

    def bspec(k):
        return pl.BlockSpec((1, tn), lambda i, j: (0, k * per + j))

    return pl.pallas_call(
        _merge_kernel,
        grid=(m // tm, D_MODEL // tn),
        in_specs=[yspec(GM_WIDTH), yspec(RW_WIDTH), yspec(FX_WIDTH),
                  wspec(GM_WIDTH), wspec(RW_WIDTH), wspec(FX_WIDTH),
                  gspec(0), gspec(1), gspec(2), bspec(0), bspec(1), bspec(2)],
        out_specs=pl.BlockSpec((tm, tn), lambda i, j: (i, j)),
        out_shape=jax.ShapeDtypeStruct((m, D_MODEL), BF16),
        compiler_params=_params("parallel", "parallel"),
        name="merge",
    )(ya, yb, yc, pa, pb, pc, z, z, z, gate_b, gate_b, gate_b)


def _out_ln_kernel(m_ref, w_ref, x_ref, g_ref, b_ref, o32_ref, o16_ref):
    h = DEEPNORM_ALPHA * x_ref[...] + _dot(m_ref[...], w_ref[...])
    y = _layer_norm(h, g_ref[...], b_ref[...], LN_EPS)
    o32_ref[...] = y
    o16_ref[...] = y.astype(BF16)


def _out_ln(merged, w_out, x, g, b, tm=512):
    m = x.shape[0]
    row = pl.BlockSpec((tm, D_MODEL), lambda i: (i, 0))
    vec = pl.BlockSpec((1, D_MODEL), lambda i: (0, 0))
    return pl.pallas_call(
        _out_ln_kernel,
        grid=(m // tm,),
        in_specs=[row, pl.BlockSpec((D_MODEL, D_MODEL), lambda i: (0, 0)), row, vec, vec],
        out_specs=[row, row],
        out_shape=[jax.ShapeDtypeStruct((m, D_MODEL), F32), jax.ShapeDtypeStruct((m, D_MODEL), BF16)],
        compiler_params=_params("parallel"),
        name="out_ln",
    )(merged, w_out, x, g, b)


def _router_kernel(x_ref, w_ref, b_ref, o_ref):
    logits = jnp.dot(x_ref[...], w_ref[...], preferred_element_type=F32,
                     precision=lax.Precision.HIGHEST) + b_ref[...]
    lane = lax.broadcasted_iota(jnp.int32, logits.shape, 1)
    logits = jnp.where(lane < MOE_EXPERTS, logits, -1e30)
    m1 = jnp.max(logits, axis=-1, keepdims=True)
    i1 = jnp.min(jnp.where(logits == m1, lane, 128), axis=-1, keepdims=True)
    rest = jnp.where(lane == i1, -1e30, logits)
    m2 = jnp.max(rest, axis=-1, keepdims=True)
    i2 = jnp.min(jnp.where(rest == m2, lane, 128), axis=-1, keepdims=True)
    e2 = jnp.exp(m2 - m1)
    w1 = 1.0 / (1.0 + e2)
    o_ref[...] = jnp.where(lane == i1, w1, 0.0) + jnp.where(lane == i2, e2 * w1, 0.0)


def _router(x1, w_pad, b_pad, tm=1024):
    m = x1.shape[0]
    return pl.pallas_call(
        _router_kernel,
        grid=(m // tm,),
        in_specs=[pl.BlockSpec((tm, D_MODEL), lambda i: (i, 0)),
                  pl.BlockSpec((D_MODEL, 128), lambda i: (0, 0)),
                  pl.BlockSpec((1, 128), lambda i: (0, 0))],
        out_specs=pl.BlockSpec((tm, 128), lambda i: (i, 0)),
        out_shape=jax.ShapeDtypeStruct((m, 128), F32),
        compiler_params=_params("parallel"),
        name="router",
    )(x1, w_pad, b_pad)


def _ffn_kernel(x_ref, c_ref, w1_ref, w3_ref, w2_ref, o_ref):
    @pl.when((pl.program_id(1) == 0) & (pl.program_id(2) == 0))
    def _():
        o_ref[...] = jnp.zeros_like(o_ref)

    x = x_ref[...]
    g = _dot(x, w1_ref[0])
    u = _dot(x, w3_ref[0])
    h = (g * _sigmoid(g)) * u * c_ref[0]
    o_ref[...] += _dot(h.astype(BF16), w2_ref[0])


def _ffn(xb, comb, w1, w3, w2, tm=1024, tf=512):
    m = xb.shape[0]
    n_e, _, f = w1.shape
    return pl.pallas_call(
        _ffn_kernel,
        grid=(m // tm, n_e, f // tf),
        in_specs=[pl.BlockSpec((tm, D_MODEL), lambda i, e, j: (i, 0)),
                  pl.BlockSpec((1, tm, 1), lambda i, e, j: (e, i, 0)),
                  pl.BlockSpec((1, D_MODEL, tf), lambda i, e, j: (e, 0, j)),
                  pl.BlockSpec((1, D_MODEL, tf), lambda i, e, j: (e, 0, j)),
                  pl.BlockSpec((1, tf, D_MODEL), lambda i, e, j: (e, j, 0))],
        out_specs=pl.BlockSpec((tm, D_MODEL), lambda i, e, j: (i, 0)),
        out_shape=jax.ShapeDtypeStruct((m, D_MODEL), F32),
        compiler_params=_params("parallel", "arbitrary", "arbitrary"),
        name="swiglu",
    )(xb, comb, w1, w3, w2)


def _ple_kernel(x_ref, wg_ref, bg_ref, p_ref, wp_ref, o_ref):
    gate = _sigmoid(_dot(x_ref[...], wg_ref[...]) + bg_ref[...])
    o_ref[...] = gate * _dot(p_ref[...].astype(BF16), wp_ref[...])


def _ple(x1b, wg, bg, p, wp, tm=1024, tn=1024):
    m = x1b.shape[0]
    return pl.pallas_call(
        _ple_kernel,
        grid=(m // tm, D_MODEL // tn),
        in_specs=[pl.BlockSpec((tm, D_MODEL), lambda i, j: (i, 0)),
                  pl.BlockSpec((D_MODEL, tn), lambda i, j: (0, j)),
                  pl.BlockSpec((1, tn), lambda i, j: (0, j)),
                  pl.BlockSpec((tm, PLE_DIM), lambda i, j: (i, 0)),
                  pl.BlockSpec((PLE_DIM, tn), lambda i, j: (0, j))],
        out_specs=pl.BlockSpec((tm, tn), lambda i, j: (i, j)),
        out_shape=jax.ShapeDtypeStruct((m, D_MODEL), F32),
        compiler_params=_params("parallel", "parallel"),
        name="ple",
    )(x1b, wg, bg, p, wp)


def _final_ln_kernel(x1_ref, ff_ref, ple_ref, g_ref, b_ref, o32_ref, o16_ref):
    h = DEEPNORM_ALPHA * x1_ref[...] + ff_ref[...] + ple_ref[...]
    y = _layer_norm(h, g_ref[...], b_ref[...], LN_EPS)
    o32_ref[...] = y
    o16_ref[...] = y.astype(BF16)


def _final_ln(x1, ff, ple, g, b, tm=512):
    m = x1.shape[0]
    row = pl.BlockSpec((tm, D_MODEL), lambda i: (i, 0))
    vec = pl.BlockSpec((1, D_MODEL), lambda i: (0, 0))
    return pl.pallas_call(
        _final_ln_kernel,
        grid=(m // tm,),
        in_specs=[row, row, row, vec, vec],
        out_specs=[row, row],
        out_shape=[jax.ShapeDtypeStruct((m, D_MODEL), F32), jax.ShapeDtypeStruct((m, D_MODEL), BF16)],
        compiler_params=_params("parallel"),
        name="final_ln",
    )(x1, ff, ple, g, b)


def _pack_w_in(w_in):
    n_l, d, _ = w_in.shape

    def zeros(n):
        return jnp.zeros((n_l, d, n), w_in.dtype)

    o = 1024 + 3 * RW_WIDTH
    xw = w_in[..., o:o + RW_LORA_W]
    xa = w_in[..., o + RW_LORA_W:o + RW_LORA_W + RW_LORA_A]
    xg = w_in[..., o + RW_LORA_W + RW_LORA_A:o + RW_LORA_W + RW_LORA_A + RW_LORA_G]
    o2 = o + RW_LORA_W + RW_LORA_A + RW_LORA_G
    fx = w_in[..., o2:o2 + 3 * FX_WIDTH]
    f = w_in[..., o2 + 3 * FX_WIDTH:o2 + 3 * FX_WIDTH + FX_HEADS]
    gates = w_in[..., o2 + 3 * FX_WIDTH + FX_HEADS:]
    misc = jnp.concatenate([xw, xa, f, zeros(128 - FX_HEADS), xg, zeros(256 - RW_LORA_G)], axis=-1)
    packed = jnp.concatenate([w_in[..., :o], misc, fx, gates], axis=-1)
    assert packed.shape[-1] == IN_COLS_PACKED
    return packed.astype(BF16)


def _pad_rows(w, lo, total):
    n_l, r, n = w.shape
    return jnp.concatenate([jnp.zeros((n_l, lo, n), w.dtype), w,
                            jnp.zeros((n_l, total - lo - r, n), w.dtype)], axis=1)


def kernel(x, p, w_in, gate_b, gm_ln_g, gm_ln_b, gm_ws, gm_bs, rw_mu, rw_w0, rw_wb, rw_a0, rw_ab, rw_gb, rw_kk, rw_ka, rw_rk, rw_lnx_g, rw_lnx_b, fx_fb, proj_a, proj_b, proj_c, w_out, ln1_g, ln1_b, ffn_w1, ffn_w3, ffn_w2, moe_router, moe_router_b, moe_w1, moe_w3, moe_w2, ple_gate_w, ple_gate_b, ple_proj, ln2_g, ln2_b):
    bsz, seq, d = x.shape
    m = bsz * seq
    n_l = w_in.shape[0]

    w_cat = _pack_w_in(w_in)
    mu_r = rw_mu[:, None, 0:RW_WIDTH]
    mu_k = rw_mu[:, None, RW_WIDTH:2 * RW_WIDTH]
    mu_v = rw_mu[:, None, 2 * RW_WIDTH:3 * RW_WIDTH]
    o = 3 * RW_WIDTH
    zl = lambda n: jnp.zeros((n_l, n), F32)
    mu_m = jnp.concatenate([rw_mu[:, o:o + 128], zl(128), rw_mu[:, o + 128:o + 128 + RW_LORA_G],
                            zl(256 - RW_LORA_G)], axis=-1)[:, None, :]
    wb_pad = _pad_rows(rw_wb, 0, 128).astype(BF16)
    ab_pad = _pad_rows(rw_ab, RW_LORA_W, 128).astype(BF16)
    gb_pad = _pad_rows(rw_gb, 0, 256).astype(BF16)
    fb_pad = jnp.concatenate([fx_fb, zl(128 - FX_HEADS)], axis=-1)[:, None, :]
    head3 = lambda t: t.reshape(n_l, RW_HEADS, 1, RW_HEAD_DIM)
    gm_bs3 = gm_bs[..., None]
    router_w = jnp.concatenate([moe_router, jnp.zeros(moe_router.shape[:2] + (128 - MOE_EXPERTS,), F32)], axis=-1)
    router_b = jnp.concatenate([moe_router_b, jnp.zeros((moe_router_b.shape[0], 128 - MOE_EXPERTS), F32)],
                               axis=-1)[:, None, :]
    bf = lambda t: t.astype(BF16)
    proj_a16, proj_b16, proj_c16, w_out16 = bf(proj_a), bf(proj_b), bf(proj_c), bf(w_out)
    ffn_w1_16, ffn_w3_16, ffn_w2_16 = bf(ffn_w1), bf(ffn_w3), bf(ffn_w2)
    moe_w1_16, moe_w3_16, moe_w2_16 = bf(moe_w1), bf(moe_w3), bf(moe_w2)
    ple_gw16, ple_pw16 = bf(ple_gate_w), bf(ple_proj)
    ones_comb = jnp.ones((1, m, 1), F32)

    x32 = x.reshape(m, d)
    x16 = x32.astype(BF16)
    for i in range(n_l):
        z = _matmul(x16, w_cat[i], F32, 1024, 1024)
        z3 = z.reshape(bsz, seq, IN_COLS_PACKED)
        y_a = _gmlp(z, gm_ln_g[i][None], gm_ln_b[i][None], gm_ws[i], gm_bs3[i])
        y_b = _rwkv(z3, mu_r[i], mu_k[i], mu_v[i], mu_m[i], rw_w0[i][None], wb_pad[i], rw_a0[i][None],
                    ab_pad[i], gb_pad[i], rw_kk[i][None], rw_ka[i][None], head3(rw_rk)[i],
                    head3(rw_lnx_g)[i], head3(rw_lnx_b)[i]).reshape(m, RW_WIDTH)
        cum_col, cum_row = _fox_cum(z3, fb_pad[i])
        y_c = _fox(z3, cum_col, cum_row.reshape(bsz, FX_HEADS, seq // 256, 256)).reshape(m, FX_WIDTH)
        merged = _merge(y_a, y_b, y_c, proj_a16[i], proj_b16[i], proj_c16[i], z, gate_b[i][None])
        x1, x1b = _out_ln(merged, w_out16[i], x32, ln1_g[i][None], ln1_b[i][None])
        j = i // 2
        if i % 2 == 0:
            ff = _ffn(x1b, ones_comb, ffn_w1_16[j:j + 1], ffn_w3_16[j:j + 1], ffn_w2_16[j:j + 1])
        else:
            comb = _router(x1, router_w[j], router_b[j])[:, :MOE_EXPERTS]
            ff = _ffn(x1b, comb.T[:, :, None], moe_w1_16[j], moe_w3_16[j], moe_w2_16[j])
        ple = _ple(x1b, ple_gw16[i], ple_gate_b[i][None], p[i].reshape(m, PLE_DIM), ple_pw16[i])
        x32, x16 = _final_ln(x1, ff, ple, ln2_g[i][None], ln2_b[i][None])
    return x32.reshape(bsz, seq, d)
```

```python
import functools
import math

import jax
import jax.numpy as jnp
from jax import lax
from jax.experimental import pallas as pl
from jax.experimental.pallas import tpu as pltpu

F32 = jnp.float32
BF16 = jnp.bfloat16

D_MODEL = 2048
DEPTH = 4
PLE_DIM = 256
GM_CHUNK = 128
GM_GROUPS = 4
GM_WIDTH = 512
RW_HEADS = 16
RW_HEAD_DIM = 64
RW_WIDTH = 1024
RW_LORA_W = 64
RW_LORA_A = 64
RW_LORA_G = 160
RW_GN_EPS = 64e-5
FX_HEADS = 8
FX_HEAD_DIM = 64
FX_WIDTH = 512
MOE_EXPERTS = 8
DEEPNORM_ALPHA = (2.0 * DEPTH) ** 0.25
LN_EPS = 1e-5

COL_A = 0
COL_R = 1024
COL_MISC = 4096
COL_FX = 4608
COL_GATE = 6144
IN_COLS_PACKED = 12288
MISC_W = 512
MISC_F_LANE = 128

RW_CHUNK = 64
RW_QUAD = 4
RW_TB = 256
LOGW_SCALE = math.exp(-0.5)

VMEM_LIMIT = 56 * 1024 * 1024

NT_DIMS = (((1,), (1,)), ((), ()))
TN_DIMS = (((0,), (0,)), ((), ()))


def _params(*sem):
    return pltpu.CompilerParams(dimension_semantics=sem, vmem_limit_bytes=VMEM_LIMIT)


def _dot(a, b):
    return jnp.dot(a, b, preferred_element_type=F32)


def _dg(a, b, dims):
    return lax.dot_general(a, b, dims, preferred_element_type=F32)


def _sigmoid(x):
    return 1.0 / (1.0 + jnp.exp(-x))


def _layer_norm(h, g, b, eps):
    mu = jnp.mean(h, axis=-1, keepdims=True)
    d = h - mu
    var = jnp.mean(d * d, axis=-1, keepdims=True)
    return d * lax.rsqrt(var + eps) * g + b


def _mm_kernel(x_ref, w_ref, o_ref):
    o_ref[...] = _dot(x_ref[...], w_ref[...]).astype(o_ref.dtype)


def _mm_gate_kernel(x_ref, w_ref, b_ref, o_ref):
    o_ref[...] = _sigmoid(_dot(x_ref[...], w_ref[...]) + b_ref[...]).astype(o_ref.dtype)


def _matmul(x, w, out_dtype, tm, tn, name, gate_bias=None):
    m, k = x.shape
    n = w.shape[1]
    in_specs = [pl.BlockSpec((tm, k), lambda i, j: (i, 0)),
                pl.BlockSpec((k, tn), lambda i, j: (0, j))]
    args = (x, w)
    body = _mm_kernel
    if gate_bias is not None:
        in_specs.append(pl.BlockSpec((1, tn), lambda i, j: (0, j)))
        args = (x, w, gate_bias)
        body = _mm_gate_kernel
    return pl.pallas_call(
        body,
        grid=(m // tm, n // tn),
        in_specs=in_specs,
        out_specs=pl.BlockSpec((tm, tn), lambda i, j: (i, j)),
        out_shape=jax.ShapeDtypeStruct((m, n), out_dtype),
        compiler_params=_params("parallel", "parallel"),
        name=name,
    )(*args)


def _gmlp_kernel(z_ref, g_ref, b_ref, ws_ref, bs_ref, o_ref, *, rows):
    z = z_ref[...]
    za = 0.5 * z * (1.0 + jnp.tanh(math.sqrt(2.0 / math.pi) * (z + 0.044715 * (z * z * z))))
    u = za[:, :GM_WIDTH]
    v = _layer_norm(za[:, GM_WIDTH:], g_ref[...], b_ref[...], LN_EPS).astype(BF16)
    t_idx = lax.broadcasted_iota(jnp.int32, (GM_CHUNK, GM_CHUNK), 0)
    s_idx = lax.broadcasted_iota(jnp.int32, (GM_CHUNK, GM_CHUNK), 1)
    causal = t_idx >= s_idx
    for g in range(GM_GROUPS):
        w = jnp.where(causal, ws_ref[g], 0.0).astype(BF16)
        cs = slice(g * GM_CHUNK, (g + 1) * GM_CHUNK)
        for c in range(rows // GM_CHUNK):
            rs = slice(c * GM_CHUNK, (c + 1) * GM_CHUNK)
            s = _dot(w, v[rs, cs]) + bs_ref[g]
            o_ref[rs, cs] = (u[rs, cs] * s).astype(o_ref.dtype)


def _gmlp(z, ln_g, ln_b, ws, bs, rows=512):
    m = z.shape[0]
    return pl.pallas_call(
        functools.partial(_gmlp_kernel, rows=rows),
        grid=(m // rows,),
        in_specs=[pl.BlockSpec((rows, 2 * GM_WIDTH), lambda i: (i, COL_A // (2 * GM_WIDTH))),
                  pl.BlockSpec((1, GM_WIDTH), lambda i: (0, 0)),
                  pl.BlockSpec((1, GM_WIDTH), lambda i: (0, 0)),
                  pl.BlockSpec((GM_GROUPS, GM_CHUNK, GM_CHUNK), lambda i: (0, 0, 0)),
                  pl.BlockSpec((GM_GROUPS, GM_CHUNK, 1), lambda i: (0, 0, 0))],
        out_specs=pl.BlockSpec((rows, GM_WIDTH), lambda i: (i, 0)),
        out_shape=jax.ShapeDtypeStruct((m, GM_WIDTH), BF16),
        compiler_params=_params("parallel"),
        name="gmlp",
    )(z, ln_g, ln_b, ws, bs)


def _token_shift(z_ref, p_ref, mu_ref, first):
    h = z_ref[0]
    prev_row = jnp.where(first, 0.0, p_ref[0, 7:8, :])
    hp = pltpu.roll(h, 1, axis=0)
    row = lax.broadcasted_iota(jnp.int32, h.shape, 0)
    hp = jnp.where(row == 0, prev_row, hp)
    return h + (hp - h) * mu_ref[...]


def _stage(dst_ref, val, n_chunks):
    for c in range(n_chunks):
        for h in range(RW_HEADS):
            dst_ref[c, h * RW_CHUNK:(h + 1) * RW_CHUNK, 0:RW_HEAD_DIM] = (
                val[c * RW_CHUNK:(c + 1) * RW_CHUNK, h * RW_HEAD_DIM:(h + 1) * RW_HEAD_DIM])


def _rwkv_kernel(zr_ref, zk_ref, zv_ref, zm_ref, pr_ref, pk_ref, pv_ref, pm_ref,
                 mur_ref, muk_ref, muv_ref, mum_ref, w0_ref, wb_ref, a0_ref, ab_ref, gb_ref,
                 kk_ref, ka_ref, rk_ref, lng_ref, lnb_ref, o_ref,
                 r_s, k_s, v_s, kk_s, a_s, lw_s, g_s, y_s, st_s, *, n_chunks):
    first = pl.program_id(1) == 0
    qrows = RW_QUAD * RW_CHUNK
    nd = RW_HEAD_DIM

    @pl.when(first)
    def _():
        st_s[...] = jnp.zeros_like(st_s)
        v_s[...] = jnp.zeros_like(v_s)

    m = _token_shift(zm_ref, pm_ref, mum_ref, first)
    m01 = m[:, 0:128]
    lw_lin = _dot(jnp.tanh(m01).astype(BF16), wb_ref[...]) + w0_ref[...]
    _stage(lw_s, -LOGW_SCALE * _sigmoid(lw_lin), n_chunks)
    a = _sigmoid(a0_ref[...] + _dot(m01.astype(BF16), ab_ref[...]))
    _stage(a_s, a, n_chunks)
    _stage(g_s, _dot(_sigmoid(m[:, 256:512]).astype(BF16), gb_ref[...]), n_chunks)
    k = _token_shift(zk_ref, pk_ref, muk_ref, first)
    _stage(kk_s, k * kk_ref[...], n_chunks)
    _stage(k_s, k * (1.0 + (a - 1.0) * ka_ref[...]), n_chunks)
    _stage(r_s, _token_shift(zr_ref, pr_ref, mur_ref, first), n_chunks)
    _stage(v_s, _token_shift(zv_ref, pv_ref, muv_ref, first), n_chunks)

    ri = lax.broadcasted_iota(jnp.int32, (qrows, qrows), 0)
    ci = lax.broadcasted_iota(jnp.int32, (qrows, qrows), 1)
    shift = RW_CHUNK.bit_length() - 1
    same = (ri >> shift) == (ci >> shift)
    strict = same & (ci < ri)
    incl = same & (ci <= ri)
    tri_bd = jnp.where(incl, 1.0, 0.0).astype(BF16)
    ones_bd = jnp.where(same, 1.0, 0.0).astype(BF16)
    ones_sel = jnp.where(
        (lax.broadcasted_iota(jnp.int32, (qrows, RW_QUAD * 128), 0) >> shift)
        == (lax.broadcasted_iota(jnp.int32, (qrows, RW_QUAD * 128), 1) >> 7), 1.0, 0.0).astype(BF16)
    e2 = jnp.where(lax.broadcasted_iota(jnp.int32, (nd, 2 * nd), 1)
                   == lax.broadcasted_iota(jnp.int32, (nd, 2 * nd), 0) + nd, 1.0, 0.0).astype(BF16)

    def chunk_body(c, carry):
        quads = range(RW_HEADS // RW_QUAD)
        rows = [slice(q * qrows, (q + 1) * qrows) for q in quads]
        r = [r_s[c, rows[q], :] for q in quads]
        kq = [k_s[c, rows[q], :] for q in quads]
        vpad = [v_s[c, rows[q], :] for q in quads]
        vpb = [vpad[q].astype(BF16) for q in quads]
        kkr = [kk_s[c, rows[q], :] for q in quads]
        lw = [lw_s[c, rows[q], :] for q in quads]
        hi = [lw[q].astype(BF16) for q in quads]
        lo = [(lw[q] - hi[q].astype(F32)).astype(BF16) for q in quads]
        cum = [_dot(tri_bd, hi[q]) + _dot(tri_bd, lo[q]) for q in quads]
        tot = [_dot(ones_bd, hi[q]) + _dot(ones_bd, lo[q]) for q in quads]
        pc_t = [jnp.exp(_dg(hi[q], ones_sel, TN_DIMS) + _dg(lo[q], ones_sel, TN_DIMS)) for q in quads]
        nrm = [jnp.sqrt(jnp.sum(kkr[q] * kkr[q], axis=-1, keepdims=True)) for q in quads]
        kk = [kkr[q] / jnp.maximum(nrm[q], 1e-12) for q in quads]
        bt0 = [kk[q] * a_s[c, rows[q], :] for q in quads]
        cum_ex = [cum[q] - (hi[q].astype(F32) + lo[q].astype(F32)) for q in quads]
        e_neg = [jnp.exp(-cum[q]) for q in quads]
        alb = [(kk[q] * jnp.exp(cum_ex[q])).astype(BF16) for q in quads]
        rbb = [(r[q] * jnp.exp(cum[q])).astype(BF16) for q in quads]
        kh = [(kq[q] * e_neg[q]).astype(BF16) for q in quads]
        bh = [(bt0[q] * e_neg[q]).astype(BF16) for q in quads]
        e_tail = [jnp.exp(tot[q] - cum[q]) for q in quads]
        kt = [(kq[q] * e_tail[q]).astype(BF16) for q in quads]
        btl = [(bt0[q] * e_tail[q]).astype(BF16) for q in quads]
        lmat = [jnp.where(strict, _dg(alb[q], bh[q], NT_DIMS), 0.0).astype(BF16) for q in quads]
        a_ak = [jnp.where(strict, _dg(alb[q], kh[q], NT_DIMS), 0.0).astype(BF16) for q in quads]
        a_rb = [jnp.where(incl, _dg(rbb[q], bh[q], NT_DIMS), 0.0).astype(BF16) for q in quads]
        a_rk = [jnp.where(incl, _dg(rbb[q], kh[q], NT_DIMS), 0.0).astype(BF16) for q in quads]
        x = [_dot(a_ak[q], vpb[q]) + _dot(alb[q], e2) for q in quads]
        x = [x[q] - _dot(lmat[q], x[q].astype(BF16)) for q in quads]
        pw = lmat
        step = 2
        while step < RW_CHUNK:
            pw = [_dot(pw[q], pw[q]).astype(BF16) for q in quads]
            x = [x[q] + _dot(pw[q], x[q].astype(BF16)) for q in quads]
            step *= 2
        xb = [x[q].astype(BF16) for q in quads]
        yr = [_dot(a_rk[q], vpb[q]) - _dot(a_rb[q], xb[q]) + _dot(rbb[q], e2) for q in quads]
        yrb = [yr[q].astype(BF16) for q in quads]

        heads = [(q, hh) for q in quads for hh in range(RW_QUAD)]
        hsl = [slice(hh * RW_CHUNK, (hh + 1) * RW_CHUNK) for hh in range(RW_QUAD)]
        h_aug = [st_s[q * RW_QUAD + hh] for q, hh in heads]
        hb = [h.astype(BF16) for h in h_aug]
        ys = [_dot(yrb[q][hsl[hh]], hb[n]) + yr[q][hsl[hh], 0:nd] for n, (q, hh) in enumerate(heads)]
        us = [_dot(xb[q][hsl[hh]], hb[n]) + x[q][hsl[hh], 0:nd] for n, (q, hh) in enumerate(heads)]
        kv = [_dg(kt[q][hsl[hh]], vpb[q][hsl[hh], 0:nd], TN_DIMS) for q, hh in heads]
        bu = [_dg(btl[q][hsl[hh]], us[n].astype(BF16), TN_DIMS) for n, (q, hh) in enumerate(heads)]
        for n, (q, hh) in enumerate(heads):
            st_s[n, nd:2 * nd, :] = pc_t[q][:, hh * 128:hh * 128 + nd] * h_aug[n][nd:2 * nd, :] + kv[n] - bu[n]
        for n, (q, hh) in enumerate(heads):
            y = ys[n]
            mu = jnp.mean(y, axis=-1, keepdims=True)
            d = y - mu
            var = jnp.mean(d * d, axis=-1, keepdims=True)
            yn = d * lax.rsqrt(var + RW_GN_EPS) * lng_ref[n] + lnb_ref[n]
            hs = hsl[hh]
            bonus = jnp.sum(r[q][hs] * kq[q][hs] * rk_ref[n], axis=-1, keepdims=True) * vpad[q][hs, 0:nd]
            orow = slice(n * RW_CHUNK, (n + 1) * RW_CHUNK)
            y_s[c, orow, :] = (yn + bonus) * g_s[c, orow, :]
        return carry

    lax.fori_loop(0, n_chunks, chunk_body, 0)

    for c in range(n_chunks):
        for h in range(RW_HEADS):
            o_ref[0, c * RW_CHUNK:(c + 1) * RW_CHUNK, h * RW_HEAD_DIM:(h + 1) * RW_HEAD_DIM] = (
                y_s[c, h * RW_CHUNK:(h + 1) * RW_CHUNK, :].astype(o_ref.dtype))


def _rwkv(z3, zm3, mu_r, mu_k, mu_v, mu_m, w0, wb, a0, ab, gb, kk_p, ka_p, rk_p, lng, lnb):
    bsz, seq, _ = z3.shape
    tb = RW_TB
    n_chunks = tb // RW_CHUNK
    cb = COL_R // RW_WIDTH

    def zspec(col_blk, width):
        return pl.BlockSpec((1, tb, width), lambda b, i: (b, i, col_blk))

    def pspec(col_blk, width):
        return pl.BlockSpec((1, 8, width), lambda b, i: (b, jnp.maximum(i * (tb // 8) - 1, 0), col_blk))

    def full(shape):
        return pl.BlockSpec(shape, lambda b, i: (0,) * len(shape))

    stage = pltpu.VMEM((n_chunks, RW_HEADS * RW_CHUNK, RW_HEAD_DIM), F32)
    stage_wide = pltpu.VMEM((n_chunks, RW_HEADS * RW_CHUNK, 2 * RW_HEAD_DIM), F32)
    return pl.pallas_call(
        functools.partial(_rwkv_kernel, n_chunks=n_chunks),
        grid=(bsz, seq // tb),
        in_specs=[zspec(cb, RW_WIDTH), zspec(cb + 1, RW_WIDTH), zspec(cb + 2, RW_WIDTH),
                  zspec(0, MISC_W),
                  pspec(cb, RW_WIDTH), pspec(cb + 1, RW_WIDTH), pspec(cb + 2, RW_WIDTH),
                  pspec(0, MISC_W),
                  full((1, RW_WIDTH)), full((1, RW_WIDTH)), full((1, RW_WIDTH)), full((1, MISC_W)),
                  full((1, RW_WIDTH)), full((128, RW_WIDTH)), full((1, RW_WIDTH)), full((128, RW_WIDTH)),
                  full((256, RW_WIDTH)),
                  full((1, RW_WIDTH)), full((1, RW_WIDTH)),
                  full((RW_HEADS, 1, RW_HEAD_DIM)), full((RW_HEADS, 1, RW_HEAD_DIM)),
                  full((RW_HEADS, 1, RW_HEAD_DIM))],
        out_specs=pl.BlockSpec((1, tb, RW_WIDTH), lambda b, i: (b, i, 0)),
        out_shape=jax.ShapeDtypeStruct((bsz, seq, RW_WIDTH), BF16),
        scratch_shapes=[stage, stage, stage_wide, stage, stage, stage, stage, stage,
                        pltpu.VMEM((RW_HEADS, 2 * RW_HEAD_DIM, RW_HEAD_DIM), F32)],
        compiler_params=_params("parallel", "arbitrary"),
        name="rwkv7",
    )(z3, z3, z3, zm3, z3, z3, z3, zm3, mu_r, mu_k, mu_v, mu_m, w0, wb, a0, ab, gb,
      kk_p, ka_p, rk_p, lng, lnb)


FX_CUM_BLOCK = 256
FX_HEAD_GROUP = 4


def _fox_cum_kernel(zf_ref, fb_ref, col_ref, row_ref, *, seq):
    n = FX_CUM_BLOCK
    ri = lax.broadcasted_iota(jnp.int32, (n, n), 0)
    ci = lax.broadcasted_iota(jnp.int32, (n, n), 1)
    tri = jnp.where(ci <= ri, 1.0, 0.0).astype(BF16)
    carry = jnp.zeros((1, 128), F32)
    for j in range(seq // n):
        rs = slice(j * n, (j + 1) * n)
        x = zf_ref[0, rs, :] + fb_ref[...]
        lf = jnp.minimum(x, 0.0) - jnp.log(1.0 + jnp.exp(-jnp.abs(x)))
        hi = lf.astype(BF16)
        r1 = lf - hi.astype(F32)
        mid = r1.astype(BF16)
        lo = (r1 - mid.astype(F32)).astype(BF16)
        c = _dot(tri, hi) + _dot(tri, mid) + _dot(tri, lo) + carry
        col_ref[0, rs, :] = c
        row_ref[0, :, rs] = c.T[0:FX_HEADS, :]
        carry = c[n - 1:n, :]


def _fox_cum(zm3, fb_pad):
    bsz, seq, _ = zm3.shape
    return pl.pallas_call(
        functools.partial(_fox_cum_kernel, seq=seq),
        grid=(bsz,),
        in_specs=[pl.BlockSpec((1, seq, 128), lambda b: (b, 0, MISC_F_LANE // 128)),
                  pl.BlockSpec((1, 128), lambda b: (0, 0))],
        out_specs=[pl.BlockSpec((1, seq, 128), lambda b: (b, 0, 0)),
                   pl.BlockSpec((1, FX_HEADS, seq), lambda b: (b, 0, 0))],
        out_shape=[jax.ShapeDtypeStruct((bsz, seq, 128), F32),
                   jax.ShapeDtypeStruct((bsz, FX_HEADS, seq), F32)],
        compiler_params=_params("parallel"),
        name="fox_cum",
    )(zm3, fb_pad)


def _fox_kernel(q_ref, k_ref, v_ref, cc_ref, cr_ref, o_ref, m_s, l_s, acc_s, *, tq, tk):
    i = pl.program_id(1)
    scale = FX_HEAD_DIM ** -0.5
    rows = i * tq + lax.broadcasted_iota(jnp.int32, (tq, tk), 0)
    col0 = lax.broadcasted_iota(jnp.int32, (tq, tk), 1)
    n_kv = (i * tq + tq + tk - 1) // tk
    m_s[...] = jnp.full(m_s.shape, -1e30, F32)
    l_s[...] = jnp.zeros(l_s.shape, F32)
    acc_s[...] = jnp.zeros(acc_s.shape, F32)
    heads = [slice(h * FX_HEAD_DIM, (h + 1) * FX_HEAD_DIM) for h in range(FX_HEADS)]
    q_all = q_ref[0] * scale
    qs = [q_all[:, ls].astype(BF16) for ls in heads]
    cqs = [cc_ref[0, :, h:h + 1] for h in range(FX_HEADS)]

    def block(j, masked):
        ks = pl.multiple_of(j * tk, tk)
        kblk = k_ref[0, pl.ds(ks, tk), :].astype(BF16)
        vblk = v_ref[0, pl.ds(ks, tk), :].astype(BF16)
        for g0 in range(0, FX_HEADS, FX_HEAD_GROUP):
            hs = range(g0, g0 + FX_HEAD_GROUP)
            ss = {h: _dg(qs[h], kblk[:, heads[h]], NT_DIMS) for h in hs}
            ss = {h: ss[h] + cqs[h] - cr_ref[0, h, pl.ds(j, 1), :] for h in hs}
            if masked:
                keep = rows >= ks + col0
                ss = {h: jnp.where(keep, ss[h], -1e30) for h in hs}
            m_prev = {h: m_s[h] for h in hs}
            m_new = {h: jnp.maximum(m_prev[h], jnp.max(ss[h], axis=-1, keepdims=True)) for h in hs}
            ps = {h: jnp.exp(ss[h] - m_new[h]) for h in hs}
            alphas = {h: jnp.exp(m_prev[h] - m_new[h]) for h in hs}
            pvs = {h: _dot(ps[h].astype(BF16), vblk[:, heads[h]]) for h in hs}
            for h in hs:
                l_s[h] = alphas[h] * l_s[h] + jnp.sum(ps[h], axis=-1, keepdims=True)
                acc_s[h] = alphas[h] * acc_s[h] + pvs[h]
                m_s[h] = m_new[h]

    def visible_body(j, carry):
        block(j, masked=False)
        return carry

    lax.fori_loop(0, n_kv - 1, visible_body, 0)
    block(n_kv - 1, masked=True)
    for h, ls in enumerate(heads):
        o_ref[0, :, ls] = (acc_s[h] / l_s[h]).astype(o_ref.dtype)


def _fox(zfx3, cum_col, cum_row, tq=256, tk=512):
    bsz, seq, _ = zfx3.shape
    cb = 0
    assert tk % tq == 0 and seq % tk == 0
    cum_row4 = cum_row.reshape(bsz, FX_HEADS, seq // tk, tk)
    return pl.pallas_call(
        functools.partial(_fox_kernel, tq=tq, tk=tk),
        grid=(bsz, seq // tq),
        in_specs=[pl.BlockSpec((1, tq, FX_WIDTH), lambda b, i: (b, i, cb)),
                  pl.BlockSpec((1, seq, FX_WIDTH), lambda b, i: (b, 0, cb + 1)),
                  pl.BlockSpec((1, seq, FX_WIDTH), lambda b, i: (b, 0, cb + 2)),
                  pl.BlockSpec((1, tq, 128), lambda b, i: (b, i, 0)),
                  pl.BlockSpec((1, FX_HEADS, seq // tk, tk), lambda b, i: (b, 0, 0, 0))],
        out_specs=pl.BlockSpec((1, tq, FX_WIDTH), lambda b, i: (b, i, 0)),
        out_shape=jax.ShapeDtypeStruct((bsz, seq, FX_WIDTH), BF16),
        scratch_shapes=[pltpu.VMEM((FX_HEADS, tq, 1), F32), pltpu.VMEM((FX_HEADS, tq, 1), F32),
                        pltpu.VMEM((FX_HEADS, tq, FX_HEAD_DIM), F32)],
        compiler_params=_params("parallel", "parallel"),
        name="fox_attn",
    )(zfx3, zfx3, zfx3, cum_col, cum_row4)


def _merge_kernel(ya_ref, yb_ref, yc_ref, pa_ref, pb_ref, pc_ref, g0_ref, g1_ref, g2_ref, o_ref):
    acc = g0_ref[...].astype(F32) * _dot(ya_ref[...], pa_ref[...])
    acc = acc + g1_ref[...].astype(F32) * _dot(yb_ref[...], pb_ref[...])
    acc = acc + g2_ref[...].astype(F32) * _dot(yc_ref[...], pc_ref[...])
    o_ref[...] = acc.astype(o_ref.dtype)


def _merge(ya, yb, yc, pa, pb, pc, gates, tm=1024, tn=512):
    m = ya.shape[0]
    gb0 = 0
    per = D_MODEL // tn

    def yspec(width):
        return pl.BlockSpec((tm, width), lambda i, j: (i, 0))

    def wspec(width):
        return pl.BlockSpec((width, tn), lambda i, j: (0, j))

    def gspec(k):
        return pl.BlockSpec((tm, tn), lambda i, j: (i, gb0 + k * per + j))

    return pl.pallas_call(
        _merge_kernel,
        grid=(m // tm, D_MODEL // tn),
        in_specs=[yspec(GM_WIDTH), yspec(RW_WIDTH), yspec(FX_WIDTH),
                  wspec(GM_WIDTH), wspec(RW_WIDTH), wspec(FX_WIDTH),
                  gspec(0), gspec(1), gspec(2)],
        out_specs=pl.BlockSpec((tm, tn), lambda i, j: (i, j)),
        out_shape=jax.ShapeDtypeStruct((m, D_MODEL), BF16),
        compiler_params=_params("parallel", "parallel"),
        name="merge",
    )(ya, yb, yc, pa, pb, pc, gates, gates, gates)


def _out_ln_kernel(m_ref, w_ref, x_ref, g_ref, b_ref, o32_ref, o16_ref):
    h = DEEPNORM_ALPHA * x_ref[...] + _dot(m_ref[...], w_ref[...])
    y = _layer_norm(h, g_ref[...], b_ref[...], LN_EPS)
    o32_ref[...] = y
    o16_ref[...] = y.astype(BF16)


def _out_ln(merged, w_out, x, g, b, tm=512):
    m = x.shape[0]
    row = pl.BlockSpec((tm, D_MODEL), lambda i: (i, 0))
    vec = pl.BlockSpec((1, D_MODEL), lambda i: (0, 0))
    return pl.pallas_call(
        _out_ln_kernel,
        grid=(m // tm,),
        in_specs=[row, pl.BlockSpec((D_MODEL, D_MODEL), lambda i: (0, 0)), row, vec, vec],
        out_specs=[row, row],
        out_shape=[jax.ShapeDtypeStruct((m, D_MODEL), F32), jax.ShapeDtypeStruct((m, D_MODEL), BF16)],
        compiler_params=_params("parallel"),
        name="out_ln",
    )(merged, w_out, x, g, b)


def _router_kernel(x_ref, w_ref, b_ref, o_ref):
    logits = jnp.dot(x_ref[...], w_ref[...], preferred_element_type=F32,
                     precision=lax.Precision.HIGHEST) + b_ref[...]
    lane = lax.broadcasted_iota(jnp.int32, logits.shape, 1)
    logits = jnp.where(lane < MOE_EXPERTS, logits, -1e30)
    m1 = jnp.max(logits, axis=-1, keepdims=True)
    i1 = jnp.min(jnp.where(logits == m1, lane, 128), axis=-1, keepdims=True)
    rest = jnp.where(lane == i1, -1e30, logits)
    m2 = jnp.max(rest, axis=-1, keepdims=True)
    i2 = jnp.min(jnp.where(rest == m2, lane, 128), axis=-1, keepdims=True)
    e2 = jnp.exp(m2 - m1)
    w1 = 1.0 / (1.0 + e2)
    o_ref[...] = jnp.where(lane == i1, w1, 0.0) + jnp.where(lane == i2, e2 * w1, 0.0)


def _router(x1, w_pad, b_pad, tm=1024):
    m = x1.shape[0]
    return pl.pallas_call(
        _router_kernel,
        grid=(m // tm,),
        in_specs=[pl.BlockSpec((tm, D_MODEL), lambda i: (i, 0)),
                  pl.BlockSpec((D_MODEL, 128), lambda i: (0, 0)),
                  pl.BlockSpec((1, 128), lambda i: (0, 0))],
        out_specs=pl.BlockSpec((tm, 128), lambda i: (i, 0)),
        out_shape=jax.ShapeDtypeStruct((m, 128), F32),
        compiler_params=_params("parallel"),
        name="router",
    )(x1, w_pad, b_pad)


def _ffn_kernel(x_ref, c_ref, w1_ref, w3_ref, w2_ref, o_ref):
    @pl.when((pl.program_id(1) == 0) & (pl.program_id(2) == 0))
    def _():
        o_ref[...] = jnp.zeros_like(o_ref)

    x = x_ref[...]
    g = _dot(x, w1_ref[0])
    u = _dot(x, w3_ref[0])
    h = (g * _sigmoid(g)) * u * c_ref[0]
    o_ref[...] += _dot(h.astype(BF16), w2_ref[0])


def _ffn(xb, comb, w1, w3, w2, tm=1024, tf=512):
    m = xb.shape[0]
    n_e, _, f = w1.shape
    return pl.pallas_call(
        _ffn_kernel,
        grid=(m // tm, n_e, f // tf),
        in_specs=[pl.BlockSpec((tm, D_MODEL), lambda i, e, j: (i, 0)),
                  pl.BlockSpec((1, tm, 1), lambda i, e, j: (e, i, 0)),
                  pl.BlockSpec((1, D_MODEL, tf), lambda i, e, j: (e, 0, j)),
                  pl.BlockSpec((1, D_MODEL, tf), lambda i, e, j: (e, 0, j)),
                  pl.BlockSpec((1, tf, D_MODEL), lambda i, e, j: (e, j, 0))],
        out_specs=pl.BlockSpec((tm, D_MODEL), lambda i, e, j: (i, 0)),
        out_shape=jax.ShapeDtypeStruct((m, D_MODEL), F32),
        compiler_params=_params("parallel", "arbitrary", "arbitrary"),
        name="swiglu",
    )(xb, comb, w1, w3, w2)


MOE_BLK = 288
MOE_TM = 1024


def _moe_kernel(nblk_ref, x_ref, comb_ref, rank_ref, rankt_ref, w1_ref, w3_ref, w2_ref, o_ref,
                xs_s, acc_s, *, tm, n_f, col_split):
    i, e, f = pl.program_id(0), pl.program_id(1), pl.program_id(2)
    nb = nblk_ref[i, e]

    @pl.when((e == 0) & (f == 0))
    def _():
        o_ref[...] = jnp.zeros_like(o_ref)

    @pl.when(f == 0)
    def _():
        rank_row = rankt_ref[pl.ds(e, 1), :]
        slot = lax.broadcasted_iota(jnp.int32, (MOE_BLK, tm), 0)

        def gather(b, carry):
            onehot = jnp.where(rank_row == slot + b * MOE_BLK, 1.0, 0.0).astype(BF16)
            xs_s[b] = _dot(onehot, x_ref[...]).astype(BF16)
            acc_s[b] = jnp.zeros(acc_s.shape[1:], F32)
            return carry

        lax.fori_loop(0, nb, gather, 0)

    def compute(b, carry):
        xb = xs_s[b]
        g = _dot(xb, w1_ref[0])
        u = _dot(xb, w3_ref[0])
        h = (g * _sigmoid(g)) * u
        acc_s[b] += _dot(h.astype(BF16), w2_ref[0])
        return carry

    lax.fori_loop(0, nb, compute, 0)

    @pl.when(f == n_f - 1)
    def _():
        lane = lax.broadcasted_iota(jnp.int32, (tm, 128), 1)
        sel = lane == e
        c_col = jnp.sum(jnp.where(sel, comb_ref[...], 0.0), axis=-1, keepdims=True)
        r_col = jnp.sum(jnp.where(sel, rank_ref[...], 0), axis=-1, keepdims=True)
        slot = lax.broadcasted_iota(jnp.int32, (tm, MOE_BLK), 1)
        cw = D_MODEL // col_split

        def scatter(b, carry):
            onehot_t = jnp.where(r_col == slot + b * MOE_BLK, 1.0, 0.0).astype(BF16)
            for s in range(col_split):
                cs = slice(s * cw, (s + 1) * cw)
                o_ref[:, cs] += _dot(onehot_t, acc_s[b, :, cs].astype(BF16)) * c_col
            return carry

        lax.fori_loop(0, nb, scatter, 0)


def _moe(xb, comb, rank, rank_t, nblk, w1, w3, w2, tm=1024, tf=512):
    m = xb.shape[0]
    n_e, _, f = w1.shape
    n_f = f // tf
    grid_spec = pltpu.PrefetchScalarGridSpec(
        num_scalar_prefetch=1,
        grid=(m // tm, n_e, n_f),
        in_specs=[pl.BlockSpec((tm, D_MODEL), lambda i, e, j, nb: (i, 0)),
                  pl.BlockSpec((tm, 128), lambda i, e, j, nb: (i, 0)),
                  pl.BlockSpec((tm, 128), lambda i, e, j, nb: (i, 0)),
                  pl.BlockSpec((128, tm), lambda i, e, j, nb: (0, i)),
                  pl.BlockSpec((1, D_MODEL, tf), lambda i, e, j, nb: (e, 0, j)),
                  pl.BlockSpec((1, D_MODEL, tf), lambda i, e, j, nb: (e, 0, j)),
                  pl.BlockSpec((1, tf, D_MODEL), lambda i, e, j, nb: (e, j, 0))],
        out_specs=pl.BlockSpec((tm, D_MODEL), lambda i, e, j, nb: (i, 0)),
        scratch_shapes=[pltpu.VMEM((pl.cdiv(tm, MOE_BLK), MOE_BLK, D_MODEL), BF16),
                        pltpu.VMEM((pl.cdiv(tm, MOE_BLK), MOE_BLK, D_MODEL), F32)],
    )
    return pl.pallas_call(
        functools.partial(_moe_kernel, tm=tm, n_f=n_f, col_split=2),
        grid_spec=grid_spec,
        out_shape=jax.ShapeDtypeStruct((m, D_MODEL), F32),
        compiler_params=_params("parallel", "arbitrary", "arbitrary"),
        name="moe",
    )(nblk, xb, comb, rank, rank_t, w1, w3, w2)


def _moe_plan(comb, tm):
    m = comb.shape[0]
    routed = comb > 0.0
    r3 = routed.reshape(m // tm, tm, 128).astype(jnp.int32)
    rank = jnp.cumsum(r3, axis=1) - r3
    rank = jnp.where(r3 > 0, rank, -1).reshape(m, 128)
    count = jnp.sum(r3, axis=1)[:, :MOE_EXPERTS]
    nblk = (count + MOE_BLK - 1) // MOE_BLK
    return rank, rank.T, nblk.astype(jnp.int32)


def _ple_kernel(x_ref, wg_ref, bg_ref, p_ref, wp_ref, o_ref):
    gate = _sigmoid(_dot(x_ref[...], wg_ref[...]) + bg_ref[...])
    o_ref[...] = gate * _dot(p_ref[...].astype(BF16), wp_ref[...])


def _ple(x1b, wg, bg, p, wp, tm=1024, tn=1024):
    m = x1b.shape[0]
    return pl.pallas_call(
        _ple_kernel,
        grid=(m // tm, D_MODEL // tn),
        in_specs=[pl.BlockSpec((tm, D_MODEL), lambda i, j: (i, 0)),
                  pl.BlockSpec((D_MODEL, tn), lambda i, j: (0, j)),
                  pl.BlockSpec((1, tn), lambda i, j: (0, j)),
                  pl.BlockSpec((tm, PLE_DIM), lambda i, j: (i, 0)),
                  pl.BlockSpec((PLE_DIM, tn), lambda i, j: (0, j))],
        out_specs=pl.BlockSpec((tm, tn), lambda i, j: (i, j)),
        out_shape=jax.ShapeDtypeStruct((m, D_MODEL), F32),
        compiler_params=_params("parallel", "parallel"),
        name="ple",
    )(x1b, wg, bg, p, wp)


def _final_ln_kernel(x1_ref, ff_ref, ple_ref, g_ref, b_ref, o32_ref, o16_ref):
    h = DEEPNORM_ALPHA * x1_ref[...] + ff_ref[...] + ple_ref[...]
    y = _layer_norm(h, g_ref[...], b_ref[...], LN_EPS)
    o32_ref[...] = y
    o16_ref[...] = y.astype(BF16)


def _final_ln(x1, ff, ple, g, b, tm=512):
    m = x1.shape[0]
    row = pl.BlockSpec((tm, D_MODEL), lambda i: (i, 0))
    vec = pl.BlockSpec((1, D_MODEL), lambda i: (0, 0))
    return pl.pallas_call(
        _final_ln_kernel,
        grid=(m // tm,),
        in_specs=[row, row, row, vec, vec],
        out_specs=[row, row],
        out_shape=[jax.ShapeDtypeStruct((m, D_MODEL), F32), jax.ShapeDtypeStruct((m, D_MODEL), BF16)],
        compiler_params=_params("parallel"),
        name="final_ln",
    )(x1, ff, ple, g, b)


def _split_w_in(w_in):
    n_l, d, _ = w_in.shape

    def zeros(n):
        return jnp.zeros((n_l, d, n), BF16)

    o = 1024 + 3 * RW_WIDTH
    xw = w_in[..., o:o + RW_LORA_W].astype(BF16)
    xa = w_in[..., o + RW_LORA_W:o + RW_LORA_W + RW_LORA_A].astype(BF16)
    xg = w_in[..., o + RW_LORA_W + RW_LORA_A:o + RW_LORA_W + RW_LORA_A + RW_LORA_G].astype(BF16)
    o2 = o + RW_LORA_W + RW_LORA_A + RW_LORA_G
    fx = w_in[..., o2:o2 + 3 * FX_WIDTH].astype(BF16)
    f = w_in[..., o2 + 3 * FX_WIDTH:o2 + 3 * FX_WIDTH + FX_HEADS].astype(BF16)
    gates = w_in[..., o2 + 3 * FX_WIDTH + FX_HEADS:].astype(BF16)
    misc = jnp.concatenate([xw, xa, f, zeros(128 - FX_HEADS), xg, zeros(256 - RW_LORA_G)], axis=-1)
    assert misc.shape[-1] == MISC_W and gates.shape[-1] == 3 * D_MODEL
    return w_in[..., :o].astype(BF16), misc, fx, gates


def _pad_rows(w, lo, total):
    n_l, r, n = w.shape
    return jnp.concatenate([jnp.zeros((n_l, lo, n), w.dtype), w,
                            jnp.zeros((n_l, total - lo - r, n), w.dtype)], axis=1)


def kernel(x, p, w_in, gate_b, gm_ln_g, gm_ln_b, gm_ws, gm_bs, rw_mu, rw_w0, rw_wb, rw_a0, rw_ab, rw_gb, rw_kk, rw_ka, rw_rk, rw_lnx_g, rw_lnx_b, fx_fb, proj_a, proj_b, proj_c, w_out, ln1_g, ln1_b, ffn_w1, ffn_w3, ffn_w2, moe_router, moe_router_b, moe_w1, moe_w3, moe_w2, ple_gate_w, ple_gate_b, ple_proj, ln2_g, ln2_b):
    bsz, seq, d = x.shape
    m = bsz * seq
    n_l = w_in.shape[0]

    w_ar, w_misc, w_fx, w_gate = _split_w_in(w_in)
    mu_r = rw_mu[:, None, 0:RW_WIDTH]
    mu_k = rw_mu[:, None, RW_WIDTH:2 * RW_WIDTH]
    mu_v = rw_mu[:, None, 2 * RW_WIDTH:3 * RW_WIDTH]
    o = 3 * RW_WIDTH
    zl = lambda n: jnp.zeros((n_l, n), F32)
    mu_m = jnp.concatenate([rw_mu[:, o:o + 128], zl(128), rw_mu[:, o + 128:o + 128 + RW_LORA_G],
                            zl(256 - RW_LORA_G)], axis=-1)[:, None, :]
    wb_pad = _pad_rows(rw_wb, 0, 128).astype(BF16)
    ab_pad = _pad_rows(rw_ab, RW_LORA_W, 128).astype(BF16)
    gb_pad = _pad_rows(rw_gb, 0, 256).astype(BF16)
    fb_pad = jnp.concatenate([fx_fb, zl(128 - FX_HEADS)], axis=-1)[:, None, :]
    head3 = lambda t: t.reshape(n_l, RW_HEADS, 1, RW_HEAD_DIM)
    gm_bs3 = gm_bs[..., None]
    router_w = jnp.concatenate([moe_router, jnp.zeros(moe_router.shape[:2] + (128 - MOE_EXPERTS,), F32)], axis=-1)
    router_b = jnp.concatenate([moe_router_b, jnp.zeros((moe_router_b.shape[0], 128 - MOE_EXPERTS), F32)],
                               axis=-1)[:, None, :]
    bf = lambda t: t.astype(BF16)
    proj_a16, proj_b16, proj_c16, w_out16 = bf(proj_a), bf(proj_b), bf(proj_c), bf(w_out)
    ffn_w1_16, ffn_w3_16, ffn_w2_16 = bf(ffn_w1), bf(ffn_w3), bf(ffn_w2)
    moe_w1_16, moe_w3_16, moe_w2_16 = bf(moe_w1), bf(moe_w3), bf(moe_w2)
    ple_gw16, ple_pw16 = bf(ple_gate_w), bf(ple_proj)
    ones_comb = jnp.ones((1, m, 1), F32)

    x32 = x.reshape(m, d)
    x16 = x32.astype(BF16)
    for i in range(n_l):
        z = _matmul(x16, w_ar[i], F32, 1024, 1024, "in_proj_ar")
        zm = _matmul(x16, w_misc[i], F32, 1024, MISC_W, "in_proj_misc")
        zfx = _matmul(x16, w_fx[i], BF16, 1024, 3 * FX_WIDTH, "in_proj_fx")
        gates = _matmul(x16, w_gate[i], BF16, 1024, 1024, "in_proj_gate", gate_bias=gate_b[i][None])
        z3 = z.reshape(bsz, seq, z.shape[-1])
        zm3 = zm.reshape(bsz, seq, MISC_W)
        y_a = _gmlp(z, gm_ln_g[i][None], gm_ln_b[i][None], gm_ws[i], gm_bs3[i])
        y_b = _rwkv(z3, zm3, mu_r[i], mu_k[i], mu_v[i], mu_m[i], rw_w0[i][None], wb_pad[i], rw_a0[i][None],
                    ab_pad[i], gb_pad[i], rw_kk[i][None], rw_ka[i][None], head3(rw_rk)[i],
                    head3(rw_lnx_g)[i], head3(rw_lnx_b)[i]).reshape(m, RW_WIDTH)
        cum_col, cum_row = _fox_cum(zm3, fb_pad[i])
        y_c = _fox(zfx.reshape(bsz, seq, 3 * FX_WIDTH), cum_col, cum_row).reshape(m, FX_WIDTH)
        merged = _merge(y_a, y_b, y_c, proj_a16[i], proj_b16[i], proj_c16[i], gates)
        x1, x1b = _out_ln(merged, w_out16[i], x32, ln1_g[i][None], ln1_b[i][None])
        j = i // 2
        if i % 2 == 0:
            ff = _ffn(x1b, ones_comb, ffn_w1_16[j:j + 1], ffn_w3_16[j:j + 1], ffn_w2_16[j:j + 1])
        else:
            comb = _router(x1, router_w[j], router_b[j])
            rank, rank_t, nblk = _moe_plan(comb, MOE_TM)
            ff = _moe(x1b, comb, rank, rank_t, nblk, moe_w1_16[j], moe_w3_16[j], moe_w2_16[j], tm=MOE_TM)
        ple = _ple(x1b, ple_gw16[i], ple_gate_b[i][None], p[i].reshape(m, PLE_DIM), ple_pw16[i])
        x32, x16 = _final_ln(x1, ff, ple, ln2_g[i][None], ln2_b[i][None])
    return x32.reshape(bsz, seq, d)
```

```python
import functools
import math

import jax
import jax.numpy as jnp
from jax import lax
from jax.experimental import pallas as pl
from jax.experimental.pallas import tpu as pltpu

F32 = jnp.float32
BF16 = jnp.bfloat16

D_MODEL = 2048
DEPTH = 4
PLE_DIM = 256
GM_CHUNK = 128
GM_GROUPS = 4
GM_WIDTH = 512
RW_HEADS = 16
RW_HEAD_DIM = 64
RW_WIDTH = 1024
RW_LORA_W = 64
RW_LORA_A = 64
RW_LORA_G = 160
RW_GN_EPS = 64e-5
FX_HEADS = 8
FX_HEAD_DIM = 64
FX_WIDTH = 512
MOE_EXPERTS = 8
DEEPNORM_ALPHA = (2.0 * DEPTH) ** 0.25
LN_EPS = 1e-5

COL_A = 0
COL_R = 1024
COL_LORA = 4096
COL_FX = 4384
COL_F = 5920
COL_GATE = 5928
LORA_W = 512
MISC_W = LORA_W + 128
MISC_F_LANE = LORA_W

RW_CHUNK = 64
RW_QUAD = 4
RW_TB = 256
LOGW_SCALE = math.exp(-0.5)

VMEM_LIMIT = 56 * 1024 * 1024

NT_DIMS = (((1,), (1,)), ((), ()))
TN_DIMS = (((0,), (0,)), ((), ()))


def _params(*sem):
    return pltpu.CompilerParams(dimension_semantics=sem, vmem_limit_bytes=VMEM_LIMIT)


def _dot(a, b):
    return jnp.dot(a, b, preferred_element_type=F32)


def _dg(a, b, dims):
    return lax.dot_general(a, b, dims, preferred_element_type=F32)


def _sigmoid(x):
    return 1.0 / (1.0 + jnp.exp(-x))


def _layer_norm(h, g, b, eps):
    mu = jnp.mean(h, axis=-1, keepdims=True)
    d = h - mu
    var = jnp.mean(d * d, axis=-1, keepdims=True)
    return d * lax.rsqrt(var + eps) * g + b


def _mm_kernel(x_ref, w_ref, o_ref):
    o_ref[...] = _dot(x_ref[...], w_ref[...]).astype(o_ref.dtype)


def _mm_gate_kernel(x_ref, w_ref, b_ref, o_ref):
    o_ref[...] = _sigmoid(_dot(x_ref[...], w_ref[...]) + b_ref[...]).astype(o_ref.dtype)


def _matmul(x, w, out_dtype, tm, tn, name, gate_bias=None):
    m, k = x.shape
    n = w.shape[1]
    in_specs = [pl.BlockSpec((tm, k), lambda i, j: (i, 0)),
                pl.BlockSpec((k, tn), lambda i, j: (0, j))]
    args = (x, w)
    body = _mm_kernel
    if gate_bias is not None:
        in_specs.append(pl.BlockSpec((1, tn), lambda i, j: (0, j)))
        args = (x, w, gate_bias)
        body = _mm_gate_kernel
    return pl.pallas_call(
        body,
        grid=(m // tm, n // tn),
        in_specs=in_specs,
        out_specs=pl.BlockSpec((tm, tn), lambda i, j: (i, j)),
        out_shape=jax.ShapeDtypeStruct((m, n), out_dtype),
        compiler_params=_params("parallel", "parallel"),
        name=name,
    )(*args)


def _gmlp_kernel(z_ref, g_ref, b_ref, ws_ref, bs_ref, o_ref, *, rows):
    z = z_ref[...]
    za = 0.5 * z * (1.0 + jnp.tanh(math.sqrt(2.0 / math.pi) * (z + 0.044715 * (z * z * z))))
    u = za[:, :GM_WIDTH]
    v = _layer_norm(za[:, GM_WIDTH:], g_ref[...], b_ref[...], LN_EPS).astype(BF16)
    t_idx = lax.broadcasted_iota(jnp.int32, (GM_CHUNK, GM_CHUNK), 0)
    s_idx = lax.broadcasted_iota(jnp.int32, (GM_CHUNK, GM_CHUNK), 1)
    causal = t_idx >= s_idx
    for g in range(GM_GROUPS):
        w = jnp.where(causal, ws_ref[g], 0.0).astype(BF16)
        cs = slice(g * GM_CHUNK, (g + 1) * GM_CHUNK)
        for c in range(rows // GM_CHUNK):
            rs = slice(c * GM_CHUNK, (c + 1) * GM_CHUNK)
            s = _dot(w, v[rs, cs]) + bs_ref[g]
            o_ref[rs, cs] = (u[rs, cs] * s).astype(o_ref.dtype)


def _gmlp(z, ln_g, ln_b, ws, bs, rows=512):
    m = z.shape[0]
    return pl.pallas_call(
        functools.partial(_gmlp_kernel, rows=rows),
        grid=(m // rows,),
        in_specs=[pl.BlockSpec((rows, 2 * GM_WIDTH), lambda i: (i, COL_A // (2 * GM_WIDTH))),
                  pl.BlockSpec((1, GM_WIDTH), lambda i: (0, 0)),
                  pl.BlockSpec((1, GM_WIDTH), lambda i: (0, 0)),
                  pl.BlockSpec((GM_GROUPS, GM_CHUNK, GM_CHUNK), lambda i: (0, 0, 0)),
                  pl.BlockSpec((GM_GROUPS, GM_CHUNK, 1), lambda i: (0, 0, 0))],
        out_specs=pl.BlockSpec((rows, GM_WIDTH), lambda i: (i, 0)),
        out_shape=jax.ShapeDtypeStruct((m, GM_WIDTH), BF16),
        compiler_params=_params("parallel"),
        name="gmlp",
    )(z, ln_g, ln_b, ws, bs)


def _token_shift(z_ref, p_ref, mu_ref, first):
    h = z_ref[0]
    prev_row = jnp.where(first, 0.0, p_ref[0, 7:8, :])
    hp = pltpu.roll(h, 1, axis=0)
    row = lax.broadcasted_iota(jnp.int32, h.shape, 0)
    hp = jnp.where(row == 0, prev_row, hp)
    return h + (hp - h) * mu_ref[...]


def _stage(dst_ref, val, n_chunks):
    for c in range(n_chunks):
        for h in range(RW_HEADS):
            dst_ref[c, h * RW_CHUNK:(h + 1) * RW_CHUNK, 0:RW_HEAD_DIM] = (
                val[c * RW_CHUNK:(c + 1) * RW_CHUNK, h * RW_HEAD_DIM:(h + 1) * RW_HEAD_DIM])


def _rwkv_kernel(zr_ref, zk_ref, zv_ref, zm_ref, pr_ref, pk_ref, pv_ref, pm_ref,
                 mur_ref, muk_ref, muv_ref, mum_ref, w0_ref, wb_ref, a0_ref, ab_ref, gb_ref,
                 kk_ref, ka_ref, rk_ref, lng_ref, lnb_ref, o_ref,
                 r_s, k_s, v_s, kk_s, a_s, lw_s, g_s, y_s, st_s, *, n_chunks):
    first = pl.program_id(1) == 0
    qrows = RW_QUAD * RW_CHUNK
    nd = RW_HEAD_DIM

    @pl.when(first)
    def _():
        st_s[...] = jnp.zeros_like(st_s)
        v_s[...] = jnp.zeros_like(v_s)

    m = _token_shift(zm_ref, pm_ref, mum_ref, first)
    m01 = m[:, 0:128]
    lw_lin = _dot(jnp.tanh(m01).astype(BF16), wb_ref[...]) + w0_ref[...]
    _stage(lw_s, -LOGW_SCALE * _sigmoid(lw_lin), n_chunks)
    a = _sigmoid(a0_ref[...] + _dot(m01.astype(BF16), ab_ref[...]))
    _stage(a_s, a, n_chunks)
    _stage(g_s, _dot(_sigmoid(m[:, 128:384]).astype(BF16), gb_ref[...]), n_chunks)
    k = _token_shift(zk_ref, pk_ref, muk_ref, first)
    _stage(kk_s, k * kk_ref[...], n_chunks)
    _stage(k_s, k * (1.0 + (a - 1.0) * ka_ref[...]), n_chunks)
    _stage(r_s, _token_shift(zr_ref, pr_ref, mur_ref, first), n_chunks)
    _stage(v_s, _token_shift(zv_ref, pv_ref, muv_ref, first), n_chunks)

    ri = lax.broadcasted_iota(jnp.int32, (qrows, qrows), 0)
    ci = lax.broadcasted_iota(jnp.int32, (qrows, qrows), 1)
    shift = RW_CHUNK.bit_length() - 1
    same = (ri >> shift) == (ci >> shift)
    strict = same & (ci < ri)
    incl = same & (ci <= ri)
    tri_bd = jnp.where(incl, 1.0, 0.0).astype(BF16)
    ones_bd = jnp.where(same, 1.0, 0.0).astype(BF16)
    ones_sel = jnp.where(
        (lax.broadcasted_iota(jnp.int32, (qrows, RW_QUAD * 128), 0) >> shift)
        == (lax.broadcasted_iota(jnp.int32, (qrows, RW_QUAD * 128), 1) >> 7), 1.0, 0.0).astype(BF16)
    e2 = jnp.where(lax.broadcasted_iota(jnp.int32, (nd, 2 * nd), 1)
                   == lax.broadcasted_iota(jnp.int32, (nd, 2 * nd), 0) + nd, 1.0, 0.0).astype(BF16)

    def chunk_body(c, carry):
        quads = range(RW_HEADS // RW_QUAD)
        rows = [slice(q * qrows, (q + 1) * qrows) for q in quads]
        r = [r_s[c, rows[q], :] for q in quads]
        kq = [k_s[c, rows[q], :] for q in quads]
        vpad = [v_s[c, rows[q], :] for q in quads]
        vpb = [vpad[q].astype(BF16) for q in quads]
        kkr = [kk_s[c, rows[q], :] for q in quads]
        lw = [lw_s[c, rows[q], :] for q in quads]
        hi = [lw[q].astype(BF16) for q in quads]
        lo = [(lw[q] - hi[q].astype(F32)).astype(BF16) for q in quads]
        cum = [_dot(tri_bd, hi[q]) + _dot(tri_bd, lo[q]) for q in quads]
        tot = [_dot(ones_bd, hi[q]) + _dot(ones_bd, lo[q]) for q in quads]
        pc_t = [jnp.exp(_dg(hi[q], ones_sel, TN_DIMS) + _dg(lo[q], ones_sel, TN_DIMS)) for q in quads]
        nrm = [jnp.sqrt(jnp.sum(kkr[q] * kkr[q], axis=-1, keepdims=True)) for q in quads]
        kk = [kkr[q] / jnp.maximum(nrm[q], 1e-12) for q in quads]
        bt0 = [kk[q] * a_s[c, rows[q], :] for q in quads]
        cum_ex = [cum[q] - (hi[q].astype(F32) + lo[q].astype(F32)) for q in quads]
        e_neg = [jnp.exp(-cum[q]) for q in quads]
        alb = [(kk[q] * jnp.exp(cum_ex[q])).astype(BF16) for q in quads]
        rbb = [(r[q] * jnp.exp(cum[q])).astype(BF16) for q in quads]
        kh = [(kq[q] * e_neg[q]).astype(BF16) for q in quads]
        bh = [(bt0[q] * e_neg[q]).astype(BF16) for q in quads]
        e_tail = [jnp.exp(tot[q] - cum[q]) for q in quads]
        kt = [(kq[q] * e_tail[q]).astype(BF16) for q in quads]
        btl = [(bt0[q] * e_tail[q]).astype(BF16) for q in quads]
        lmat = [jnp.where(strict, _dg(alb[q], bh[q], NT_DIMS), 0.0).astype(BF16) for q in quads]
        a_ak = [jnp.where(strict, _dg(alb[q], kh[q], NT_DIMS), 0.0).astype(BF16) for q in quads]
        a_rb = [jnp.where(incl, _dg(rbb[q], bh[q], NT_DIMS), 0.0).astype(BF16) for q in quads]
        a_rk = [jnp.where(incl, _dg(rbb[q], kh[q], NT_DIMS), 0.0).astype(BF16) for q in quads]
        x = [_dot(a_ak[q], vpb[q]) + _dot(alb[q], e2) for q in quads]
        x = [x[q] - _dot(lmat[q], x[q].astype(BF16)) for q in quads]
        pw = lmat
        step = 2
        while step < RW_CHUNK:
            pw = [_dot(pw[q], pw[q]).astype(BF16) for q in quads]
            x = [x[q] + _dot(pw[q], x[q].astype(BF16)) for q in quads]
            step *= 2
        xb = [x[q].astype(BF16) for q in quads]
        yr = [_dot(a_rk[q], vpb[q]) - _dot(a_rb[q], xb[q]) + _dot(rbb[q], e2) for q in quads]
        yrb = [yr[q].astype(BF16) for q in quads]

        heads = [(q, hh) for q in quads for hh in range(RW_QUAD)]
        hsl = [slice(hh * RW_CHUNK, (hh + 1) * RW_CHUNK) for hh in range(RW_QUAD)]
        h_aug = [st_s[q * RW_QUAD + hh] for q, hh in heads]
        hb = [h.astype(BF16) for h in h_aug]
        ys = [_dot(yrb[q][hsl[hh]], hb[n]) + yr[q][hsl[hh], 0:nd] for n, (q, hh) in enumerate(heads)]
        us = [_dot(xb[q][hsl[hh]], hb[n]) + x[q][hsl[hh], 0:nd] for n, (q, hh) in enumerate(heads)]
        kv = [_dg(kt[q][hsl[hh]], vpb[q][hsl[hh], 0:nd], TN_DIMS) for q, hh in heads]
        bu = [_dg(btl[q][hsl[hh]], us[n].astype(BF16), TN_DIMS) for n, (q, hh) in enumerate(heads)]
        for n, (q, hh) in enumerate(heads):
            st_s[n, nd:2 * nd, :] = pc_t[q][:, hh * 128:hh * 128 + nd] * h_aug[n][nd:2 * nd, :] + kv[n] - bu[n]
        for n, (q, hh) in enumerate(heads):
            y = ys[n]
            mu = jnp.mean(y, axis=-1, keepdims=True)
            d = y - mu
            var = jnp.mean(d * d, axis=-1, keepdims=True)
            yn = d * lax.rsqrt(var + RW_GN_EPS) * lng_ref[n] + lnb_ref[n]
            hs = hsl[hh]
            bonus = jnp.sum(r[q][hs] * kq[q][hs] * rk_ref[n], axis=-1, keepdims=True) * vpad[q][hs, 0:nd]
            orow = slice(n * RW_CHUNK, (n + 1) * RW_CHUNK)
            y_s[c, orow, :] = (yn + bonus) * g_s[c, orow, :]
        return carry

    lax.fori_loop(0, n_chunks, chunk_body, 0)

    for c in range(n_chunks):
        for h in range(RW_HEADS):
            o_ref[0, c * RW_CHUNK:(c + 1) * RW_CHUNK, h * RW_HEAD_DIM:(h + 1) * RW_HEAD_DIM] = (
                y_s[c, h * RW_CHUNK:(h + 1) * RW_CHUNK, :].astype(o_ref.dtype))


def _rwkv(z3, zm3, mu_r, mu_k, mu_v, mu_m, w0, wb, a0, ab, gb, kk_p, ka_p, rk_p, lng, lnb):
    bsz, seq, _ = z3.shape
    tb = RW_TB
    n_chunks = tb // RW_CHUNK
    cb = COL_R // RW_WIDTH

    def zspec(col_blk, width):
        return pl.BlockSpec((1, tb, width), lambda b, i: (b, i, col_blk))

    def pspec(col_blk, width):
        return pl.BlockSpec((1, 8, width), lambda b, i: (b, jnp.maximum(i * (tb // 8) - 1, 0), col_blk))

    def full(shape):
        return pl.BlockSpec(shape, lambda b, i: (0,) * len(shape))

    stage = pltpu.VMEM((n_chunks, RW_HEADS * RW_CHUNK, RW_HEAD_DIM), F32)
    stage_wide = pltpu.VMEM((n_chunks, RW_HEADS * RW_CHUNK, 2 * RW_HEAD_DIM), F32)
    return pl.pallas_call(
        functools.partial(_rwkv_kernel, n_chunks=n_chunks),
        grid=(bsz, seq // tb),
        in_specs=[zspec(cb, RW_WIDTH), zspec(cb + 1, RW_WIDTH), zspec(cb + 2, RW_WIDTH),
                  zspec(0, MISC_W),
                  pspec(cb, RW_WIDTH), pspec(cb + 1, RW_WIDTH), pspec(cb + 2, RW_WIDTH),
                  pspec(0, MISC_W),
                  full((1, RW_WIDTH)), full((1, RW_WIDTH)), full((1, RW_WIDTH)), full((1, MISC_W)),
                  full((1, RW_WIDTH)), full((128, RW_WIDTH)), full((1, RW_WIDTH)), full((128, RW_WIDTH)),
                  full((256, RW_WIDTH)),
                  full((1, RW_WIDTH)), full((1, RW_WIDTH)),
                  full((RW_HEADS, 1, RW_HEAD_DIM)), full((RW_HEADS, 1, RW_HEAD_DIM)),
                  full((RW_HEADS, 1, RW_HEAD_DIM))],
        out_specs=pl.BlockSpec((1, tb, RW_WIDTH), lambda b, i: (b, i, 0)),
        out_shape=jax.ShapeDtypeStruct((bsz, seq, RW_WIDTH), BF16),
        scratch_shapes=[stage, stage, stage_wide, stage, stage, stage, stage, stage,
                        pltpu.VMEM((RW_HEADS, 2 * RW_HEAD_DIM, RW_HEAD_DIM), F32)],
        compiler_params=_params("parallel", "arbitrary"),
        name="rwkv7",
    )(z3, z3, z3, zm3, z3, z3, z3, zm3, mu_r, mu_k, mu_v, mu_m, w0, wb, a0, ab, gb,
      kk_p, ka_p, rk_p, lng, lnb)


FX_CUM_BLOCK = 256
FX_HEAD_GROUP = 4


def _fox_cum_kernel(zf_ref, fb_ref, col_ref, row_ref, *, seq):
    n = FX_CUM_BLOCK
    ri = lax.broadcasted_iota(jnp.int32, (n, n), 0)
    ci = lax.broadcasted_iota(jnp.int32, (n, n), 1)
    tri = jnp.where(ci <= ri, 1.0, 0.0).astype(BF16)
    carry = jnp.zeros((1, 128), F32)
    for j in range(seq // n):
        rs = slice(j * n, (j + 1) * n)
        x = zf_ref[0, rs, :] + fb_ref[...]
        lf = jnp.minimum(x, 0.0) - jnp.log(1.0 + jnp.exp(-jnp.abs(x)))
        hi = lf.astype(BF16)
        r1 = lf - hi.astype(F32)
        mid = r1.astype(BF16)
        lo = (r1 - mid.astype(F32)).astype(BF16)
        c = _dot(tri, hi) + _dot(tri, mid) + _dot(tri, lo) + carry
        col_ref[0, rs, :] = c
        row_ref[0, :, rs] = c.T[0:FX_HEADS, :]
        carry = c[n - 1:n, :]


def _fox_cum(zm3, fb_pad):
    bsz, seq, _ = zm3.shape
    return pl.pallas_call(
        functools.partial(_fox_cum_kernel, seq=seq),
        grid=(bsz,),
        in_specs=[pl.BlockSpec((1, seq, 128), lambda b: (b, 0, MISC_F_LANE // 128)),
                  pl.BlockSpec((1, 128), lambda b: (0, 0))],
        out_specs=[pl.BlockSpec((1, seq, 128), lambda b: (b, 0, 0)),
                   pl.BlockSpec((1, FX_HEADS, seq), lambda b: (b, 0, 0))],
        out_shape=[jax.ShapeDtypeStruct((bsz, seq, 128), F32),
                   jax.ShapeDtypeStruct((bsz, FX_HEADS, seq), F32)],
        compiler_params=_params("parallel"),
        name="fox_cum",
    )(zm3, fb_pad)


def _fox_kernel(q_ref, k_ref, v_ref, cc_ref, cr_ref, o_ref, m_s, l_s, acc_s, *, tq, tk):
    i = pl.program_id(1)
    scale = FX_HEAD_DIM ** -0.5
    rows = i * tq + lax.broadcasted_iota(jnp.int32, (tq, tk), 0)
    col0 = lax.broadcasted_iota(jnp.int32, (tq, tk), 1)
    n_kv = (i * tq + tq + tk - 1) // tk
    m_s[...] = jnp.full(m_s.shape, -1e30, F32)
    l_s[...] = jnp.zeros(l_s.shape, F32)
    acc_s[...] = jnp.zeros(acc_s.shape, F32)
    heads = [slice(h * FX_HEAD_DIM, (h + 1) * FX_HEAD_DIM) for h in range(FX_HEADS)]
    q_all = q_ref[0] * scale
    qs = [q_all[:, ls].astype(BF16) for ls in heads]
    cqs = [cc_ref[0, :, h:h + 1] for h in range(FX_HEADS)]

    def block(j, masked):
        ks = pl.multiple_of(j * tk, tk)
        kblk = k_ref[0, pl.ds(ks, tk), :].astype(BF16)
        vblk = v_ref[0, pl.ds(ks, tk), :].astype(BF16)
        for g0 in range(0, FX_HEADS, FX_HEAD_GROUP):
            hs = range(g0, g0 + FX_HEAD_GROUP)
            ss = {h: _dg(qs[h], kblk[:, heads[h]], NT_DIMS) for h in hs}
            ss = {h: ss[h] + cqs[h] - cr_ref[0, h, pl.ds(j, 1), :] for h in hs}
            if masked:
                keep = rows >= ks + col0
                ss = {h: jnp.where(keep, ss[h], -1e30) for h in hs}
            m_prev = {h: m_s[h] for h in hs}
            m_new = {h: jnp.maximum(m_prev[h], jnp.max(ss[h], axis=-1, keepdims=True)) for h in hs}
            ps = {h: jnp.exp(ss[h] - m_new[h]) for h in hs}
            alphas = {h: jnp.exp(m_prev[h] - m_new[h]) for h in hs}
            pvs = {h: _dot(ps[h].astype(BF16), vblk[:, heads[h]]) for h in hs}
            for h in hs:
                l_s[h] = alphas[h] * l_s[h] + jnp.sum(ps[h], axis=-1, keepdims=True)
                acc_s[h] = alphas[h] * acc_s[h] + pvs[h]
                m_s[h] = m_new[h]

    def visible_body(j, carry):
        block(j, masked=False)
        return carry

    lax.fori_loop(0, n_kv - 1, visible_body, 0)
    block(n_kv - 1, masked=True)
    for h, ls in enumerate(heads):
        o_ref[0, :, ls] = (acc_s[h] / l_s[h]).astype(o_ref.dtype)


def _fox(zfx3, cum_col, cum_row, tq=256, tk=512):
    bsz, seq, _ = zfx3.shape
    cb = 0
    assert tk % tq == 0 and seq % tk == 0
    cum_row4 = cum_row.reshape(bsz, FX_HEADS, seq // tk, tk)
    return pl.pallas_call(
        functools.partial(_fox_kernel, tq=tq, tk=tk),
        grid=(bsz, seq // tq),
        in_specs=[pl.BlockSpec((1, tq, FX_WIDTH), lambda b, i: (b, i, cb)),
                  pl.BlockSpec((1, seq, FX_WIDTH), lambda b, i: (b, 0, cb + 1)),
                  pl.BlockSpec((1, seq, FX_WIDTH), lambda b, i: (b, 0, cb + 2)),
                  pl.BlockSpec((1, tq, 128), lambda b, i: (b, i, 0)),
                  pl.BlockSpec((1, FX_HEADS, seq // tk, tk), lambda b, i: (b, 0, 0, 0))],
        out_specs=pl.BlockSpec((1, tq, FX_WIDTH), lambda b, i: (b, i, 0)),
        out_shape=jax.ShapeDtypeStruct((bsz, seq, FX_WIDTH), BF16),
        scratch_shapes=[pltpu.VMEM((FX_HEADS, tq, 1), F32), pltpu.VMEM((FX_HEADS, tq, 1), F32),
                        pltpu.VMEM((FX_HEADS, tq, FX_HEAD_DIM), F32)],
        compiler_params=_params("parallel", "parallel"),
        name="fox_attn",
    )(zfx3, zfx3, zfx3, cum_col, cum_row4)


def _merge_kernel(ya_ref, yb_ref, yc_ref, pa_ref, pb_ref, pc_ref, g0_ref, g1_ref, g2_ref, o_ref):
    acc = g0_ref[...].astype(F32) * _dot(ya_ref[...], pa_ref[...])
    acc = acc + g1_ref[...].astype(F32) * _dot(yb_ref[...], pb_ref[...])
    acc = acc + g2_ref[...].astype(F32) * _dot(yc_ref[...], pc_ref[...])
    o_ref[...] = acc.astype(o_ref.dtype)


def _merge(ya, yb, yc, pa, pb, pc, gates, tm=1024, tn=512):
    m = ya.shape[0]
    gb0 = 0
    per = D_MODEL // tn

    def yspec(width):
        return pl.BlockSpec((tm, width), lambda i, j: (i, 0))

    def wspec(width):
        return pl.BlockSpec((width, tn), lambda i, j: (0, j))

    def gspec(k):
        return pl.BlockSpec((tm, tn), lambda i, j: (i, gb0 + k * per + j))

    return pl.pallas_call(
        _merge_kernel,
        grid=(m // tm, D_MODEL // tn),
        in_specs=[yspec(GM_WIDTH), yspec(RW_WIDTH), yspec(FX_WIDTH),
                  wspec(GM_WIDTH), wspec(RW_WIDTH), wspec(FX_WIDTH),
                  gspec(0), gspec(1), gspec(2)],
        out_specs=pl.BlockSpec((tm, tn), lambda i, j: (i, j)),
        out_shape=jax.ShapeDtypeStruct((m, D_MODEL), BF16),
        compiler_params=_params("parallel", "parallel"),
        name="merge",
    )(ya, yb, yc, pa, pb, pc, gates, gates, gates)


def _out_ln_kernel(m_ref, w_ref, x_ref, g_ref, b_ref, o32_ref, o16_ref):
    h = DEEPNORM_ALPHA * x_ref[...] + _dot(m_ref[...], w_ref[...])
    y = _layer_norm(h, g_ref[...], b_ref[...], LN_EPS)
    o32_ref[...] = y
    o16_ref[...] = y.astype(BF16)


def _out_ln(merged, w_out, x, g, b, tm=512):
    m = x.shape[0]
    row = pl.BlockSpec((tm, D_MODEL), lambda i: (i, 0))
    vec = pl.BlockSpec((1, D_MODEL), lambda i: (0, 0))
    return pl.pallas_call(
        _out_ln_kernel,
        grid=(m // tm,),
        in_specs=[row, pl.BlockSpec((D_MODEL, D_MODEL), lambda i: (0, 0)), row, vec, vec],
        out_specs=[row, row],
        out_shape=[jax.ShapeDtypeStruct((m, D_MODEL), F32), jax.ShapeDtypeStruct((m, D_MODEL), BF16)],
        compiler_params=_params("parallel"),
        name="out_ln",
    )(merged, w_out, x, g, b)


def _router_kernel(x_ref, w_ref, b_ref, o_ref):
    logits = jnp.dot(x_ref[...], w_ref[...], preferred_element_type=F32,
                     precision=lax.Precision.HIGHEST) + b_ref[...]
    lane = lax.broadcasted_iota(jnp.int32, logits.shape, 1)
    logits = jnp.where(lane < MOE_EXPERTS, logits, -1e30)
    m1 = jnp.max(logits, axis=-1, keepdims=True)
    i1 = jnp.min(jnp.where(logits == m1, lane, 128), axis=-1, keepdims=True)
    rest = jnp.where(lane == i1, -1e30, logits)
    m2 = jnp.max(rest, axis=-1, keepdims=True)
    i2 = jnp.min(jnp.where(rest == m2, lane, 128), axis=-1, keepdims=True)
    e2 = jnp.exp(m2 - m1)
    w1 = 1.0 / (1.0 + e2)
    o_ref[...] = jnp.where(lane == i1, w1, 0.0) + jnp.where(lane == i2, e2 * w1, 0.0)


def _router(x1, w_pad, b_pad, tm=1024):
    m = x1.shape[0]
    return pl.pallas_call(
        _router_kernel,
        grid=(m // tm,),
        in_specs=[pl.BlockSpec((tm, D_MODEL), lambda i: (i, 0)),
                  pl.BlockSpec((D_MODEL, 128), lambda i: (0, 0)),
                  pl.BlockSpec((1, 128), lambda i: (0, 0))],
        out_specs=pl.BlockSpec((tm, 128), lambda i: (i, 0)),
        out_shape=jax.ShapeDtypeStruct((m, 128), F32),
        compiler_params=_params("parallel"),
        name="router",
    )(x1, w_pad, b_pad)


def _ffn_kernel(x_ref, c_ref, w1_ref, w3_ref, w2_ref, o_ref):
    @pl.when((pl.program_id(1) == 0) & (pl.program_id(2) == 0))
    def _():
        o_ref[...] = jnp.zeros_like(o_ref)

    x = x_ref[...]
    g = _dot(x, w1_ref[0])
    u = _dot(x, w3_ref[0])
    h = (g * _sigmoid(g)) * u * c_ref[0]
    o_ref[...] += _dot(h.astype(BF16), w2_ref[0])


def _ffn(xb, comb, w1, w3, w2, tm=1024, tf=512):
    m = xb.shape[0]
    n_e, _, f = w1.shape
    return pl.pallas_call(
        _ffn_kernel,
        grid=(m // tm, n_e, f // tf),
        in_specs=[pl.BlockSpec((tm, D_MODEL), lambda i, e, j: (i, 0)),
                  pl.BlockSpec((1, tm, 1), lambda i, e, j: (e, i, 0)),
                  pl.BlockSpec((1, D_MODEL, tf), lambda i, e, j: (e, 0, j)),
                  pl.BlockSpec((1, D_MODEL, tf), lambda i, e, j: (e, 0, j)),
                  pl.BlockSpec((1, tf, D_MODEL), lambda i, e, j: (e, j, 0))],
        out_specs=pl.BlockSpec((tm, D_MODEL), lambda i, e, j: (i, 0)),
        out_shape=jax.ShapeDtypeStruct((m, D_MODEL), F32),
        compiler_params=_params("parallel", "arbitrary", "arbitrary"),
        name="swiglu",
    )(xb, comb, w1, w3, w2)


MOE_BLK = 288
MOE_SUB = 1024
MOE_NSUB = 2
MOE_PASS = 4


def _moe_kernel(ntot_ref, bsub_ref, bloc_ref, x_ref, comb_ref, rank_ref, rankt_ref, w1_ref, w3_ref, w2_ref,
                o_ref, xs_s, acc_s, *, n_f, col_split):
    i, e, g, f = pl.program_id(0), pl.program_id(1), pl.program_id(2), pl.program_id(3)
    row = i * MOE_EXPERTS + e
    nb = jnp.clip(ntot_ref[row] - g * MOE_PASS, 0, MOE_PASS)

    @pl.when((e == 0) & (g == 0) & (f == 0))
    def _():
        o_ref[...] = jnp.zeros_like(o_ref)

    def sub_rows(b):
        sub = bsub_ref[row, g * MOE_PASS + b]
        return sub, pl.ds(pl.multiple_of(sub * MOE_SUB, MOE_SUB), MOE_SUB)

    @pl.when(f == 0)
    def _():
        slot = lax.broadcasted_iota(jnp.int32, (MOE_BLK, MOE_SUB), 0)

        def gather(b, carry):
            sub, rows = sub_rows(b)
            first = bloc_ref[row, g * MOE_PASS + b] * MOE_BLK
            rank_row = rankt_ref[sub, pl.ds(e, 1), :]
            onehot = jnp.where(rank_row == slot + first, 1.0, 0.0).astype(BF16)
            xs_s[b] = _dot(onehot, x_ref[rows, :]).astype(BF16)
            acc_s[b] = jnp.zeros(acc_s.shape[1:], F32)
            return carry

        lax.fori_loop(0, nb, gather, 0)

    def compute(b, carry):
        xb = xs_s[b]
        gate = _dot(xb, w1_ref[0])
        up = _dot(xb, w3_ref[0])
        h = (gate * _sigmoid(gate)) * up
        acc_s[b] += _dot(h.astype(BF16), w2_ref[0])
        return carry

    lax.fori_loop(0, nb, compute, 0)

    @pl.when(f == n_f - 1)
    def _():
        lane = lax.broadcasted_iota(jnp.int32, (MOE_SUB, 128), 1)
        sel = lane == e
        slot = lax.broadcasted_iota(jnp.int32, (MOE_SUB, MOE_BLK), 1)
        cw = D_MODEL // col_split

        def scatter(b, carry):
            _, rows = sub_rows(b)
            first = bloc_ref[row, g * MOE_PASS + b] * MOE_BLK
            c_col = jnp.sum(jnp.where(sel, comb_ref[rows, :], 0.0), axis=-1, keepdims=True)
            r_col = jnp.sum(jnp.where(sel, rank_ref[rows, :], 0), axis=-1, keepdims=True)
            onehot_t = jnp.where(r_col == slot + first, 1.0, 0.0).astype(BF16)
            for s in range(col_split):
                cs = slice(s * cw, (s + 1) * cw)
                o_ref[rows, cs] += _dot(onehot_t, acc_s[b, :, cs].astype(BF16)) * c_col
            return carry

        lax.fori_loop(0, nb, scatter, 0)


def _moe(xb, comb, plan, w1, w3, w2, tf=512):
    rank, rank_t, ntot, bsub, bloc = plan
    m = xb.shape[0]
    tm = MOE_SUB * MOE_NSUB
    n_e, _, f = w1.shape
    n_f = f // tf
    blocks_max = MOE_NSUB * pl.cdiv(MOE_SUB, MOE_BLK)
    n_pass = pl.cdiv(blocks_max, MOE_PASS)
    once = pl.Buffered(1)

    def wmap(sel):
        def index_map(i, e, g, j, ntot_ref, bsub_ref, bloc_ref):
            live = (g == 0) | (ntot_ref[i * MOE_EXPERTS + e] > g * MOE_PASS)
            jj = jnp.where(live, j, n_f - 1)
            return (e, 0, jj) if sel == 0 else (e, jj, 0)
        return index_map

    def tile(i, e, g, j, *_):
        return (i, 0)

    grid_spec = pltpu.PrefetchScalarGridSpec(
        num_scalar_prefetch=3,
        grid=(m // tm, n_e, n_pass, n_f),
        in_specs=[pl.BlockSpec((tm, D_MODEL), tile, pipeline_mode=once),
                  pl.BlockSpec((tm, 128), tile, pipeline_mode=once),
                  pl.BlockSpec((tm, 128), tile, pipeline_mode=once),
                  pl.BlockSpec((MOE_NSUB, 128, MOE_SUB), lambda i, e, g, j, *_: (i, 0, 0), pipeline_mode=once),
                  pl.BlockSpec((1, D_MODEL, tf), wmap(0)),
                  pl.BlockSpec((1, D_MODEL, tf), wmap(0)),
                  pl.BlockSpec((1, tf, D_MODEL), wmap(1))],
        out_specs=pl.BlockSpec((tm, D_MODEL), tile, pipeline_mode=once),
        scratch_shapes=[pltpu.VMEM((MOE_PASS, MOE_BLK, D_MODEL), BF16),
                        pltpu.VMEM((MOE_PASS, MOE_BLK, D_MODEL), F32)],
    )
    return pl.pallas_call(
        functools.partial(_moe_kernel, n_f=n_f, col_split=2),
        grid_spec=grid_spec,
        out_shape=jax.ShapeDtypeStruct((m, D_MODEL), F32),
        compiler_params=_params("parallel", "arbitrary", "arbitrary", "arbitrary"),
        name="moe",
    )(ntot, bsub, bloc, xb, comb, rank, rank_t, w1, w3, w2)


def _moe_plan(comb):
    m = comb.shape[0]
    n_sub = m // MOE_SUB
    routed = (comb > 0.0).reshape(n_sub, MOE_SUB, 128).astype(jnp.int32)
    rank = jnp.cumsum(routed, axis=1) - routed
    rank = jnp.where(routed > 0, rank, -1)
    count = jnp.sum(routed, axis=1)[:, :MOE_EXPERTS]
    nblk = ((count + MOE_BLK - 1) // MOE_BLK).reshape(n_sub // MOE_NSUB, MOE_NSUB, MOE_EXPERTS)
    ends = jnp.swapaxes(jnp.cumsum(nblk, axis=1), 1, 2)
    starts = ends - jnp.swapaxes(nblk, 1, 2)
    ntot = ends[:, :, -1].reshape(-1)
    blocks_max = MOE_NSUB * pl.cdiv(MOE_SUB, MOE_BLK)
    bidx = jnp.arange(blocks_max)[None, None, :, None]
    bsub = jnp.sum((bidx >= ends[:, :, None, :]).astype(jnp.int32), axis=-1)
    bsub = jnp.minimum(bsub, MOE_NSUB - 1)
    sub_onehot = (bsub[..., None] == jnp.arange(MOE_NSUB)).astype(jnp.int32)
    bloc = bidx[..., 0] - jnp.sum(sub_onehot * starts[:, :, None, :], axis=-1)
    shape2 = (-1, blocks_max)
    return (rank.reshape(m, 128), jnp.swapaxes(rank, 1, 2), ntot.astype(jnp.int32),
            bsub.reshape(shape2).astype(jnp.int32), bloc.reshape(shape2).astype(jnp.int32))


def _ple_kernel(x_ref, wg_ref, bg_ref, p_ref, wp_ref, o_ref):
    gate = _sigmoid(_dot(x_ref[...], wg_ref[...]) + bg_ref[...])
    o_ref[...] = gate * _dot(p_ref[...].astype(BF16), wp_ref[...])


def _ple(x1b, wg, bg, p, wp, tm=1024, tn=1024):
    m = x1b.shape[0]
    return pl.pallas_call(
        _ple_kernel,
        grid=(m // tm, D_MODEL // tn),
        in_specs=[pl.BlockSpec((tm, D_MODEL), lambda i, j: (i, 0)),
                  pl.BlockSpec((D_MODEL, tn), lambda i, j: (0, j)),
                  pl.BlockSpec((1, tn), lambda i, j: (0, j)),
                  pl.BlockSpec((tm, PLE_DIM), lambda i, j: (i, 0)),
                  pl.BlockSpec((PLE_DIM, tn), lambda i, j: (0, j))],
        out_specs=pl.BlockSpec((tm, tn), lambda i, j: (i, j)),
        out_shape=jax.ShapeDtypeStruct((m, D_MODEL), F32),
        compiler_params=_params("parallel", "parallel"),
        name="ple",
    )(x1b, wg, bg, p, wp)


def _final_ln_kernel(x1_ref, ff_ref, ple_ref, g_ref, b_ref, o32_ref, o16_ref):
    h = DEEPNORM_ALPHA * x1_ref[...] + ff_ref[...] + ple_ref[...]
    y = _layer_norm(h, g_ref[...], b_ref[...], LN_EPS)
    o32_ref[...] = y
    o16_ref[...] = y.astype(BF16)


def _final_ln(x1, ff, ple, g, b, tm=512):
    m = x1.shape[0]
    row = pl.BlockSpec((tm, D_MODEL), lambda i: (i, 0))
    vec = pl.BlockSpec((1, D_MODEL), lambda i: (0, 0))
    return pl.pallas_call(
        _final_ln_kernel,
        grid=(m // tm,),
        in_specs=[row, row, row, vec, vec],
        out_specs=[row, row],
        out_shape=[jax.ShapeDtypeStruct((m, D_MODEL), F32), jax.ShapeDtypeStruct((m, D_MODEL), BF16)],
        compiler_params=_params("parallel"),
        name="final_ln",
    )(x1, ff, ple, g, b)


def _shift_cast_kernel(a_ref, b_ref, o_ref, *, shift):
    w = a_ref.shape[-1]
    both = jnp.concatenate([a_ref[0], b_ref[0]], axis=1)
    o_ref[0] = pltpu.roll(both, 2 * w - shift, axis=1)[:, :w].astype(o_ref.dtype)


def _shift_cast(w_in, first_col, n_cols, blk):
    n_l, k, n = w_in.shape
    base, shift = divmod(first_col, blk)
    assert n_cols % blk == 0 and 0 < shift and first_col + n_cols <= n
    return pl.pallas_call(
        functools.partial(_shift_cast_kernel, shift=shift),
        grid=(n_l, n_cols // blk),
        in_specs=[pl.BlockSpec((1, k, blk), lambda l, j: (l, 0, base + j)),
                  pl.BlockSpec((1, k, blk), lambda l, j: (l, 0, base + j + 1))],
        out_specs=pl.BlockSpec((1, k, blk), lambda l, j: (l, 0, j)),
        out_shape=jax.ShapeDtypeStruct((n_l, k, n_cols), BF16),
        compiler_params=_params("parallel", "parallel"),
        name="shift_cast",
    )(w_in, w_in)


def _split_w_in(w_in):
    w_ar = w_in[..., :COL_LORA].astype(BF16)
    w_lora = w_in[..., COL_LORA:COL_LORA + LORA_W].astype(BF16)
    w_f = _shift_cast(w_in, COL_F, 128, 128)
    w_fx = _shift_cast(w_in, COL_FX, 3 * FX_WIDTH, 256)
    w_gate = _shift_cast(w_in, COL_GATE, 3 * D_MODEL, 256)
    return w_ar, jnp.concatenate([w_lora, w_f], axis=-1), w_fx, w_gate


def _pad_rows(w, lo, total):
    n_l, r, n = w.shape
    return jnp.concatenate([jnp.zeros((n_l, lo, n), w.dtype), w,
                            jnp.zeros((n_l, total - lo - r, n), w.dtype)], axis=1)


def kernel(x, p, w_in, gate_b, gm_ln_g, gm_ln_b, gm_ws, gm_bs, rw_mu, rw_w0, rw_wb, rw_a0, rw_ab, rw_gb, rw_kk, rw_ka, rw_rk, rw_lnx_g, rw_lnx_b, fx_fb, proj_a, proj_b, proj_c, w_out, ln1_g, ln1_b, ffn_w1, ffn_w3, ffn_w2, moe_router, moe_router_b, moe_w1, moe_w3, moe_w2, ple_gate_w, ple_gate_b, ple_proj, ln2_g, ln2_b):
    bsz, seq, d = x.shape
    m = bsz * seq
    n_l = w_in.shape[0]

    w_ar, w_misc, w_fx, w_gate = _split_w_in(w_in)
    mu_r = rw_mu[:, None, 0:RW_WIDTH]
    mu_k = rw_mu[:, None, RW_WIDTH:2 * RW_WIDTH]
    mu_v = rw_mu[:, None, 2 * RW_WIDTH:3 * RW_WIDTH]
    o = 3 * RW_WIDTH
    zl = lambda n: jnp.zeros((n_l, n), F32)
    n_lora = RW_LORA_W + RW_LORA_A + RW_LORA_G
    mu_m = jnp.concatenate([rw_mu[:, o:o + n_lora], zl(MISC_W - n_lora)], axis=-1)[:, None, :]
    wb_pad = _pad_rows(rw_wb, 0, 128).astype(BF16)
    ab_pad = _pad_rows(rw_ab, RW_LORA_W, 128).astype(BF16)
    gb_pad = _pad_rows(rw_gb, 0, 256).astype(BF16)
    fb_pad = jnp.concatenate([fx_fb, zl(128 - FX_HEADS)], axis=-1)[:, None, :]
    head3 = lambda t: t.reshape(n_l, RW_HEADS, 1, RW_HEAD_DIM)
    gm_bs3 = gm_bs[..., None]
    router_w = jnp.concatenate([moe_router, jnp.zeros(moe_router.shape[:2] + (128 - MOE_EXPERTS,), F32)], axis=-1)
    router_b = jnp.concatenate([moe_router_b, jnp.zeros((moe_router_b.shape[0], 128 - MOE_EXPERTS), F32)],
                               axis=-1)[:, None, :]
    bf = lambda t: t.astype(BF16)
    proj_a16, proj_b16, proj_c16, w_out16 = bf(proj_a), bf(proj_b), bf(proj_c), bf(w_out)
    ffn_w1_16, ffn_w3_16, ffn_w2_16 = bf(ffn_w1), bf(ffn_w3), bf(ffn_w2)
    moe_w1_16, moe_w3_16, moe_w2_16 = bf(moe_w1), bf(moe_w3), bf(moe_w2)
    ple_gw16, ple_pw16 = bf(ple_gate_w), bf(ple_proj)
    ones_comb = jnp.ones((1, m, 1), F32)

    x32 = x.reshape(m, d)
    x16 = x32.astype(BF16)
    for i in range(n_l):
        z = _matmul(x16, w_ar[i], F32, 1024, 1024, "in_proj_ar")
        zm = _matmul(x16, w_misc[i], F32, 1024, MISC_W, "in_proj_misc")
        zfx = _matmul(x16, w_fx[i], BF16, 1024, 3 * FX_WIDTH, "in_proj_fx")
        gates = _matmul(x16, w_gate[i], BF16, 1024, 1024, "in_proj_gate", gate_bias=gate_b[i][None])
        z3 = z.reshape(bsz, seq, z.shape[-1])
        zm3 = zm.reshape(bsz, seq, MISC_W)
        y_a = _gmlp(z, gm_ln_g[i][None], gm_ln_b[i][None], gm_ws[i], gm_bs3[i])
        y_b = _rwkv(z3, zm3, mu_r[i], mu_k[i], mu_v[i], mu_m[i], rw_w0[i][None], wb_pad[i], rw_a0[i][None],
                    ab_pad[i], gb_pad[i], rw_kk[i][None], rw_ka[i][None], head3(rw_rk)[i],
                    head3(rw_lnx_g)[i], head3(rw_lnx_b)[i]).reshape(m, RW_WIDTH)
        cum_col, cum_row = _fox_cum(zm3, fb_pad[i])
        y_c = _fox(zfx.reshape(bsz, seq, 3 * FX_WIDTH), cum_col, cum_row).reshape(m, FX_WIDTH)
        merged = _merge(y_a, y_b, y_c, proj_a16[i], proj_b16[i], proj_c16[i], gates)
        x1, x1b = _out_ln(merged, w_out16[i], x32, ln1_g[i][None], ln1_b[i][None])
        j = i // 2
        if i % 2 == 0:
            ff = _ffn(x1b, ones_comb, ffn_w1_16[j:j + 1], ffn_w3_16[j:j + 1], ffn_w2_16[j:j + 1])
        else:
            comb = _router(x1, router_w[j], router_b[j])
            ff = _moe(x1b, comb, _moe_plan(comb), moe_w1_16[j], moe_w3_16[j], moe_w2_16[j])
        ple = _ple(x1b, ple_gw16[i], ple_gate_b[i][None], p[i].reshape(m, PLE_DIM), ple_pw16[i])
        x32, x16 = _final_ln(x1, ff, ple, ln2_g[i][None], ln2_b[i][None])
    return x32.reshape(bsz, seq, d)
```

```python
import functools
import math

import jax
import jax.numpy as jnp
from jax import lax
from jax.experimental import pallas as pl
from jax.experimental.pallas import tpu as pltpu

F32 = jnp.float32
BF16 = jnp.bfloat16

D_MODEL = 2048
DEPTH = 4
PLE_DIM = 256
GM_CHUNK = 128
GM_GROUPS = 4
GM_WIDTH = 512
RW_HEADS = 16
RW_HEAD_DIM = 64
RW_WIDTH = 1024
RW_LORA_W = 64
RW_LORA_A = 64
RW_LORA_G = 160
RW_GN_EPS = 64e-5
FX_HEADS = 8
FX_HEAD_DIM = 64
FX_WIDTH = 512
MOE_EXPERTS = 8
DEEPNORM_ALPHA = (2.0 * DEPTH) ** 0.25
LN_EPS = 1e-5

COL_A = 0
COL_R = 1024
COL_LORA = 4096
COL_FX = 4384
COL_F = 5920
COL_GATE = 5928
MISC_W = 384
MISC_F_LANE = COL_FX - COL_LORA
FX_F_LOCAL = MISC_F_LANE % 128

RW_CHUNK = 64
RW_QUAD = 4
RW_TB = 256
LOGW_SCALE = math.exp(-0.5)

VMEM_LIMIT = 56 * 1024 * 1024

NT_DIMS = (((1,), (1,)), ((), ()))
TN_DIMS = (((0,), (0,)), ((), ()))


def _params(*sem):
    return pltpu.CompilerParams(dimension_semantics=sem, vmem_limit_bytes=VMEM_LIMIT)


def _dot(a, b):
    return jnp.dot(a, b, preferred_element_type=F32)


def _dg(a, b, dims):
    return lax.dot_general(a, b, dims, preferred_element_type=F32)


def _sigmoid(x):
    return 1.0 / (1.0 + jnp.exp(-x))


def _layer_norm(h, g, b, eps):
    mu = jnp.mean(h, axis=-1, keepdims=True)
    d = h - mu
    var = jnp.mean(d * d, axis=-1, keepdims=True)
    return d * lax.rsqrt(var + eps) * g + b


def _mm_kernel(x_ref, w_ref, o_ref):
    o_ref[...] = _dg(x_ref[...], w_ref[...], NT_DIMS).astype(o_ref.dtype)


def _mm_gate_kernel(x_ref, w_ref, b_ref, o_ref):
    o_ref[...] = _sigmoid(_dg(x_ref[...], w_ref[...], NT_DIMS) + b_ref[...]).astype(o_ref.dtype)


def _matmul(x, w_t, layer, out_dtype, tm, tn, name, gate_bias=None):
    m, k = x.shape
    n = w_t.shape[1]
    in_specs = [pl.BlockSpec((tm, k), lambda i, j: (i, 0)),
                pl.BlockSpec((None, tn, k), lambda i, j: (layer, j, 0))]
    args = (x, w_t)
    body = _mm_kernel
    if gate_bias is not None:
        in_specs.append(pl.BlockSpec((1, tn), lambda i, j: (0, j)))
        args = (x, w_t, gate_bias)
        body = _mm_gate_kernel
    return pl.pallas_call(
        body,
        grid=(m // tm, n // tn),
        in_specs=in_specs,
        out_specs=pl.BlockSpec((tm, tn), lambda i, j: (i, j)),
        out_shape=jax.ShapeDtypeStruct((m, n), out_dtype),
        compiler_params=_params("parallel", "parallel"),
        name=name,
    )(*args)


def _gmlp_kernel(z_ref, g_ref, b_ref, ws_ref, bs_ref, o_ref, *, rows):
    z = z_ref[...]
    za = 0.5 * z * (1.0 + jnp.tanh(math.sqrt(2.0 / math.pi) * (z + 0.044715 * (z * z * z))))
    u = za[:, :GM_WIDTH]
    v = _layer_norm(za[:, GM_WIDTH:], g_ref[...], b_ref[...], LN_EPS).astype(BF16)
    t_idx = lax.broadcasted_iota(jnp.int32, (GM_CHUNK, GM_CHUNK), 0)
    s_idx = lax.broadcasted_iota(jnp.int32, (GM_CHUNK, GM_CHUNK), 1)
    causal = t_idx >= s_idx
    for g in range(GM_GROUPS):
        w = jnp.where(causal, ws_ref[g], 0.0).astype(BF16)
        cs = slice(g * GM_CHUNK, (g + 1) * GM_CHUNK)
        for c in range(rows // GM_CHUNK):
            rs = slice(c * GM_CHUNK, (c + 1) * GM_CHUNK)
            s = _dot(w, v[rs, cs]) + bs_ref[g]
            o_ref[rs, cs] = (u[rs, cs] * s).astype(o_ref.dtype)


def _gmlp(z, ln_g, ln_b, ws, bs, rows=512):
    m = z.shape[0]
    return pl.pallas_call(
        functools.partial(_gmlp_kernel, rows=rows),
        grid=(m // rows,),
        in_specs=[pl.BlockSpec((rows, 2 * GM_WIDTH), lambda i: (i, COL_A // (2 * GM_WIDTH))),
                  pl.BlockSpec((1, GM_WIDTH), lambda i: (0, 0)),
                  pl.BlockSpec((1, GM_WIDTH), lambda i: (0, 0)),
                  pl.BlockSpec((GM_GROUPS, GM_CHUNK, GM_CHUNK), lambda i: (0, 0, 0)),
                  pl.BlockSpec((GM_GROUPS, GM_CHUNK, 1), lambda i: (0, 0, 0))],
        out_specs=pl.BlockSpec((rows, GM_WIDTH), lambda i: (i, 0)),
        out_shape=jax.ShapeDtypeStruct((m, GM_WIDTH), BF16),
        compiler_params=_params("parallel"),
        name="gmlp",
    )(z, ln_g, ln_b, ws, bs)


def _token_shift(z_ref, p_ref, mu_ref, first):
    h = z_ref[0]
    prev_row = jnp.where(first, 0.0, p_ref[0, 7:8, :])
    hp = pltpu.roll(h, 1, axis=0)
    row = lax.broadcasted_iota(jnp.int32, h.shape, 0)
    hp = jnp.where(row == 0, prev_row, hp)
    return h + (hp - h) * mu_ref[...]


def _stage(dst_ref, val, n_chunks):
    for c in range(n_chunks):
        for h in range(RW_HEADS):
            dst_ref[c, h * RW_CHUNK:(h + 1) * RW_CHUNK, 0:RW_HEAD_DIM] = (
                val[c * RW_CHUNK:(c + 1) * RW_CHUNK, h * RW_HEAD_DIM:(h + 1) * RW_HEAD_DIM])


def _rwkv_kernel(zr_ref, zk_ref, zv_ref, zm_ref, pr_ref, pk_ref, pv_ref, pm_ref,
                 mur_ref, muk_ref, muv_ref, mum_ref, w0_ref, wb_ref, a0_ref, ab_ref, gb_ref,
                 kk_ref, ka_ref, rk_ref, lng_ref, lnb_ref, o_ref,
                 r_s, k_s, v_s, kk_s, a_s, lw_s, g_s, y_s, st_s, *, n_chunks):
    first = pl.program_id(1) == 0
    qrows = RW_QUAD * RW_CHUNK
    nd = RW_HEAD_DIM

    @pl.when(first)
    def _():
        st_s[...] = jnp.zeros_like(st_s)
        v_s[...] = jnp.zeros_like(v_s)

    m = _token_shift(zm_ref, pm_ref, mum_ref, first)
    m01 = m[:, 0:128]
    lw_lin = _dot(jnp.tanh(m01).astype(BF16), wb_ref[...]) + w0_ref[...]
    _stage(lw_s, -LOGW_SCALE * _sigmoid(lw_lin), n_chunks)
    a = _sigmoid(a0_ref[...] + _dot(m01.astype(BF16), ab_ref[...]))
    _stage(a_s, a, n_chunks)
    _stage(g_s, _dot(_sigmoid(m[:, 128:384]).astype(BF16), gb_ref[...]), n_chunks)
    k = _token_shift(zk_ref, pk_ref, muk_ref, first)
    _stage(kk_s, k * kk_ref[...], n_chunks)
    _stage(k_s, k * (1.0 + (a - 1.0) * ka_ref[...]), n_chunks)
    _stage(r_s, _token_shift(zr_ref, pr_ref, mur_ref, first), n_chunks)
    _stage(v_s, _token_shift(zv_ref, pv_ref, muv_ref, first), n_chunks)

    ri = lax.broadcasted_iota(jnp.int32, (qrows, qrows), 0)
    ci = lax.broadcasted_iota(jnp.int32, (qrows, qrows), 1)
    shift = RW_CHUNK.bit_length() - 1
    same = (ri >> shift) == (ci >> shift)
    strict = same & (ci < ri)
    incl = same & (ci <= ri)
    tri_bd = jnp.where(incl, 1.0, 0.0).astype(BF16)
    ones_bd = jnp.where(same, 1.0, 0.0).astype(BF16)
    ones_sel = jnp.where(
        (lax.broadcasted_iota(jnp.int32, (qrows, RW_QUAD * 128), 0) >> shift)
        == (lax.broadcasted_iota(jnp.int32, (qrows, RW_QUAD * 128), 1) >> 7), 1.0, 0.0).astype(BF16)
    e2 = jnp.where(lax.broadcasted_iota(jnp.int32, (nd, 2 * nd), 1)
                   == lax.broadcasted_iota(jnp.int32, (nd, 2 * nd), 0) + nd, 1.0, 0.0).astype(BF16)

    def chunk_body(c, carry):
        quads = range(RW_HEADS // RW_QUAD)
        rows = [slice(q * qrows, (q + 1) * qrows) for q in quads]
        r = [r_s[c, rows[q], :] for q in quads]
        kq = [k_s[c, rows[q], :] for q in quads]
        vpad = [v_s[c, rows[q], :] for q in quads]
        vpb = [vpad[q].astype(BF16) for q in quads]
        kkr = [kk_s[c, rows[q], :] for q in quads]
        lw = [lw_s[c, rows[q], :] for q in quads]
        hi = [lw[q].astype(BF16) for q in quads]
        lo = [(lw[q] - hi[q].astype(F32)).astype(BF16) for q in quads]
        cum = [_dot(tri_bd, hi[q]) + _dot(tri_bd, lo[q]) for q in quads]
        tot = [_dot(ones_bd, hi[q]) + _dot(ones_bd, lo[q]) for q in quads]
        pc_t = [jnp.exp(_dg(hi[q], ones_sel, TN_DIMS) + _dg(lo[q], ones_sel, TN_DIMS)) for q in quads]
        nrm = [jnp.sqrt(jnp.sum(kkr[q] * kkr[q], axis=-1, keepdims=True)) for q in quads]
        kk = [kkr[q] / jnp.maximum(nrm[q], 1e-12) for q in quads]
        bt0 = [kk[q] * a_s[c, rows[q], :] for q in quads]
        cum_ex = [cum[q] - (hi[q].astype(F32) + lo[q].astype(F32)) for q in quads]
        e_neg = [jnp.exp(-cum[q]) for q in quads]
        alb = [(kk[q] * jnp.exp(cum_ex[q])).astype(BF16) for q in quads]
        rbb = [(r[q] * jnp.exp(cum[q])).astype(BF16) for q in quads]
        kh = [(kq[q] * e_neg[q]).astype(BF16) for q in quads]
        bh = [(bt0[q] * e_neg[q]).astype(BF16) for q in quads]
        e_tail = [jnp.exp(tot[q] - cum[q]) for q in quads]
        kt = [(kq[q] * e_tail[q]).astype(BF16) for q in quads]
        btl = [(bt0[q] * e_tail[q]).astype(BF16) for q in quads]
        lmat = [jnp.where(strict, _dg(alb[q], bh[q], NT_DIMS), 0.0).astype(BF16) for q in quads]
        a_ak = [jnp.where(strict, _dg(alb[q], kh[q], NT_DIMS), 0.0).astype(BF16) for q in quads]
        a_rb = [jnp.where(incl, _dg(rbb[q], bh[q], NT_DIMS), 0.0).astype(BF16) for q in quads]
        a_rk = [jnp.where(incl, _dg(rbb[q], kh[q], NT_DIMS), 0.0).astype(BF16) for q in quads]
        x = [_dot(a_ak[q], vpb[q]) + _dot(alb[q], e2) for q in quads]
        x = [x[q] - _dot(lmat[q], x[q].astype(BF16)) for q in quads]
        pw = lmat
        step = 2
        while step < RW_CHUNK:
            pw = [_dot(pw[q], pw[q]).astype(BF16) for q in quads]
            x = [x[q] + _dot(pw[q], x[q].astype(BF16)) for q in quads]
            step *= 2
        xb = [x[q].astype(BF16) for q in quads]
        yr = [_dot(a_rk[q], vpb[q]) - _dot(a_rb[q], xb[q]) + _dot(rbb[q], e2) for q in quads]
        yrb = [yr[q].astype(BF16) for q in quads]

        heads = [(q, hh) for q in quads for hh in range(RW_QUAD)]
        hsl = [slice(hh * RW_CHUNK, (hh + 1) * RW_CHUNK) for hh in range(RW_QUAD)]
        h_aug = [st_s[q * RW_QUAD + hh] for q, hh in heads]
        hb = [h.astype(BF16) for h in h_aug]
        ys = [_dot(yrb[q][hsl[hh]], hb[n]) + yr[q][hsl[hh], 0:nd] for n, (q, hh) in enumerate(heads)]
        us = [_dot(xb[q][hsl[hh]], hb[n]) + x[q][hsl[hh], 0:nd] for n, (q, hh) in enumerate(heads)]
        kv = [_dg(kt[q][hsl[hh]], vpb[q][hsl[hh], 0:nd], TN_DIMS) for q, hh in heads]
        bu = [_dg(btl[q][hsl[hh]], us[n].astype(BF16), TN_DIMS) for n, (q, hh) in enumerate(heads)]
        for n, (q, hh) in enumerate(heads):
            st_s[n, nd:2 * nd, :] = pc_t[q][:, hh * 128:hh * 128 + nd] * h_aug[n][nd:2 * nd, :] + kv[n] - bu[n]
        for n, (q, hh) in enumerate(heads):
            y = ys[n]
            mu = jnp.mean(y, axis=-1, keepdims=True)
            d = y - mu
            var = jnp.mean(d * d, axis=-1, keepdims=True)
            yn = d * lax.rsqrt(var + RW_GN_EPS) * lng_ref[n] + lnb_ref[n]
            hs = hsl[hh]
            bonus = jnp.sum(r[q][hs] * kq[q][hs] * rk_ref[n], axis=-1, keepdims=True) * vpad[q][hs, 0:nd]
            orow = slice(n * RW_CHUNK, (n + 1) * RW_CHUNK)
            y_s[c, orow, :] = (yn + bonus) * g_s[c, orow, :]
        return carry

    lax.fori_loop(0, n_chunks, chunk_body, 0)

    for c in range(n_chunks):
        for h in range(RW_HEADS):
            o_ref[0, c * RW_CHUNK:(c + 1) * RW_CHUNK, h * RW_HEAD_DIM:(h + 1) * RW_HEAD_DIM] = (
                y_s[c, h * RW_CHUNK:(h + 1) * RW_CHUNK, :].astype(o_ref.dtype))


def _rwkv(z3, zm3, mu_r, mu_k, mu_v, mu_m, w0, wb, a0, ab, gb, kk_p, ka_p, rk_p, lng, lnb):
    bsz, seq, _ = z3.shape
    tb = RW_TB
    n_chunks = tb // RW_CHUNK
    cb = COL_R // RW_WIDTH

    def zspec(col_blk, width):
        return pl.BlockSpec((1, tb, width), lambda b, i: (b, i, col_blk))

    def pspec(col_blk, width):
        return pl.BlockSpec((1, 8, width), lambda b, i: (b, jnp.maximum(i * (tb // 8) - 1, 0), col_blk))

    def full(shape):
        return pl.BlockSpec(shape, lambda b, i: (0,) * len(shape))

    stage = pltpu.VMEM((n_chunks, RW_HEADS * RW_CHUNK, RW_HEAD_DIM), F32)
    stage_wide = pltpu.VMEM((n_chunks, RW_HEADS * RW_CHUNK, 2 * RW_HEAD_DIM), F32)
    return pl.pallas_call(
        functools.partial(_rwkv_kernel, n_chunks=n_chunks),
        grid=(bsz, seq // tb),
        in_specs=[zspec(cb, RW_WIDTH), zspec(cb + 1, RW_WIDTH), zspec(cb + 2, RW_WIDTH),
                  zspec(0, MISC_W),
                  pspec(cb, RW_WIDTH), pspec(cb + 1, RW_WIDTH), pspec(cb + 2, RW_WIDTH),
                  pspec(0, MISC_W),
                  full((1, RW_WIDTH)), full((1, RW_WIDTH)), full((1, RW_WIDTH)), full((1, MISC_W)),
                  full((1, RW_WIDTH)), full((128, RW_WIDTH)), full((1, RW_WIDTH)), full((128, RW_WIDTH)),
                  full((256, RW_WIDTH)),
                  full((1, RW_WIDTH)), full((1, RW_WIDTH)),
                  full((RW_HEADS, 1, RW_HEAD_DIM)), full((RW_HEADS, 1, RW_HEAD_DIM)),
                  full((RW_HEADS, 1, RW_HEAD_DIM))],
        out_specs=pl.BlockSpec((1, tb, RW_WIDTH), lambda b, i: (b, i, 0)),
        out_shape=jax.ShapeDtypeStruct((bsz, seq, RW_WIDTH), BF16),
        scratch_shapes=[stage, stage, stage_wide, stage, stage, stage, stage, stage,
                        pltpu.VMEM((RW_HEADS, 2 * RW_HEAD_DIM, RW_HEAD_DIM), F32)],
        compiler_params=_params("parallel", "arbitrary"),
        name="rwkv7",
    )(z3, z3, z3, zm3, z3, z3, z3, zm3, mu_r, mu_k, mu_v, mu_m, w0, wb, a0, ab, gb,
      kk_p, ka_p, rk_p, lng, lnb)


FX_CUM_BLOCK = 256
FX_HEAD_GROUP = 4


def _fox_cum_kernel(zf_ref, fb_ref, col_ref, row_ref, *, seq):
    n = FX_CUM_BLOCK
    ri = lax.broadcasted_iota(jnp.int32, (n, n), 0)
    ci = lax.broadcasted_iota(jnp.int32, (n, n), 1)
    tri = jnp.where(ci <= ri, 1.0, 0.0).astype(BF16)
    carry = jnp.zeros((1, 128), F32)
    for j in range(seq // n):
        rs = slice(j * n, (j + 1) * n)
        x = zf_ref[0, rs, :] + fb_ref[...]
        lf = jnp.minimum(x, 0.0) - jnp.log(1.0 + jnp.exp(-jnp.abs(x)))
        hi = lf.astype(BF16)
        r1 = lf - hi.astype(F32)
        mid = r1.astype(BF16)
        lo = (r1 - mid.astype(F32)).astype(BF16)
        c = _dot(tri, hi) + _dot(tri, mid) + _dot(tri, lo) + carry
        col_ref[0, rs, :] = c
        row_ref[0, :, rs] = c.T[FX_F_LOCAL:FX_F_LOCAL + FX_HEADS, :]
        carry = c[n - 1:n, :]


def _fox_cum(zm3, fb_pad):
    bsz, seq, _ = zm3.shape
    return pl.pallas_call(
        functools.partial(_fox_cum_kernel, seq=seq),
        grid=(bsz,),
        in_specs=[pl.BlockSpec((1, seq, 128), lambda b: (b, 0, MISC_F_LANE // 128)),
                  pl.BlockSpec((1, 128), lambda b: (0, 0))],
        out_specs=[pl.BlockSpec((1, seq, 128), lambda b: (b, 0, 0)),
                   pl.BlockSpec((1, FX_HEADS, seq), lambda b: (b, 0, 0))],
        out_shape=[jax.ShapeDtypeStruct((bsz, seq, 128), F32),
                   jax.ShapeDtypeStruct((bsz, FX_HEADS, seq), F32)],
        compiler_params=_params("parallel"),
        name="fox_cum",
    )(zm3, fb_pad)


def _fox_kernel(q_ref, k_ref, v_ref, cc_ref, cr_ref, o_ref, m_s, l_s, acc_s, *, tq, tk):
    i = pl.program_id(1)
    scale = FX_HEAD_DIM ** -0.5
    rows = i * tq + lax.broadcasted_iota(jnp.int32, (tq, tk), 0)
    col0 = lax.broadcasted_iota(jnp.int32, (tq, tk), 1)
    n_kv = (i * tq + tq + tk - 1) // tk
    m_s[...] = jnp.full(m_s.shape, -1e30, F32)
    l_s[...] = jnp.zeros(l_s.shape, F32)
    acc_s[...] = jnp.zeros(acc_s.shape, F32)
    heads = [slice(h * FX_HEAD_DIM, (h + 1) * FX_HEAD_DIM) for h in range(FX_HEADS)]
    q_all = q_ref[0] * scale
    qs = [q_all[:, ls].astype(BF16) for ls in heads]
    cqs = [cc_ref[0, :, FX_F_LOCAL + h:FX_F_LOCAL + h + 1] for h in range(FX_HEADS)]

    def block(j, masked):
        ks = pl.multiple_of(j * tk, tk)
        kblk = k_ref[0, pl.ds(ks, tk), :].astype(BF16)
        vblk = v_ref[0, pl.ds(ks, tk), :].astype(BF16)
        for g0 in range(0, FX_HEADS, FX_HEAD_GROUP):
            hs = range(g0, g0 + FX_HEAD_GROUP)
            ss = {h: _dg(qs[h], kblk[:, heads[h]], NT_DIMS) for h in hs}
            ss = {h: ss[h] + cqs[h] - cr_ref[0, h, pl.ds(j, 1), :] for h in hs}
            if masked:
                keep = rows >= ks + col0
                ss = {h: jnp.where(keep, ss[h], -1e30) for h in hs}
            m_prev = {h: m_s[h] for h in hs}
            m_new = {h: jnp.maximum(m_prev[h], jnp.max(ss[h], axis=-1, keepdims=True)) for h in hs}
            ps = {h: jnp.exp(ss[h] - m_new[h]) for h in hs}
            alphas = {h: jnp.exp(m_prev[h] - m_new[h]) for h in hs}
            pvs = {h: _dot(ps[h].astype(BF16), vblk[:, heads[h]]) for h in hs}
            for h in hs:
                l_s[h] = alphas[h] * l_s[h] + jnp.sum(ps[h], axis=-1, keepdims=True)
                acc_s[h] = alphas[h] * acc_s[h] + pvs[h]
                m_s[h] = m_new[h]

    def visible_body(j, carry):
        block(j, masked=False)
        return carry

    lax.fori_loop(0, n_kv - 1, visible_body, 0)
    block(n_kv - 1, masked=True)
    for h, ls in enumerate(heads):
        o_ref[0, :, ls] = (acc_s[h] / l_s[h]).astype(o_ref.dtype)


def _fox(zfx3, cum_col, cum_row, tq=256, tk=512):
    bsz, seq, _ = zfx3.shape
    cb = 0
    assert tk % tq == 0 and seq % tk == 0
    cum_row4 = cum_row.reshape(bsz, FX_HEADS, seq // tk, tk)
    return pl.pallas_call(
        functools.partial(_fox_kernel, tq=tq, tk=tk),
        grid=(bsz, seq // tq),
        in_specs=[pl.BlockSpec((1, tq, FX_WIDTH), lambda b, i: (b, i, cb)),
                  pl.BlockSpec((1, seq, FX_WIDTH), lambda b, i: (b, 0, cb + 1)),
                  pl.BlockSpec((1, seq, FX_WIDTH), lambda b, i: (b, 0, cb + 2)),
                  pl.BlockSpec((1, tq, 128), lambda b, i: (b, i, 0)),
                  pl.BlockSpec((1, FX_HEADS, seq // tk, tk), lambda b, i: (b, 0, 0, 0))],
        out_specs=pl.BlockSpec((1, tq, FX_WIDTH), lambda b, i: (b, i, 0)),
        out_shape=jax.ShapeDtypeStruct((bsz, seq, FX_WIDTH), BF16),
        scratch_shapes=[pltpu.VMEM((FX_HEADS, tq, 1), F32), pltpu.VMEM((FX_HEADS, tq, 1), F32),
                        pltpu.VMEM((FX_HEADS, tq, FX_HEAD_DIM), F32)],
        compiler_params=_params("parallel", "parallel"),
        name="fox_attn",
    )(zfx3, zfx3, zfx3, cum_col, cum_row4)


def _merge_kernel(ya_ref, yb_ref, yc_ref, pa_ref, pb_ref, pc_ref, g0_ref, g1_ref, g2_ref, o_ref):
    acc = g0_ref[...].astype(F32) * _dot(ya_ref[...], pa_ref[...])
    acc = acc + g1_ref[...].astype(F32) * _dot(yb_ref[...], pb_ref[...])
    acc = acc + g2_ref[...].astype(F32) * _dot(yc_ref[...], pc_ref[...])
    o_ref[...] = acc.astype(o_ref.dtype)


def _merge(ya, yb, yc, pa, pb, pc, layer, gates, tm=1024, tn=512):
    m = ya.shape[0]
    per = D_MODEL // tn

    def yspec(width):
        return pl.BlockSpec((tm, width), lambda i, j: (i, 0))

    def wspec(width):
        return pl.BlockSpec((None, width, tn), lambda i, j: (layer, 0, j))

    def gspec(k):
        return pl.BlockSpec((tm, tn), lambda i, j: (i, k * per + j))

    return pl.pallas_call(
        _merge_kernel,
        grid=(m // tm, D_MODEL // tn),
        in_specs=[yspec(GM_WIDTH), yspec(RW_WIDTH), yspec(FX_WIDTH),
                  wspec(GM_WIDTH), wspec(RW_WIDTH), wspec(FX_WIDTH),
                  gspec(0), gspec(1), gspec(2)],
        out_specs=pl.BlockSpec((tm, tn), lambda i, j: (i, j)),
        out_shape=jax.ShapeDtypeStruct((m, D_MODEL), BF16),
        compiler_params=_params("parallel", "parallel"),
        name="merge",
    )(ya, yb, yc, pa, pb, pc, gates, gates, gates)


def _out_ln_kernel(m_ref, w_ref, x_ref, g_ref, b_ref, o32_ref, o16_ref):
    h = DEEPNORM_ALPHA * x_ref[...] + _dot(m_ref[...], w_ref[...])
    y = _layer_norm(h, g_ref[...], b_ref[...], LN_EPS)
    o32_ref[...] = y
    o16_ref[...] = y.astype(BF16)


def _out_ln(merged, w_out, layer, x, g, b, tm=512):
    m = x.shape[0]
    row = pl.BlockSpec((tm, D_MODEL), lambda i: (i, 0))
    vec = pl.BlockSpec((1, D_MODEL), lambda i: (0, 0))
    return pl.pallas_call(
        _out_ln_kernel,
        grid=(m // tm,),
        in_specs=[row, pl.BlockSpec((None, D_MODEL, D_MODEL), lambda i: (layer, 0, 0)), row, vec, vec],
        out_specs=[row, row],
        out_shape=[jax.ShapeDtypeStruct((m, D_MODEL), F32), jax.ShapeDtypeStruct((m, D_MODEL), BF16)],
        compiler_params=_params("parallel"),
        name="out_ln",
    )(merged, w_out, x, g, b)


def _router_kernel(x_ref, w_ref, b_ref, o_ref):
    logits = jnp.dot(x_ref[...], w_ref[...], preferred_element_type=F32,
                     precision=lax.Precision.HIGHEST) + b_ref[...]
    lane = lax.broadcasted_iota(jnp.int32, logits.shape, 1)
    logits = jnp.where(lane < MOE_EXPERTS, logits, -1e30)
    m1 = jnp.max(logits, axis=-1, keepdims=True)
    i1 = jnp.min(jnp.where(logits == m1, lane, 128), axis=-1, keepdims=True)
    rest = jnp.where(lane == i1, -1e30, logits)
    m2 = jnp.max(rest, axis=-1, keepdims=True)
    i2 = jnp.min(jnp.where(rest == m2, lane, 128), axis=-1, keepdims=True)
    e2 = jnp.exp(m2 - m1)
    w1 = 1.0 / (1.0 + e2)
    o_ref[...] = jnp.where(lane == i1, w1, 0.0) + jnp.where(lane == i2, e2 * w1, 0.0)


def _router(x1, w_pad, b_pad, tm=1024):
    m = x1.shape[0]
    return pl.pallas_call(
        _router_kernel,
        grid=(m // tm,),
        in_specs=[pl.BlockSpec((tm, D_MODEL), lambda i: (i, 0)),
                  pl.BlockSpec((D_MODEL, 128), lambda i: (0, 0)),
                  pl.BlockSpec((1, 128), lambda i: (0, 0))],
        out_specs=pl.BlockSpec((tm, 128), lambda i: (i, 0)),
        out_shape=jax.ShapeDtypeStruct((m, 128), F32),
        compiler_params=_params("parallel"),
        name="router",
    )(x1, w_pad, b_pad)


def _ffn_kernel(x_ref, c_ref, w1_ref, w3_ref, w2_ref, o_ref):
    @pl.when((pl.program_id(1) == 0) & (pl.program_id(2) == 0))
    def _():
        o_ref[...] = jnp.zeros_like(o_ref)

    x = x_ref[...]
    g = _dot(x, w1_ref[0])
    u = _dot(x, w3_ref[0])
    h = (g * _sigmoid(g)) * u * c_ref[0]
    o_ref[...] += _dot(h.astype(BF16), w2_ref[0])


def _ffn(xb, comb, w1, w3, w2, first, n_e, tm=1024, tf=512):
    m = xb.shape[0]
    f = w1.shape[-1]
    return pl.pallas_call(
        _ffn_kernel,
        grid=(m // tm, n_e, f // tf),
        in_specs=[pl.BlockSpec((tm, D_MODEL), lambda i, e, j: (i, 0)),
                  pl.BlockSpec((1, tm, 1), lambda i, e, j: (e, i, 0)),
                  pl.BlockSpec((1, D_MODEL, tf), lambda i, e, j: (first + e, 0, j)),
                  pl.BlockSpec((1, D_MODEL, tf), lambda i, e, j: (first + e, 0, j)),
                  pl.BlockSpec((1, tf, D_MODEL), lambda i, e, j: (first + e, j, 0))],
        out_specs=pl.BlockSpec((tm, D_MODEL), lambda i, e, j: (i, 0)),
        out_shape=jax.ShapeDtypeStruct((m, D_MODEL), F32),
        compiler_params=_params("parallel", "arbitrary", "arbitrary"),
        name="swiglu",
    )(xb, comb, w1, w3, w2)


MOE_BLK = 288
MOE_SUB = 1024
MOE_NSUB = 1
MOE_PASS = 4


def _moe_kernel(ntot_ref, bsub_ref, bloc_ref, x_ref, comb_ref, rank_ref, rankt_ref, w1_ref, w3_ref, w2_ref,
                o_ref, xs_s, acc_s, *, n_f, col_split):
    i, e, g, f = pl.program_id(0), pl.program_id(1), pl.program_id(2), pl.program_id(3)
    row = i * MOE_EXPERTS + e
    nb = jnp.clip(ntot_ref[row] - g * MOE_PASS, 0, MOE_PASS)

    @pl.when((e == 0) & (g == 0) & (f == 0))
    def _():
        o_ref[...] = jnp.zeros_like(o_ref)

    def sub_rows(b):
        sub = bsub_ref[row, g * MOE_PASS + b]
        return sub, pl.ds(pl.multiple_of(sub * MOE_SUB, MOE_SUB), MOE_SUB)

    @pl.when(f == 0)
    def _():
        slot = lax.broadcasted_iota(jnp.int32, (MOE_BLK, MOE_SUB), 0)

        def gather(b, carry):
            sub, rows = sub_rows(b)
            first = bloc_ref[row, g * MOE_PASS + b] * MOE_BLK
            rank_row = rankt_ref[sub, pl.ds(e, 1), :]
            onehot = jnp.where(rank_row == slot + first, 1.0, 0.0).astype(BF16)
            xs_s[b] = _dot(onehot, x_ref[rows, :]).astype(BF16)
            acc_s[b] = jnp.zeros(acc_s.shape[1:], F32)
            return carry

        lax.fori_loop(0, nb, gather, 0)

    def compute(b, carry):
        xb = xs_s[b]
        gate = _dot(xb, w1_ref[0])
        up = _dot(xb, w3_ref[0])
        h = (gate * _sigmoid(gate)) * up
        acc_s[b] += _dot(h.astype(BF16), w2_ref[0])
        return carry

    lax.fori_loop(0, nb, compute, 0)

    @pl.when(f == n_f - 1)
    def _():
        lane = lax.broadcasted_iota(jnp.int32, (MOE_SUB, 128), 1)
        sel = lane == e
        slot = lax.broadcasted_iota(jnp.int32, (MOE_SUB, MOE_BLK), 1)
        cw = D_MODEL // col_split

        def scatter(b, carry):
            _, rows = sub_rows(b)
            first = bloc_ref[row, g * MOE_PASS + b] * MOE_BLK
            c_col = jnp.sum(jnp.where(sel, comb_ref[rows, :], 0.0), axis=-1, keepdims=True)
            r_col = jnp.sum(jnp.where(sel, rank_ref[rows, :], 0), axis=-1, keepdims=True)
            onehot_t = jnp.where(r_col == slot + first, 1.0, 0.0).astype(BF16)
            for s in range(col_split):
                cs = slice(s * cw, (s + 1) * cw)
                o_ref[rows, cs] += _dot(onehot_t, acc_s[b, :, cs].astype(BF16)) * c_col
            return carry

        lax.fori_loop(0, nb, scatter, 0)


def _moe(xb, comb, plan, w1, w3, w2, layer, tf=512):
    rank, rank_t, ntot, bsub, bloc = plan
    m = xb.shape[0]
    tm = MOE_SUB * MOE_NSUB
    _, n_e, _, f = w1.shape
    n_f = f // tf
    blocks_max = MOE_NSUB * pl.cdiv(MOE_SUB, MOE_BLK)
    n_pass = pl.cdiv(blocks_max, MOE_PASS)
    once = pl.Buffered(1)

    def wmap(sel):
        def index_map(i, e, g, j, ntot_ref, bsub_ref, bloc_ref):
            live = (g == 0) | (ntot_ref[i * MOE_EXPERTS + e] > g * MOE_PASS)
            jj = jnp.where(live, j, n_f - 1)
            return (layer, e, 0, jj) if sel == 0 else (layer, e, jj, 0)
        return index_map

    def tile(i, e, g, j, *_):
        return (i, 0)

    grid_spec = pltpu.PrefetchScalarGridSpec(
        num_scalar_prefetch=3,
        grid=(m // tm, n_e, n_pass, n_f),
        in_specs=[pl.BlockSpec((tm, D_MODEL), tile, pipeline_mode=once),
                  pl.BlockSpec((tm, 128), tile, pipeline_mode=once),
                  pl.BlockSpec((tm, 128), tile, pipeline_mode=once),
                  pl.BlockSpec((MOE_NSUB, 128, MOE_SUB), lambda i, e, g, j, *_: (i, 0, 0), pipeline_mode=once),
                  pl.BlockSpec((None, 1, D_MODEL, tf), wmap(0)),
                  pl.BlockSpec((None, 1, D_MODEL, tf), wmap(0)),
                  pl.BlockSpec((None, 1, tf, D_MODEL), wmap(1))],
        out_specs=pl.BlockSpec((tm, D_MODEL), tile, pipeline_mode=once),
        scratch_shapes=[pltpu.VMEM((MOE_PASS, MOE_BLK, D_MODEL), BF16),
                        pltpu.VMEM((MOE_PASS, MOE_BLK, D_MODEL), F32)],
    )
    return pl.pallas_call(
        functools.partial(_moe_kernel, n_f=n_f, col_split=2),
        grid_spec=grid_spec,
        out_shape=jax.ShapeDtypeStruct((m, D_MODEL), F32),
        compiler_params=_params("parallel", "arbitrary", "arbitrary", "arbitrary"),
        name="moe",
    )(ntot, bsub, bloc, xb, comb, rank, rank_t, w1, w3, w2)


def _moe_plan(comb):
    m = comb.shape[0]
    n_sub = m // MOE_SUB
    routed = (comb > 0.0).reshape(n_sub, MOE_SUB, 128).astype(jnp.int32)
    rank = jnp.cumsum(routed, axis=1) - routed
    rank = jnp.where(routed > 0, rank, -1)
    count = jnp.sum(routed, axis=1)[:, :MOE_EXPERTS]
    nblk = ((count + MOE_BLK - 1) // MOE_BLK).reshape(n_sub // MOE_NSUB, MOE_NSUB, MOE_EXPERTS)
    ends = jnp.swapaxes(jnp.cumsum(nblk, axis=1), 1, 2)
    starts = ends - jnp.swapaxes(nblk, 1, 2)
    ntot = ends[:, :, -1].reshape(-1)
    blocks_max = MOE_NSUB * pl.cdiv(MOE_SUB, MOE_BLK)
    bidx = jnp.arange(blocks_max)[None, None, :, None]
    bsub = jnp.sum((bidx >= ends[:, :, None, :]).astype(jnp.int32), axis=-1)
    bsub = jnp.minimum(bsub, MOE_NSUB - 1)
    sub_onehot = (bsub[..., None] == jnp.arange(MOE_NSUB)).astype(jnp.int32)
    bloc = bidx[..., 0] - jnp.sum(sub_onehot * starts[:, :, None, :], axis=-1)
    shape2 = (-1, blocks_max)
    return (rank.reshape(m, 128), jnp.swapaxes(rank, 1, 2), ntot.astype(jnp.int32),
            bsub.reshape(shape2).astype(jnp.int32), bloc.reshape(shape2).astype(jnp.int32))


def _ple_kernel(x_ref, wg_ref, bg_ref, p_ref, wp_ref, o_ref):
    gate = _sigmoid(_dot(x_ref[...], wg_ref[...]) + bg_ref[...])
    o_ref[...] = gate * _dot(p_ref[...].astype(BF16), wp_ref[...])


def _ple(x1b, wg, bg, p, wp, layer, tm=1024, tn=1024):
    m = x1b.shape[0]
    return pl.pallas_call(
        _ple_kernel,
        grid=(m // tm, D_MODEL // tn),
        in_specs=[pl.BlockSpec((tm, D_MODEL), lambda i, j: (i, 0)),
                  pl.BlockSpec((None, D_MODEL, tn), lambda i, j: (layer, 0, j)),
                  pl.BlockSpec((1, tn), lambda i, j: (0, j)),
                  pl.BlockSpec((None, tm, PLE_DIM), lambda i, j: (layer, i, 0)),
                  pl.BlockSpec((None, PLE_DIM, tn), lambda i, j: (layer, 0, j))],
        out_specs=pl.BlockSpec((tm, tn), lambda i, j: (i, j)),
        out_shape=jax.ShapeDtypeStruct((m, D_MODEL), F32),
        compiler_params=_params("parallel", "parallel"),
        name="ple",
    )(x1b, wg, bg, p, wp)


def _final_ln_kernel(x1_ref, ff_ref, ple_ref, g_ref, b_ref, o32_ref, o16_ref):
    h = DEEPNORM_ALPHA * x1_ref[...] + ff_ref[...] + ple_ref[...]
    y = _layer_norm(h, g_ref[...], b_ref[...], LN_EPS)
    o32_ref[...] = y
    o16_ref[...] = y.astype(BF16)


def _final_ln(x1, ff, ple, g, b, tm=512):
    m = x1.shape[0]
    row = pl.BlockSpec((tm, D_MODEL), lambda i: (i, 0))
    vec = pl.BlockSpec((1, D_MODEL), lambda i: (0, 0))
    return pl.pallas_call(
        _final_ln_kernel,
        grid=(m // tm,),
        in_specs=[row, row, row, vec, vec],
        out_specs=[row, row],
        out_shape=[jax.ShapeDtypeStruct((m, D_MODEL), F32), jax.ShapeDtypeStruct((m, D_MODEL), BF16)],
        compiler_params=_params("parallel"),
        name="final_ln",
    )(x1, ff, ple, g, b)


def _split_w_in(w_in):
    w_t = jnp.swapaxes(w_in, 1, 2)
    n_l, _, d = w_t.shape
    lora = w_t[:, COL_LORA:COL_FX]
    f = w_t[:, COL_F:COL_GATE]
    pad = jnp.zeros((n_l, MISC_W - lora.shape[1] - f.shape[1], d), w_t.dtype)
    misc = jnp.concatenate([lora, f, pad], axis=1)
    return (w_t[:, :COL_LORA].astype(BF16), misc.astype(BF16),
            w_t[:, COL_FX:COL_F].astype(BF16), w_t[:, COL_GATE:].astype(BF16))


def _pad_rows(w, lo, total):
    n_l, r, n = w.shape
    return jnp.concatenate([jnp.zeros((n_l, lo, n), w.dtype), w,
                            jnp.zeros((n_l, total - lo - r, n), w.dtype)], axis=1)


def kernel(x, p, w_in, gate_b, gm_ln_g, gm_ln_b, gm_ws, gm_bs, rw_mu, rw_w0, rw_wb, rw_a0, rw_ab, rw_gb, rw_kk, rw_ka, rw_rk, rw_lnx_g, rw_lnx_b, fx_fb, proj_a, proj_b, proj_c, w_out, ln1_g, ln1_b, ffn_w1, ffn_w3, ffn_w2, moe_router, moe_router_b, moe_w1, moe_w3, moe_w2, ple_gate_w, ple_gate_b, ple_proj, ln2_g, ln2_b):
    bsz, seq, d = x.shape
    m = bsz * seq
    n_l = w_in.shape[0]

    w_ar, w_misc, w_fx, w_gate = _split_w_in(w_in)
    mu_r = rw_mu[:, None, 0:RW_WIDTH]
    mu_k = rw_mu[:, None, RW_WIDTH:2 * RW_WIDTH]
    mu_v = rw_mu[:, None, 2 * RW_WIDTH:3 * RW_WIDTH]
    o = 3 * RW_WIDTH
    zl = lambda n: jnp.zeros((n_l, n), F32)
    n_lora = RW_LORA_W + RW_LORA_A + RW_LORA_G
    mu_m = jnp.concatenate([rw_mu[:, o:o + n_lora], zl(MISC_W - n_lora)], axis=-1)[:, None, :]
    wb_pad = _pad_rows(rw_wb, 0, 128).astype(BF16)
    ab_pad = _pad_rows(rw_ab, RW_LORA_W, 128).astype(BF16)
    gb_pad = _pad_rows(rw_gb, 0, 256).astype(BF16)
    fb_pad = jnp.concatenate([zl(FX_F_LOCAL), fx_fb, zl(128 - FX_F_LOCAL - FX_HEADS)], axis=-1)[:, None, :]
    head3 = lambda t: t.reshape(n_l, RW_HEADS, 1, RW_HEAD_DIM)
    gm_bs3 = gm_bs[..., None]
    router_w = jnp.concatenate([moe_router, jnp.zeros(moe_router.shape[:2] + (128 - MOE_EXPERTS,), F32)], axis=-1)
    router_b = jnp.concatenate([moe_router_b, jnp.zeros((moe_router_b.shape[0], 128 - MOE_EXPERTS), F32)],
                               axis=-1)[:, None, :]
    bf = lambda t: t.astype(BF16)
    proj_a16, proj_b16, proj_c16, w_out16 = bf(proj_a), bf(proj_b), bf(proj_c), bf(w_out)
    ffn_w1_16, ffn_w3_16, ffn_w2_16 = bf(ffn_w1), bf(ffn_w3), bf(ffn_w2)
    moe_w1_16, moe_w3_16, moe_w2_16 = bf(moe_w1), bf(moe_w3), bf(moe_w2)
    ple_gw16, ple_pw16 = bf(ple_gate_w), bf(ple_proj)
    ones_comb = jnp.ones((1, m, 1), F32)
    p3 = p.reshape(n_l, m, PLE_DIM)

    x32 = x.reshape(m, d)
    x16 = x32.astype(BF16)
    for i in range(n_l):
        z = _matmul(x16, w_ar, i, F32, 1024, 1024, "in_proj_ar")
        zm = _matmul(x16, w_misc, i, F32, 1024, MISC_W, "in_proj_misc")
        zfx = _matmul(x16, w_fx, i, BF16, 1024, 3 * FX_WIDTH, "in_proj_fx")
        gates = _matmul(x16, w_gate, i, BF16, 1024, 1024, "in_proj_gate", gate_bias=gate_b[i][None])
        z3 = z.reshape(bsz, seq, z.shape[-1])
        zm3 = zm.reshape(bsz, seq, MISC_W)
        y_a = _gmlp(z, gm_ln_g[i][None], gm_ln_b[i][None], gm_ws[i], gm_bs3[i])
        y_b = _rwkv(z3, zm3, mu_r[i], mu_k[i], mu_v[i], mu_m[i], rw_w0[i][None], wb_pad[i], rw_a0[i][None],
                    ab_pad[i], gb_pad[i], rw_kk[i][None], rw_ka[i][None], head3(rw_rk)[i],
                    head3(rw_lnx_g)[i], head3(rw_lnx_b)[i]).reshape(m, RW_WIDTH)
        cum_col, cum_row = _fox_cum(zm3, fb_pad[i])
        y_c = _fox(zfx.reshape(bsz, seq, 3 * FX_WIDTH), cum_col, cum_row).reshape(m, FX_WIDTH)
        merged = _merge(y_a, y_b, y_c, proj_a16, proj_b16, proj_c16, i, gates)
        x1, x1b = _out_ln(merged, w_out16, i, x32, ln1_g[i][None], ln1_b[i][None])
        j = i // 2
        if i % 2 == 0:
            ff = _ffn(x1b, ones_comb, ffn_w1_16, ffn_w3_16, ffn_w2_16, j, 1)
        else:
            comb = _router(x1, router_w[j], router_b[j])
            ff = _moe(x1b, comb, _moe_plan(comb), moe_w1_16, moe_w3_16, moe_w2_16, j)
        ple = _ple(x1b, ple_gw16, ple_gate_b[i][None], p3, ple_pw16, i)
        x32, x16 = _final_ln(x1, ff, ple, ln2_g[i][None], ln2_b[i][None])
    return x32.reshape(bsz, seq, d)
```

```python
import functools
import math

import jax
import jax.numpy as jnp
from jax import lax
from jax.experimental import pallas as pl
from jax.experimental.pallas import tpu as pltpu

F32 = jnp.float32
BF16 = jnp.bfloat16

D_MODEL = 2048
DEPTH = 4
PLE_DIM = 256
GM_CHUNK = 128
GM_GROUPS = 4
GM_WIDTH = 512
RW_HEADS = 16
RW_HEAD_DIM = 64
RW_WIDTH = 1024
RW_LORA_W = 64
RW_LORA_A = 64
RW_LORA_G = 160
RW_GN_EPS = 64e-5
FX_HEADS = 8
FX_HEAD_DIM = 64
FX_WIDTH = 512
MOE_EXPERTS = 8
DEEPNORM_ALPHA = (2.0 * DEPTH) ** 0.25
LN_EPS = 1e-5

COL_A = 0
COL_R = 1024
COL_LORA = 4096
COL_FX = 4384
COL_F = 5920
COL_GATE = 5928
MISC_W = 384
MISC_F_LANE = COL_FX - COL_LORA
FX_F_LOCAL = MISC_F_LANE % 128

RW_CHUNK = 64
RW_QUAD = 2
RW_TB = 256
LOGW_SCALE = math.exp(-0.5)

VMEM_LIMIT = 56 * 1024 * 1024

NT_DIMS = (((1,), (1,)), ((), ()))
TN_DIMS = (((0,), (0,)), ((), ()))


def _params(*sem):
    return pltpu.CompilerParams(dimension_semantics=sem, vmem_limit_bytes=VMEM_LIMIT)


def _dot(a, b):
    return jnp.dot(a, b, preferred_element_type=F32)


def _dg(a, b, dims):
    return lax.dot_general(a, b, dims, preferred_element_type=F32)


def _sigmoid(x):
    return 1.0 / (1.0 + jnp.exp(-x))


def _layer_norm(h, g, b, eps):
    mu = jnp.mean(h, axis=-1, keepdims=True)
    d = h - mu
    var = jnp.mean(d * d, axis=-1, keepdims=True)
    return d * lax.rsqrt(var + eps) * g + b


def _mm_kernel(x_ref, w_ref, o_ref):
    o_ref[...] = _dg(x_ref[...], w_ref[...], NT_DIMS).astype(o_ref.dtype)


def _mm_gate_kernel(x_ref, w_ref, b_ref, o_ref):
    o_ref[...] = _sigmoid(_dg(x_ref[...], w_ref[...], NT_DIMS) + b_ref[...]).astype(o_ref.dtype)


def _matmul(x, w_t, layer, out_dtype, tm, tn, name, gate_bias=None):
    m, k = x.shape
    n = w_t.shape[1]
    in_specs = [pl.BlockSpec((tm, k), lambda i, j: (i, 0)),
                pl.BlockSpec((None, tn, k), lambda i, j: (layer, j, 0))]
    args = (x, w_t)
    body = _mm_kernel
    if gate_bias is not None:
        in_specs.append(pl.BlockSpec((1, tn), lambda i, j: (0, j)))
        args = (x, w_t, gate_bias)
        body = _mm_gate_kernel
    return pl.pallas_call(
        body,
        grid=(m // tm, n // tn),
        in_specs=in_specs,
        out_specs=pl.BlockSpec((tm, tn), lambda i, j: (i, j)),
        out_shape=jax.ShapeDtypeStruct((m, n), out_dtype),
        compiler_params=_params("parallel", "parallel"),
        name=name,
    )(*args)


def _gmlp_kernel(z_ref, g_ref, b_ref, ws_ref, bs_ref, o_ref, *, rows):
    z = z_ref[...]
    za = 0.5 * z * (1.0 + jnp.tanh(math.sqrt(2.0 / math.pi) * (z + 0.044715 * (z * z * z))))
    u = za[:, :GM_WIDTH]
    v = _layer_norm(za[:, GM_WIDTH:], g_ref[...], b_ref[...], LN_EPS).astype(BF16)
    t_idx = lax.broadcasted_iota(jnp.int32, (GM_CHUNK, GM_CHUNK), 0)
    s_idx = lax.broadcasted_iota(jnp.int32, (GM_CHUNK, GM_CHUNK), 1)
    causal = t_idx >= s_idx
    for g in range(GM_GROUPS):
        w = jnp.where(causal, ws_ref[g], 0.0).astype(BF16)
        cs = slice(g * GM_CHUNK, (g + 1) * GM_CHUNK)
        for c in range(rows // GM_CHUNK):
            rs = slice(c * GM_CHUNK, (c + 1) * GM_CHUNK)
            s = _dot(w, v[rs, cs]) + bs_ref[g]
            o_ref[rs, cs] = (u[rs, cs] * s).astype(o_ref.dtype)


def _gmlp(z, ln_g, ln_b, ws, bs, rows=512):
    m = z.shape[0]
    return pl.pallas_call(
        functools.partial(_gmlp_kernel, rows=rows),
        grid=(m // rows,),
        in_specs=[pl.BlockSpec((rows, 2 * GM_WIDTH), lambda i: (i, COL_A // (2 * GM_WIDTH))),
                  pl.BlockSpec((1, GM_WIDTH), lambda i: (0, 0)),
                  pl.BlockSpec((1, GM_WIDTH), lambda i: (0, 0)),
                  pl.BlockSpec((GM_GROUPS, GM_CHUNK, GM_CHUNK), lambda i: (0, 0, 0)),
                  pl.BlockSpec((GM_GROUPS, GM_CHUNK, 1), lambda i: (0, 0, 0))],
        out_specs=pl.BlockSpec((rows, GM_WIDTH), lambda i: (i, 0)),
        out_shape=jax.ShapeDtypeStruct((m, GM_WIDTH), BF16),
        compiler_params=_params("parallel"),
        name="gmlp",
    )(z, ln_g, ln_b, ws, bs)


def _token_shift(z_ref, p_ref, mu_ref, first):
    h = z_ref[0]
    prev_row = jnp.where(first, 0.0, p_ref[0, 7:8, :])
    hp = pltpu.roll(h, 1, axis=0)
    row = lax.broadcasted_iota(jnp.int32, h.shape, 0)
    hp = jnp.where(row == 0, prev_row, hp)
    return h + (hp - h) * mu_ref[...]


def _stage(dst_ref, val, n_chunks):
    for c in range(n_chunks):
        for h in range(RW_HEADS):
            dst_ref[c, h * RW_CHUNK:(h + 1) * RW_CHUNK, 0:RW_HEAD_DIM] = (
                val[c * RW_CHUNK:(c + 1) * RW_CHUNK, h * RW_HEAD_DIM:(h + 1) * RW_HEAD_DIM])


def _rwkv_kernel(zr_ref, zk_ref, zv_ref, zm_ref, pr_ref, pk_ref, pv_ref, pm_ref,
                 mur_ref, muk_ref, muv_ref, mum_ref, w0_ref, wb_ref, a0_ref, ab_ref, gb_ref,
                 kk_ref, ka_ref, rk_ref, lng_ref, lnb_ref, o_ref,
                 r_s, k_s, v_s, kk_s, a_s, lw_s, g_s, y_s, st_s, *, n_chunks):
    first = pl.program_id(1) == 0
    qrows = RW_QUAD * RW_CHUNK
    nd = RW_HEAD_DIM

    @pl.when(first)
    def _():
        st_s[...] = jnp.zeros_like(st_s)
        v_s[...] = jnp.zeros_like(v_s)

    m = _token_shift(zm_ref, pm_ref, mum_ref, first)
    m01 = m[:, 0:128]
    lw_lin = _dot(jnp.tanh(m01).astype(BF16), wb_ref[...]) + w0_ref[...]
    _stage(lw_s, -LOGW_SCALE * _sigmoid(lw_lin), n_chunks)
    a = _sigmoid(a0_ref[...] + _dot(m01.astype(BF16), ab_ref[...]))
    _stage(a_s, a, n_chunks)
    _stage(g_s, _dot(_sigmoid(m[:, 128:384]).astype(BF16), gb_ref[...]), n_chunks)
    k = _token_shift(zk_ref, pk_ref, muk_ref, first)
    _stage(kk_s, k * kk_ref[...], n_chunks)
    _stage(k_s, k * (1.0 + (a - 1.0) * ka_ref[...]), n_chunks)
    _stage(r_s, _token_shift(zr_ref, pr_ref, mur_ref, first), n_chunks)
    _stage(v_s, _token_shift(zv_ref, pv_ref, muv_ref, first), n_chunks)

    ri = lax.broadcasted_iota(jnp.int32, (qrows, qrows), 0)
    ci = lax.broadcasted_iota(jnp.int32, (qrows, qrows), 1)
    shift = RW_CHUNK.bit_length() - 1
    same = (ri >> shift) == (ci >> shift)
    strict = same & (ci < ri)
    incl = same & (ci <= ri)
    tri_bd = jnp.where(incl, 1.0, 0.0).astype(BF16)
    ones_sel = jnp.where(
        (lax.broadcasted_iota(jnp.int32, (qrows, RW_QUAD * 128), 0) >> shift)
        == (lax.broadcasted_iota(jnp.int32, (qrows, RW_QUAD * 128), 1) >> 7), 1.0, 0.0).astype(BF16)
    e2 = jnp.where(lax.broadcasted_iota(jnp.int32, (nd, 2 * nd), 1)
                   == lax.broadcasted_iota(jnp.int32, (nd, 2 * nd), 0) + nd, 1.0, 0.0).astype(BF16)

    def chunk_body(c, carry):
        quads = range(RW_HEADS // RW_QUAD)
        rows = [slice(q * qrows, (q + 1) * qrows) for q in quads]
        r = [r_s[c, rows[q], :] for q in quads]
        kq = [k_s[c, rows[q], :] for q in quads]
        vpad = [v_s[c, rows[q], :] for q in quads]
        vpb = [vpad[q].astype(BF16) for q in quads]
        kkr = [kk_s[c, rows[q], :] for q in quads]
        lw = [lw_s[c, rows[q], :] for q in quads]
        hi = [lw[q].astype(BF16) for q in quads]
        lo = [(lw[q] - hi[q].astype(F32)).astype(BF16) for q in quads]
        cum = [_dot(tri_bd, hi[q]) + _dot(tri_bd, lo[q]) for q in quads]
        tot = [jnp.concatenate([jnp.broadcast_to(cum[q][(hh + 1) * RW_CHUNK - 1:(hh + 1) * RW_CHUNK, :], (RW_CHUNK, nd))
                                for hh in range(RW_QUAD)], axis=0) for q in quads]
        pc_t = [jnp.exp(_dg(hi[q], ones_sel, TN_DIMS) + _dg(lo[q], ones_sel, TN_DIMS)) for q in quads]
        nrm = [jnp.sqrt(jnp.sum(kkr[q] * kkr[q], axis=-1, keepdims=True)) for q in quads]
        kk = [kkr[q] / jnp.maximum(nrm[q], 1e-12) for q in quads]
        bt0 = [kk[q] * a_s[c, rows[q], :] for q in quads]
        cum_ex = [cum[q] - (hi[q].astype(F32) + lo[q].astype(F32)) for q in quads]
        e_neg = [jnp.exp(-cum[q]) for q in quads]
        alb = [(kk[q] * jnp.exp(cum_ex[q])).astype(BF16) for q in quads]
        rbb = [(r[q] * jnp.exp(cum[q])).astype(BF16) for q in quads]
        kh = [(kq[q] * e_neg[q]).astype(BF16) for q in quads]
        bh = [(bt0[q] * e_neg[q]).astype(BF16) for q in quads]
        e_tail = [jnp.exp(tot[q] - cum[q]) for q in quads]
        kt = [(kq[q] * e_tail[q]).astype(BF16) for q in quads]
        btl = [(bt0[q] * e_tail[q]).astype(BF16) for q in quads]
        lmat = [jnp.where(strict, _dg(alb[q], bh[q], NT_DIMS), 0.0).astype(BF16) for q in quads]
        a_ak = [jnp.where(strict, _dg(alb[q], kh[q], NT_DIMS), 0.0).astype(BF16) for q in quads]
        a_rb = [jnp.where(incl, _dg(rbb[q], bh[q], NT_DIMS), 0.0).astype(BF16) for q in quads]
        a_rk = [jnp.where(incl, _dg(rbb[q], kh[q], NT_DIMS), 0.0).astype(BF16) for q in quads]
        x = [_dot(a_ak[q], vpb[q]) + _dot(alb[q], e2) for q in quads]
        x = [x[q] - _dot(lmat[q], x[q].astype(BF16)) for q in quads]
        pw = lmat
        step = 2
        while step < RW_CHUNK:
            pw = [_dot(pw[q], pw[q]).astype(BF16) for q in quads]
            x = [x[q] + _dot(pw[q], x[q].astype(BF16)) for q in quads]
            step *= 2
        xb = [x[q].astype(BF16) for q in quads]
        yr = [_dot(a_rk[q], vpb[q]) - _dot(a_rb[q], xb[q]) + _dot(rbb[q], e2) for q in quads]
        yrb = [yr[q].astype(BF16) for q in quads]

        heads = [(q, hh) for q in quads for hh in range(RW_QUAD)]
        hsl = [slice(hh * RW_CHUNK, (hh + 1) * RW_CHUNK) for hh in range(RW_QUAD)]
        h_aug = [st_s[q * RW_QUAD + hh] for q, hh in heads]
        hb = [h.astype(BF16) for h in h_aug]
        ys = [_dot(yrb[q][hsl[hh]], hb[n]) + yr[q][hsl[hh], 0:nd] for n, (q, hh) in enumerate(heads)]
        us = [_dot(xb[q][hsl[hh]], hb[n]) + x[q][hsl[hh], 0:nd] for n, (q, hh) in enumerate(heads)]
        kv = [_dg(kt[q][hsl[hh]], vpb[q][hsl[hh], 0:nd], TN_DIMS) for q, hh in heads]
        bu = [_dg(btl[q][hsl[hh]], us[n].astype(BF16), TN_DIMS) for n, (q, hh) in enumerate(heads)]
        for n, (q, hh) in enumerate(heads):
            st_s[n, nd:2 * nd, :] = pc_t[q][:, hh * 128:hh * 128 + nd] * h_aug[n][nd:2 * nd, :] + kv[n] - bu[n]
        for n, (q, hh) in enumerate(heads):
            y = ys[n]
            mu = jnp.mean(y, axis=-1, keepdims=True)
            d = y - mu
            var = jnp.mean(d * d, axis=-1, keepdims=True)
            yn = d * lax.rsqrt(var + RW_GN_EPS) * lng_ref[n] + lnb_ref[n]
            hs = hsl[hh]
            bonus = jnp.sum(r[q][hs] * kq[q][hs] * rk_ref[n], axis=-1, keepdims=True) * vpad[q][hs, 0:nd]
            orow = slice(n * RW_CHUNK, (n + 1) * RW_CHUNK)
            y_s[c, orow, :] = (yn + bonus) * g_s[c, orow, :]
        return carry

    lax.fori_loop(0, n_chunks, chunk_body, 0)

    for c in range(n_chunks):
        for h in range(RW_HEADS):
            o_ref[0, c * RW_CHUNK:(c + 1) * RW_CHUNK, h * RW_HEAD_DIM:(h + 1) * RW_HEAD_DIM] = (
                y_s[c, h * RW_CHUNK:(h + 1) * RW_CHUNK, :].astype(o_ref.dtype))


def _rwkv(z3, zm3, mu_r, mu_k, mu_v, mu_m, w0, wb, a0, ab, gb, kk_p, ka_p, rk_p, lng, lnb):
    bsz, seq, _ = z3.shape
    tb = RW_TB
    n_chunks = tb // RW_CHUNK
    cb = COL_R // RW_WIDTH

    def zspec(col_blk, width):
        return pl.BlockSpec((1, tb, width), lambda b, i: (b, i, col_blk))

    def pspec(col_blk, width):
        return pl.BlockSpec((1, 8, width), lambda b, i: (b, jnp.maximum(i * (tb // 8) - 1, 0), col_blk))

    def full(shape):
        return pl.BlockSpec(shape, lambda b, i: (0,) * len(shape))

    stage = pltpu.VMEM((n_chunks, RW_HEADS * RW_CHUNK, RW_HEAD_DIM), F32)
    stage_wide = pltpu.VMEM((n_chunks, RW_HEADS * RW_CHUNK, 2 * RW_HEAD_DIM), F32)
    return pl.pallas_call(
        functools.partial(_rwkv_kernel, n_chunks=n_chunks),
        grid=(bsz, seq // tb),
        in_specs=[zspec(cb, RW_WIDTH), zspec(cb + 1, RW_WIDTH), zspec(cb + 2, RW_WIDTH),
                  zspec(0, MISC_W),
                  pspec(cb, RW_WIDTH), pspec(cb + 1, RW_WIDTH), pspec(cb + 2, RW_WIDTH),
                  pspec(0, MISC_W),
                  full((1, RW_WIDTH)), full((1, RW_WIDTH)), full((1, RW_WIDTH)), full((1, MISC_W)),
                  full((1, RW_WIDTH)), full((128, RW_WIDTH)), full((1, RW_WIDTH)), full((128, RW_WIDTH)),
                  full((256, RW_WIDTH)),
                  full((1, RW_WIDTH)), full((1, RW_WIDTH)),
                  full((RW_HEADS, 1, RW_HEAD_DIM)), full((RW_HEADS, 1, RW_HEAD_DIM)),
                  full((RW_HEADS, 1, RW_HEAD_DIM))],
        out_specs=pl.BlockSpec((1, tb, RW_WIDTH), lambda b, i: (b, i, 0)),
        out_shape=jax.ShapeDtypeStruct((bsz, seq, RW_WIDTH), BF16),
        scratch_shapes=[stage, stage, stage_wide, stage, stage, stage, stage, stage,
                        pltpu.VMEM((RW_HEADS, 2 * RW_HEAD_DIM, RW_HEAD_DIM), F32)],
        compiler_params=_params("parallel", "arbitrary"),
        name="rwkv7",
    )(z3, z3, z3, zm3, z3, z3, z3, zm3, mu_r, mu_k, mu_v, mu_m, w0, wb, a0, ab, gb,
      kk_p, ka_p, rk_p, lng, lnb)


FX_CUM_BLOCK = 256
FX_HEAD_GROUP = 4


def _fox_cum_kernel(zf_ref, fb_ref, col_ref, row_ref, *, seq):
    n = FX_CUM_BLOCK
    ri = lax.broadcasted_iota(jnp.int32, (n, n), 0)
    ci = lax.broadcasted_iota(jnp.int32, (n, n), 1)
    tri = jnp.where(ci <= ri, 1.0, 0.0).astype(BF16)
    carry = jnp.zeros((1, 128), F32)
    for j in range(seq // n):
        rs = slice(j * n, (j + 1) * n)
        x = zf_ref[0, rs, :] + fb_ref[...]
        lf = jnp.minimum(x, 0.0) - jnp.log(1.0 + jnp.exp(-jnp.abs(x)))
        hi = lf.astype(BF16)
        r1 = lf - hi.astype(F32)
        mid = r1.astype(BF16)
        lo = (r1 - mid.astype(F32)).astype(BF16)
        c = _dot(tri, hi) + _dot(tri, mid) + _dot(tri, lo) + carry
        col_ref[0, rs, :] = c
        row_ref[0, :, rs] = c.T[FX_F_LOCAL:FX_F_LOCAL + FX_HEADS, :]
        carry = c[n - 1:n, :]


def _fox_cum(zm3, fb_pad):
    bsz, seq, _ = zm3.shape
    return pl.pallas_call(
        functools.partial(_fox_cum_kernel, seq=seq),
        grid=(bsz,),
        in_specs=[pl.BlockSpec((1, seq, 128), lambda b: (b, 0, MISC_F_LANE // 128)),
                  pl.BlockSpec((1, 128), lambda b: (0, 0))],
        out_specs=[pl.BlockSpec((1, seq, 128), lambda b: (b, 0, 0)),
                   pl.BlockSpec((1, FX_HEADS, seq), lambda b: (b, 0, 0))],
        out_shape=[jax.ShapeDtypeStruct((bsz, seq, 128), F32),
                   jax.ShapeDtypeStruct((bsz, FX_HEADS, seq), F32)],
        compiler_params=_params("parallel"),
        name="fox_cum",
    )(zm3, fb_pad)


def _fox_kernel(q_ref, k_ref, v_ref, cc_ref, cr_ref, o_ref, m_s, l_s, acc_s, *, tq, tk):
    i = pl.program_id(1)
    scale = FX_HEAD_DIM ** -0.5
    rows = i * tq + lax.broadcasted_iota(jnp.int32, (tq, tk), 0)
    col0 = lax.broadcasted_iota(jnp.int32, (tq, tk), 1)
    n_kv = (i * tq + tq + tk - 1) // tk
    m_s[...] = jnp.full(m_s.shape, -1e30, F32)
    l_s[...] = jnp.zeros(l_s.shape, F32)
    acc_s[...] = jnp.zeros(acc_s.shape, F32)
    heads = [slice(h * FX_HEAD_DIM, (h + 1) * FX_HEAD_DIM) for h in range(FX_HEADS)]
    q_all = q_ref[0] * scale
    qs = [q_all[:, ls].astype(BF16) for ls in heads]
    cqs = [cc_ref[0, :, FX_F_LOCAL + h:FX_F_LOCAL + h + 1] for h in range(FX_HEADS)]

    def block(j, masked):
        ks = pl.multiple_of(j * tk, tk)
        kblk = k_ref[0, pl.ds(ks, tk), :].astype(BF16)
        vblk = v_ref[0, pl.ds(ks, tk), :].astype(BF16)
        for g0 in range(0, FX_HEADS, FX_HEAD_GROUP):
            hs = range(g0, g0 + FX_HEAD_GROUP)
            ss = {h: _dg(qs[h], kblk[:, heads[h]], NT_DIMS) for h in hs}
            ss = {h: ss[h] + cqs[h] - cr_ref[0, h, pl.ds(j, 1), :] for h in hs}
            if masked:
                keep = rows >= ks + col0
                ss = {h: jnp.where(keep, ss[h], -1e30) for h in hs}
            m_prev = {h: m_s[h] for h in hs}
            m_new = {h: jnp.maximum(m_prev[h], jnp.max(ss[h], axis=-1, keepdims=True)) for h in hs}
            ps = {h: jnp.exp(ss[h] - m_new[h]) for h in hs}
            alphas = {h: jnp.exp(m_prev[h] - m_new[h]) for h in hs}
            pvs = {h: _dot(ps[h].astype(BF16), vblk[:, heads[h]]) for h in hs}
            for h in hs:
                l_s[h] = alphas[h] * l_s[h] + jnp.sum(ps[h], axis=-1, keepdims=True)
                acc_s[h] = alphas[h] * acc_s[h] + pvs[h]
                m_s[h] = m_new[h]

    def visible_body(j, carry):
        block(j, masked=False)
        return carry

    lax.fori_loop(0, n_kv - 1, visible_body, 0)
    block(n_kv - 1, masked=True)
    for h, ls in enumerate(heads):
        o_ref[0, :, ls] = (acc_s[h] / l_s[h]).astype(o_ref.dtype)


def _fox(zfx3, cum_col, cum_row, tq=256, tk=512):
    bsz, seq, _ = zfx3.shape
    cb = 0
    assert tk % tq == 0 and seq % tk == 0
    cum_row4 = cum_row.reshape(bsz, FX_HEADS, seq // tk, tk)
    return pl.pallas_call(
        functools.partial(_fox_kernel, tq=tq, tk=tk),
        grid=(bsz, seq // tq),
        in_specs=[pl.BlockSpec((1, tq, FX_WIDTH), lambda b, i: (b, i, cb)),
                  pl.BlockSpec((1, seq, FX_WIDTH), lambda b, i: (b, 0, cb + 1)),
                  pl.BlockSpec((1, seq, FX_WIDTH), lambda b, i: (b, 0, cb + 2)),
                  pl.BlockSpec((1, tq, 128), lambda b, i: (b, i, 0)),
                  pl.BlockSpec((1, FX_HEADS, seq // tk, tk), lambda b, i: (b, 0, 0, 0))],
        out_specs=pl.BlockSpec((1, tq, FX_WIDTH), lambda b, i: (b, i, 0)),
        out_shape=jax.ShapeDtypeStruct((bsz, seq, FX_WIDTH), BF16),
        scratch_shapes=[pltpu.VMEM((FX_HEADS, tq, 1), F32), pltpu.VMEM((FX_HEADS, tq, 1), F32),
                        pltpu.VMEM((FX_HEADS, tq, FX_HEAD_DIM), F32)],
        compiler_params=_params("parallel", "parallel"),
        name="fox_attn",
    )(zfx3, zfx3, zfx3, cum_col, cum_row4)


def _merge_kernel(ya_ref, yb_ref, yc_ref, pa_ref, pb_ref, pc_ref, g0_ref, g1_ref, g2_ref, o_ref):
    acc = g0_ref[...].astype(F32) * _dot(ya_ref[...], pa_ref[...])
    acc = acc + g1_ref[...].astype(F32) * _dot(yb_ref[...], pb_ref[...])
    acc = acc + g2_ref[...].astype(F32) * _dot(yc_ref[...], pc_ref[...])
    o_ref[...] = acc.astype(o_ref.dtype)


def _merge(ya, yb, yc, pa, pb, pc, layer, gates, tm=1024, tn=512):
    m = ya.shape[0]
    per = D_MODEL // tn

    def yspec(width):
        return pl.BlockSpec((tm, width), lambda i, j: (i, 0))

    def wspec(width):
        return pl.BlockSpec((None, width, tn), lambda i, j: (layer, 0, j))

    def gspec(k):
        return pl.BlockSpec((tm, tn), lambda i, j: (i, k * per + j))

    return pl.pallas_call(
        _merge_kernel,
        grid=(m // tm, D_MODEL // tn),
        in_specs=[yspec(GM_WIDTH), yspec(RW_WIDTH), yspec(FX_WIDTH),
                  wspec(GM_WIDTH), wspec(RW_WIDTH), wspec(FX_WIDTH),
                  gspec(0), gspec(1), gspec(2)],
        out_specs=pl.BlockSpec((tm, tn), lambda i, j: (i, j)),
        out_shape=jax.ShapeDtypeStruct((m, D_MODEL), BF16),
        compiler_params=_params("parallel", "parallel"),
        name="merge",
    )(ya, yb, yc, pa, pb, pc, gates, gates, gates)


def _out_ln_kernel(m_ref, w_ref, x_ref, g_ref, b_ref, o32_ref, o16_ref):
    h = DEEPNORM_ALPHA * x_ref[...] + _dot(m_ref[...], w_ref[...])
    y = _layer_norm(h, g_ref[...], b_ref[...], LN_EPS)
    o32_ref[...] = y
    o16_ref[...] = y.astype(BF16)


def _out_ln(merged, w_out, layer, x, g, b, tm=512):
    m = x.shape[0]
    row = pl.BlockSpec((tm, D_MODEL), lambda i: (i, 0))
    vec = pl.BlockSpec((1, D_MODEL), lambda i: (0, 0))
    return pl.pallas_call(
        _out_ln_kernel,
        grid=(m // tm,),
        in_specs=[row, pl.BlockSpec((None, D_MODEL, D_MODEL), lambda i: (layer, 0, 0)), row, vec, vec],
        out_specs=[row, row],
        out_shape=[jax.ShapeDtypeStruct((m, D_MODEL), F32), jax.ShapeDtypeStruct((m, D_MODEL), BF16)],
        compiler_params=_params("parallel"),
        name="out_ln",
    )(merged, w_out, x, g, b)


def _router_kernel(x_ref, w_ref, b_ref, o_ref):
    logits = jnp.dot(x_ref[...], w_ref[...], preferred_element_type=F32,
                     precision=lax.Precision.HIGHEST) + b_ref[...]
    lane = lax.broadcasted_iota(jnp.int32, logits.shape, 1)
    logits = jnp.where(lane < MOE_EXPERTS, logits, -1e30)
    m1 = jnp.max(logits, axis=-1, keepdims=True)
    i1 = jnp.min(jnp.where(logits == m1, lane, 128), axis=-1, keepdims=True)
    rest = jnp.where(lane == i1, -1e30, logits)
    m2 = jnp.max(rest, axis=-1, keepdims=True)
    i2 = jnp.min(jnp.where(rest == m2, lane, 128), axis=-1, keepdims=True)
    e2 = jnp.exp(m2 - m1)
    w1 = 1.0 / (1.0 + e2)
    o_ref[...] = jnp.where(lane == i1, w1, 0.0) + jnp.where(lane == i2, e2 * w1, 0.0)


def _router(x1, w_pad, b_pad, tm=1024):
    m = x1.shape[0]
    return pl.pallas_call(
        _router_kernel,
        grid=(m // tm,),
        in_specs=[pl.BlockSpec((tm, D_MODEL), lambda i: (i, 0)),
                  pl.BlockSpec((D_MODEL, 128), lambda i: (0, 0)),
                  pl.BlockSpec((1, 128), lambda i: (0, 0))],
        out_specs=pl.BlockSpec((tm, 128), lambda i: (i, 0)),
        out_shape=jax.ShapeDtypeStruct((m, 128), F32),
        compiler_params=_params("parallel"),
        name="router",
    )(x1, w_pad, b_pad)


def _ffn_kernel(x_ref, c_ref, w1_ref, w3_ref, w2_ref, o_ref):
    @pl.when((pl.program_id(1) == 0) & (pl.program_id(2) == 0))
    def _():
        o_ref[...] = jnp.zeros_like(o_ref)

    x = x_ref[...]
    g = _dot(x, w1_ref[0])
    u = _dot(x, w3_ref[0])
    h = (g * _sigmoid(g)) * u * c_ref[0]
    o_ref[...] += _dot(h.astype(BF16), w2_ref[0])


def _ffn(xb, comb, w1, w3, w2, first, n_e, tm=1024, tf=512):
    m = xb.shape[0]
    f = w1.shape[-1]
    return pl.pallas_call(
        _ffn_kernel,
        grid=(m // tm, n_e, f // tf),
        in_specs=[pl.BlockSpec((tm, D_MODEL), lambda i, e, j: (i, 0)),
                  pl.BlockSpec((1, tm, 1), lambda i, e, j: (e, i, 0)),
                  pl.BlockSpec((1, D_MODEL, tf), lambda i, e, j: (first + e, 0, j)),
                  pl.BlockSpec((1, D_MODEL, tf), lambda i, e, j: (first + e, 0, j)),
                  pl.BlockSpec((1, tf, D_MODEL), lambda i, e, j: (first + e, j, 0))],
        out_specs=pl.BlockSpec((tm, D_MODEL), lambda i, e, j: (i, 0)),
        out_shape=jax.ShapeDtypeStruct((m, D_MODEL), F32),
        compiler_params=_params("parallel", "arbitrary", "arbitrary"),
        name="swiglu",
    )(xb, comb, w1, w3, w2)


MOE_BLK = 288
MOE_SUB = 1024
MOE_NSUB = 1
MOE_PASS = 4


def _moe_kernel(ntot_ref, bsub_ref, bloc_ref, x_ref, comb_ref, rank_ref, rankt_ref, w1_ref, w3_ref, w2_ref,
                o_ref, xs_s, acc_s, *, n_f, col_split):
    i, e, g, f = pl.program_id(0), pl.program_id(1), pl.program_id(2), pl.program_id(3)
    row = i * MOE_EXPERTS + e
    nb = jnp.clip(ntot_ref[row] - g * MOE_PASS, 0, MOE_PASS)

    @pl.when((e == 0) & (g == 0) & (f == 0))
    def _():
        o_ref[...] = jnp.zeros_like(o_ref)

    def sub_rows(b):
        sub = bsub_ref[row, g * MOE_PASS + b]
        return sub, pl.ds(pl.multiple_of(sub * MOE_SUB, MOE_SUB), MOE_SUB)

    @pl.when(f == 0)
    def _():
        slot = lax.broadcasted_iota(jnp.int32, (MOE_BLK, MOE_SUB), 0)

        def gather(b, carry):
            sub, rows = sub_rows(b)
            first = bloc_ref[row, g * MOE_PASS + b] * MOE_BLK
            rank_row = rankt_ref[sub, pl.ds(e, 1), :]
            onehot = jnp.where(rank_row == slot + first, 1.0, 0.0).astype(BF16)
            xs_s[b] = _dot(onehot, x_ref[rows, :]).astype(BF16)
            acc_s[b] = jnp.zeros(acc_s.shape[1:], F32)
            return carry

        lax.fori_loop(0, nb, gather, 0)

    def compute(b, carry):
        xb = xs_s[b]
        gate = _dot(xb, w1_ref[0])
        up = _dot(xb, w3_ref[0])
        h = (gate * _sigmoid(gate)) * up
        acc_s[b] += _dot(h.astype(BF16), w2_ref[0])
        return carry

    lax.fori_loop(0, nb, compute, 0)

    @pl.when(f == n_f - 1)
    def _():
        lane = lax.broadcasted_iota(jnp.int32, (MOE_SUB, 128), 1)
        sel = lane == e
        slot = lax.broadcasted_iota(jnp.int32, (MOE_SUB, MOE_BLK), 1)
        cw = D_MODEL // col_split

        def scatter(b, carry):
            _, rows = sub_rows(b)
            first = bloc_ref[row, g * MOE_PASS + b] * MOE_BLK
            c_col = jnp.sum(jnp.where(sel, comb_ref[rows, :], 0.0), axis=-1, keepdims=True)
            r_col = jnp.sum(jnp.where(sel, rank_ref[rows, :], 0), axis=-1, keepdims=True)
            onehot_t = jnp.where(r_col == slot + first, 1.0, 0.0).astype(BF16)
            for s in range(col_split):
                cs = slice(s * cw, (s + 1) * cw)
                o_ref[rows, cs] += _dot(onehot_t, acc_s[b, :, cs].astype(BF16)) * c_col
            return carry

        lax.fori_loop(0, nb, scatter, 0)


def _moe(xb, comb, plan, w1, w3, w2, layer, tf=1024):
    rank, rank_t, ntot, bsub, bloc = plan
    m = xb.shape[0]
    tm = MOE_SUB * MOE_NSUB
    _, n_e, _, f = w1.shape
    n_f = f // tf
    blocks_max = MOE_NSUB * pl.cdiv(MOE_SUB, MOE_BLK)
    n_pass = pl.cdiv(blocks_max, MOE_PASS)
    once = pl.Buffered(1)

    def wmap(sel):
        def index_map(i, e, g, j, ntot_ref, bsub_ref, bloc_ref):
            live = (g == 0) | (ntot_ref[i * MOE_EXPERTS + e] > g * MOE_PASS)
            jj = jnp.where(live, j, n_f - 1)
            return (layer, e, 0, jj) if sel == 0 else (layer, e, jj, 0)
        return index_map

    def tile(i, e, g, j, *_):
        return (i, 0)

    grid_spec = pltpu.PrefetchScalarGridSpec(
        num_scalar_prefetch=3,
        grid=(m // tm, n_e, n_pass, n_f),
        in_specs=[pl.BlockSpec((tm, D_MODEL), tile, pipeline_mode=once),
                  pl.BlockSpec((tm, 128), tile, pipeline_mode=once),
                  pl.BlockSpec((tm, 128), tile, pipeline_mode=once),
                  pl.BlockSpec((MOE_NSUB, 128, MOE_SUB), lambda i, e, g, j, *_: (i, 0, 0), pipeline_mode=once),
                  pl.BlockSpec((None, 1, D_MODEL, tf), wmap(0)),
                  pl.BlockSpec((None, 1, D_MODEL, tf), wmap(0)),
                  pl.BlockSpec((None, 1, tf, D_MODEL), wmap(1))],
        out_specs=pl.BlockSpec((tm, D_MODEL), tile, pipeline_mode=once),
        scratch_shapes=[pltpu.VMEM((MOE_PASS, MOE_BLK, D_MODEL), BF16),
                        pltpu.VMEM((MOE_PASS, MOE_BLK, D_MODEL), F32)],
    )
    return pl.pallas_call(
        functools.partial(_moe_kernel, n_f=n_f, col_split=2),
        grid_spec=grid_spec,
        out_shape=jax.ShapeDtypeStruct((m, D_MODEL), F32),
        compiler_params=_params("parallel", "arbitrary", "arbitrary", "arbitrary"),
        name="moe",
    )(ntot, bsub, bloc, xb, comb, rank, rank_t, w1, w3, w2)


def _moe_plan(comb):
    m = comb.shape[0]
    n_sub = m // MOE_SUB
    routed = (comb > 0.0).reshape(n_sub, MOE_SUB, 128).astype(jnp.int32)
    rank = jnp.cumsum(routed, axis=1) - routed
    rank = jnp.where(routed > 0, rank, -1)
    count = jnp.sum(routed, axis=1)[:, :MOE_EXPERTS]
    nblk = ((count + MOE_BLK - 1) // MOE_BLK).reshape(n_sub // MOE_NSUB, MOE_NSUB, MOE_EXPERTS)
    ends = jnp.swapaxes(jnp.cumsum(nblk, axis=1), 1, 2)
    starts = ends - jnp.swapaxes(nblk, 1, 2)
    ntot = ends[:, :, -1].reshape(-1)
    blocks_max = MOE_NSUB * pl.cdiv(MOE_SUB, MOE_BLK)
    bidx = jnp.arange(blocks_max)[None, None, :, None]
    bsub = jnp.sum((bidx >= ends[:, :, None, :]).astype(jnp.int32), axis=-1)
    bsub = jnp.minimum(bsub, MOE_NSUB - 1)
    sub_onehot = (bsub[..., None] == jnp.arange(MOE_NSUB)).astype(jnp.int32)
    bloc = bidx[..., 0] - jnp.sum(sub_onehot * starts[:, :, None, :], axis=-1)
    shape2 = (-1, blocks_max)
    return (rank.reshape(m, 128), jnp.swapaxes(rank, 1, 2), ntot.astype(jnp.int32),
            bsub.reshape(shape2).astype(jnp.int32), bloc.reshape(shape2).astype(jnp.int32))


def _ple_kernel(x_ref, wg_ref, bg_ref, p_ref, wp_ref, o_ref):
    gate = _sigmoid(_dot(x_ref[...], wg_ref[...]) + bg_ref[...])
    o_ref[...] = gate * _dot(p_ref[...].astype(BF16), wp_ref[...])


def _ple(x1b, wg, bg, p, wp, layer, tm=1024, tn=1024):
    m = x1b.shape[0]
    return pl.pallas_call(
        _ple_kernel,
        grid=(m // tm, D_MODEL // tn),
        in_specs=[pl.BlockSpec((tm, D_MODEL), lambda i, j: (i, 0)),
                  pl.BlockSpec((None, D_MODEL, tn), lambda i, j: (layer, 0, j)),
                  pl.BlockSpec((1, tn), lambda i, j: (0, j)),
                  pl.BlockSpec((None, tm, PLE_DIM), lambda i, j: (layer, i, 0)),
                  pl.BlockSpec((None, PLE_DIM, tn), lambda i, j: (layer, 0, j))],
        out_specs=pl.BlockSpec((tm, tn), lambda i, j: (i, j)),
        out_shape=jax.ShapeDtypeStruct((m, D_MODEL), F32),
        compiler_params=_params("parallel", "parallel"),
        name="ple",
    )(x1b, wg, bg, p, wp)


def _final_ln_kernel(x1_ref, ff_ref, ple_ref, g_ref, b_ref, o32_ref, o16_ref):
    h = DEEPNORM_ALPHA * x1_ref[...] + ff_ref[...] + ple_ref[...]
    y = _layer_norm(h, g_ref[...], b_ref[...], LN_EPS)
    o32_ref[...] = y
    o16_ref[...] = y.astype(BF16)


def _final_ln(x1, ff, ple, g, b, tm=512):
    m = x1.shape[0]
    row = pl.BlockSpec((tm, D_MODEL), lambda i: (i, 0))
    vec = pl.BlockSpec((1, D_MODEL), lambda i: (0, 0))
    return pl.pallas_call(
        _final_ln_kernel,
        grid=(m // tm,),
        in_specs=[row, row, row, vec, vec],
        out_specs=[row, row],
        out_shape=[jax.ShapeDtypeStruct((m, D_MODEL), F32), jax.ShapeDtypeStruct((m, D_MODEL), BF16)],
        compiler_params=_params("parallel"),
        name="final_ln",
    )(x1, ff, ple, g, b)


def _split_w_in(w_in):
    w_t = jnp.swapaxes(w_in, 1, 2)
    n_l, _, d = w_t.shape
    lora = w_t[:, COL_LORA:COL_FX]
    f = w_t[:, COL_F:COL_GATE]
    pad = jnp.zeros((n_l, MISC_W - lora.shape[1] - f.shape[1], d), w_t.dtype)
    misc = jnp.concatenate([lora, f, pad], axis=1)
    return (w_t[:, :COL_LORA].astype(BF16), misc.astype(BF16),
            w_t[:, COL_FX:COL_F].astype(BF16), w_t[:, COL_GATE:].astype(BF16))


def _pad_rows(w, lo, total):
    n_l, r, n = w.shape
    return jnp.concatenate([jnp.zeros((n_l, lo, n), w.dtype), w,
                            jnp.zeros((n_l, total - lo - r, n), w.dtype)], axis=1)


def kernel(x, p, w_in, gate_b, gm_ln_g, gm_ln_b, gm_ws, gm_bs, rw_mu, rw_w0, rw_wb, rw_a0, rw_ab, rw_gb, rw_kk, rw_ka, rw_rk, rw_lnx_g, rw_lnx_b, fx_fb, proj_a, proj_b, proj_c, w_out, ln1_g, ln1_b, ffn_w1, ffn_w3, ffn_w2, moe_router, moe_router_b, moe_w1, moe_w3, moe_w2, ple_gate_w, ple_gate_b, ple_proj, ln2_g, ln2_b):
    bsz, seq, d = x.shape
    m = bsz * seq
    n_l = w_in.shape[0]

    w_ar, w_misc, w_fx, w_gate = _split_w_in(w_in)
    mu_r = rw_mu[:, None, 0:RW_WIDTH]
    mu_k = rw_mu[:, None, RW_WIDTH:2 * RW_WIDTH]
    mu_v = rw_mu[:, None, 2 * RW_WIDTH:3 * RW_WIDTH]
    o = 3 * RW_WIDTH
    zl = lambda n: jnp.zeros((n_l, n), F32)
    n_lora = RW_LORA_W + RW_LORA_A + RW_LORA_G
    mu_m = jnp.concatenate([rw_mu[:, o:o + n_lora], zl(MISC_W - n_lora)], axis=-1)[:, None, :]
    wb_pad = _pad_rows(rw_wb, 0, 128).astype(BF16)
    ab_pad = _pad_rows(rw_ab, RW_LORA_W, 128).astype(BF16)
    gb_pad = _pad_rows(rw_gb, 0, 256).astype(BF16)
    fb_pad = jnp.concatenate([zl(FX_F_LOCAL), fx_fb, zl(128 - FX_F_LOCAL - FX_HEADS)], axis=-1)[:, None, :]
    head3 = lambda t: t.reshape(n_l, RW_HEADS, 1, RW_HEAD_DIM)
    gm_bs3 = gm_bs[..., None]
    router_w = jnp.concatenate([moe_router, jnp.zeros(moe_router.shape[:2] + (128 - MOE_EXPERTS,), F32)], axis=-1)
    router_b = jnp.concatenate([moe_router_b, jnp.zeros((moe_router_b.shape[0], 128 - MOE_EXPERTS), F32)],
                               axis=-1)[:, None, :]
    bf = lambda t: t.astype(BF16)
    proj_a16, proj_b16, proj_c16, w_out16 = bf(proj_a), bf(proj_b), bf(proj_c), bf(w_out)
    ffn_w1_16, ffn_w3_16, ffn_w2_16 = bf(ffn_w1), bf(ffn_w3), bf(ffn_w2)
    moe_w1_16, moe_w3_16, moe_w2_16 = bf(moe_w1), bf(moe_w3), bf(moe_w2)
    ple_gw16, ple_pw16 = bf(ple_gate_w), bf(ple_proj)
    ones_comb = jnp.ones((1, m, 1), F32)
    p3 = p.reshape(n_l, m, PLE_DIM)

    x32 = x.reshape(m, d)
    x16 = x32.astype(BF16)
    for i in range(n_l):
        z = _matmul(x16, w_ar, i, F32, 1024, 1024, "in_proj_ar")
        zm = _matmul(x16, w_misc, i, F32, 1024, MISC_W, "in_proj_misc")
        zfx = _matmul(x16, w_fx, i, BF16, 1024, 3 * FX_WIDTH, "in_proj_fx")
        gates = _matmul(x16, w_gate, i, BF16, 1024, 1024, "in_proj_gate", gate_bias=gate_b[i][None])
        z3 = z.reshape(bsz, seq, z.shape[-1])
        zm3 = zm.reshape(bsz, seq, MISC_W)
        y_a = _gmlp(z, gm_ln_g[i][None], gm_ln_b[i][None], gm_ws[i], gm_bs3[i])
        y_b = _rwkv(z3, zm3, mu_r[i], mu_k[i], mu_v[i], mu_m[i], rw_w0[i][None], wb_pad[i], rw_a0[i][None],
                    ab_pad[i], gb_pad[i], rw_kk[i][None], rw_ka[i][None], head3(rw_rk)[i],
                    head3(rw_lnx_g)[i], head3(rw_lnx_b)[i]).reshape(m, RW_WIDTH)
        cum_col, cum_row = _fox_cum(zm3, fb_pad[i])
        y_c = _fox(zfx.reshape(bsz, seq, 3 * FX_WIDTH), cum_col, cum_row).reshape(m, FX_WIDTH)
        merged = _merge(y_a, y_b, y_c, proj_a16, proj_b16, proj_c16, i, gates)
        x1, x1b = _out_ln(merged, w_out16, i, x32, ln1_g[i][None], ln1_b[i][None])
        j = i // 2
        if i % 2 == 0:
            ff = _ffn(x1b, ones_comb, ffn_w1_16, ffn_w3_16, ffn_w2_16, j, 1)
        else:
            comb = _router(x1, router_w[j], router_b[j])
            ff = _moe(x1b, comb, _moe_plan(comb), moe_w1_16, moe_w3_16, moe_w2_16, j)
        ple = _ple(x1b, ple_gw16, ple_gate_b[i][None], p3, ple_pw16, i)
        x32, x16 = _final_ln(x1, ff, ple, ln2_g[i][None], ln2_b[i][None])
    return x32.reshape(bsz, seq, d)
```

```python
import functools
import math

import jax
import jax.numpy as jnp
from jax import lax
from jax.experimental import pallas as pl
from jax.experimental.pallas import tpu as pltpu

F32 = jnp.float32
BF16 = jnp.bfloat16

D_MODEL = 2048
DEPTH = 4
PLE_DIM = 256
GM_CHUNK = 128
GM_GROUPS = 4
GM_WIDTH = 512
RW_HEADS = 16
RW_HEAD_DIM = 64
RW_WIDTH = 1024
RW_LORA_W = 64
RW_LORA_A = 64
RW_LORA_G = 160
RW_GN_EPS = 64e-5
FX_HEADS = 8
FX_HEAD_DIM = 64
FX_WIDTH = 512
MOE_EXPERTS = 8
DEEPNORM_ALPHA = (2.0 * DEPTH) ** 0.25
LN_EPS = 1e-5

COL_A = 0
COL_R = 1024
COL_LORA = 4096
COL_FX = 4384
COL_F = 5920
COL_GATE = 5928
MISC_W = 384
MISC_F_LANE = COL_FX - COL_LORA
FX_F_LOCAL = MISC_F_LANE % 128

RW_CHUNK = 64
RW_QUAD = 2
RW_TB = 256
LOGW_SCALE = math.exp(-0.5)

VMEM_LIMIT = 56 * 1024 * 1024

NT_DIMS = (((1,), (1,)), ((), ()))
TN_DIMS = (((0,), (0,)), ((), ()))


def _params(*sem):
    return pltpu.CompilerParams(dimension_semantics=sem, vmem_limit_bytes=VMEM_LIMIT)


def _dot(a, b):
    return jnp.dot(a, b, preferred_element_type=F32)


def _dg(a, b, dims):
    return lax.dot_general(a, b, dims, preferred_element_type=F32)


def _sigmoid(x):
    return 1.0 / (1.0 + jnp.exp(-x))


def _layer_norm(h, g, b, eps):
    mu = jnp.mean(h, axis=-1, keepdims=True)
    d = h - mu
    var = jnp.mean(d * d, axis=-1, keepdims=True)
    return d * lax.rsqrt(var + eps) * g + b


def _mm_kernel(x_ref, w_ref, o_ref):
    o_ref[...] = _dg(x_ref[...], w_ref[...], NT_DIMS).astype(o_ref.dtype)


def _mm_gate_kernel(x_ref, w_ref, b_ref, o_ref):
    o_ref[...] = _sigmoid(_dg(x_ref[...], w_ref[...], NT_DIMS) + b_ref[...]).astype(o_ref.dtype)


def _matmul(x, w_t, layer, out_dtype, tm, tn, name, gate_bias=None):
    m, k = x.shape
    n = w_t.shape[1]
    in_specs = [pl.BlockSpec((tm, k), lambda i, j: (i, 0)),
                pl.BlockSpec((None, tn, k), lambda i, j: (layer, j, 0))]
    args = (x, w_t)
    body = _mm_kernel
    if gate_bias is not None:
        in_specs.append(pl.BlockSpec((1, tn), lambda i, j: (0, j)))
        args = (x, w_t, gate_bias)
        body = _mm_gate_kernel
    return pl.pallas_call(
        body,
        grid=(m // tm, n // tn),
        in_specs=in_specs,
        out_specs=pl.BlockSpec((tm, tn), lambda i, j: (i, j)),
        out_shape=jax.ShapeDtypeStruct((m, n), out_dtype),
        compiler_params=_params("parallel", "parallel"),
        name=name,
    )(*args)


def _gmlp_kernel(z_ref, g_ref, b_ref, ws_ref, bs_ref, o_ref, *, rows):
    z = z_ref[...]
    za = 0.5 * z * (1.0 + jnp.tanh(math.sqrt(2.0 / math.pi) * (z + 0.044715 * (z * z * z))))
    u = za[:, :GM_WIDTH]
    v = _layer_norm(za[:, GM_WIDTH:], g_ref[...], b_ref[...], LN_EPS).astype(BF16)
    t_idx = lax.broadcasted_iota(jnp.int32, (GM_CHUNK, GM_CHUNK), 0)
    s_idx = lax.broadcasted_iota(jnp.int32, (GM_CHUNK, GM_CHUNK), 1)
    causal = t_idx >= s_idx
    for g in range(GM_GROUPS):
        w = jnp.where(causal, ws_ref[g], 0.0).astype(BF16)
        cs = slice(g * GM_CHUNK, (g + 1) * GM_CHUNK)
        for c in range(rows // GM_CHUNK):
            rs = slice(c * GM_CHUNK, (c + 1) * GM_CHUNK)
            s = _dot(w, v[rs, cs]) + bs_ref[g]
            o_ref[rs, cs] = (u[rs, cs] * s).astype(o_ref.dtype)


def _gmlp(z, ln_g, ln_b, ws, bs, rows=512):
    m = z.shape[0]
    return pl.pallas_call(
        functools.partial(_gmlp_kernel, rows=rows),
        grid=(m // rows,),
        in_specs=[pl.BlockSpec((rows, 2 * GM_WIDTH), lambda i: (i, COL_A // (2 * GM_WIDTH))),
                  pl.BlockSpec((1, GM_WIDTH), lambda i: (0, 0)),
                  pl.BlockSpec((1, GM_WIDTH), lambda i: (0, 0)),
                  pl.BlockSpec((GM_GROUPS, GM_CHUNK, GM_CHUNK), lambda i: (0, 0, 0)),
                  pl.BlockSpec((GM_GROUPS, GM_CHUNK, 1), lambda i: (0, 0, 0))],
        out_specs=pl.BlockSpec((rows, GM_WIDTH), lambda i: (i, 0)),
        out_shape=jax.ShapeDtypeStruct((m, GM_WIDTH), BF16),
        compiler_params=_params("parallel"),
        name="gmlp",
    )(z, ln_g, ln_b, ws, bs)


def _token_shift(z_ref, p_ref, mu_ref, first):
    h = z_ref[0]
    prev_row = jnp.where(first, 0.0, p_ref[0, 7:8, :])
    hp = pltpu.roll(h, 1, axis=0)
    row = lax.broadcasted_iota(jnp.int32, h.shape, 0)
    hp = jnp.where(row == 0, prev_row, hp)
    return h + (hp - h) * mu_ref[...]


def _stage(dst_ref, val, n_chunks):
    for c in range(n_chunks):
        for h in range(RW_HEADS):
            dst_ref[c, h * RW_CHUNK:(h + 1) * RW_CHUNK, 0:RW_HEAD_DIM] = (
                val[c * RW_CHUNK:(c + 1) * RW_CHUNK, h * RW_HEAD_DIM:(h + 1) * RW_HEAD_DIM])


def _rwkv_kernel(zr_ref, zk_ref, zv_ref, zm_ref, pr_ref, pk_ref, pv_ref, pm_ref,
                 mur_ref, muk_ref, muv_ref, mum_ref, w0_ref, wb_ref, a0_ref, ab_ref, gb_ref,
                 kk_ref, ka_ref, rk_ref, lng_ref, lnb_ref, o_ref,
                 r_s, k_s, v_s, kk_s, a_s, lw_s, g_s, y_s, st_s, *, n_chunks):
    first = pl.program_id(1) == 0
    qrows = RW_QUAD * RW_CHUNK
    nd = RW_HEAD_DIM

    @pl.when(first)
    def _():
        st_s[...] = jnp.zeros_like(st_s)
        v_s[...] = jnp.zeros_like(v_s)

    m = _token_shift(zm_ref, pm_ref, mum_ref, first)
    m01 = m[:, 0:128]
    lw_lin = _dot(jnp.tanh(m01).astype(BF16), wb_ref[...]) + w0_ref[...]
    _stage(lw_s, -LOGW_SCALE * _sigmoid(lw_lin), n_chunks)
    a = _sigmoid(a0_ref[...] + _dot(m01.astype(BF16), ab_ref[...]))
    _stage(a_s, a, n_chunks)
    _stage(g_s, _dot(_sigmoid(m[:, 128:384]).astype(BF16), gb_ref[...]), n_chunks)
    k = _token_shift(zk_ref, pk_ref, muk_ref, first)
    _stage(kk_s, k * kk_ref[...], n_chunks)
    _stage(k_s, k * (1.0 + (a - 1.0) * ka_ref[...]), n_chunks)
    _stage(r_s, _token_shift(zr_ref, pr_ref, mur_ref, first), n_chunks)
    _stage(v_s, _token_shift(zv_ref, pv_ref, muv_ref, first), n_chunks)

    ri = lax.broadcasted_iota(jnp.int32, (qrows, qrows), 0)
    ci = lax.broadcasted_iota(jnp.int32, (qrows, qrows), 1)
    shift = RW_CHUNK.bit_length() - 1
    same = (ri >> shift) == (ci >> shift)
    strict = same & (ci < ri)
    incl = same & (ci <= ri)
    tri_bd = jnp.where(incl, 1.0, 0.0).astype(BF16)
    ones_sel = jnp.where(
        (lax.broadcasted_iota(jnp.int32, (qrows, RW_QUAD * 128), 0) >> shift)
        == (lax.broadcasted_iota(jnp.int32, (qrows, RW_QUAD * 128), 1) >> 7), 1.0, 0.0).astype(BF16)
    e2 = jnp.where(lax.broadcasted_iota(jnp.int32, (nd, 2 * nd), 1)
                   == lax.broadcasted_iota(jnp.int32, (nd, 2 * nd), 0) + nd, 1.0, 0.0).astype(BF16)

    def chunk_body(c, carry):
        quads = range(RW_HEADS // RW_QUAD)
        rows = [slice(q * qrows, (q + 1) * qrows) for q in quads]
        r = [r_s[c, rows[q], :] for q in quads]
        kq = [k_s[c, rows[q], :] for q in quads]
        vpad = [v_s[c, rows[q], :] for q in quads]
        vpb = [vpad[q].astype(BF16) for q in quads]
        kkr = [kk_s[c, rows[q], :] for q in quads]
        lw = [lw_s[c, rows[q], :] for q in quads]
        hi = [lw[q].astype(BF16) for q in quads]
        lo = [(lw[q] - hi[q].astype(F32)).astype(BF16) for q in quads]
        cum = [_dot(tri_bd, hi[q]) + _dot(tri_bd, lo[q]) for q in quads]
        tot = [jnp.concatenate([jnp.broadcast_to(cum[q][(hh + 1) * RW_CHUNK - 1:(hh + 1) * RW_CHUNK, :], (RW_CHUNK, nd))
                                for hh in range(RW_QUAD)], axis=0) for q in quads]
        pc_t = [jnp.exp(_dg(hi[q], ones_sel, TN_DIMS) + _dg(lo[q], ones_sel, TN_DIMS)) for q in quads]
        nrm = [jnp.sqrt(jnp.sum(kkr[q] * kkr[q], axis=-1, keepdims=True)) for q in quads]
        kk = [kkr[q] / jnp.maximum(nrm[q], 1e-12) for q in quads]
        bt0 = [kk[q] * a_s[c, rows[q], :] for q in quads]
        cum_ex = [cum[q] - (hi[q].astype(F32) + lo[q].astype(F32)) for q in quads]
        e_neg = [jnp.exp(-cum[q]) for q in quads]
        alb = [(kk[q] * jnp.exp(cum_ex[q])).astype(BF16) for q in quads]
        rbb = [(r[q] * jnp.exp(cum[q])).astype(BF16) for q in quads]
        kh = [(kq[q] * e_neg[q]).astype(BF16) for q in quads]
        bh = [(bt0[q] * e_neg[q]).astype(BF16) for q in quads]
        e_tail = [jnp.exp(tot[q] - cum[q]) for q in quads]
        kt = [(kq[q] * e_tail[q]).astype(BF16) for q in quads]
        btl = [(bt0[q] * e_tail[q]).astype(BF16) for q in quads]
        lmat = [jnp.where(strict, _dg(alb[q], bh[q], NT_DIMS), 0.0).astype(BF16) for q in quads]
        a_ak = [jnp.where(strict, _dg(alb[q], kh[q], NT_DIMS), 0.0).astype(BF16) for q in quads]
        a_rb = [jnp.where(incl, _dg(rbb[q], bh[q], NT_DIMS), 0.0).astype(BF16) for q in quads]
        a_rk = [jnp.where(incl, _dg(rbb[q], kh[q], NT_DIMS), 0.0).astype(BF16) for q in quads]
        x = [_dot(a_ak[q], vpb[q]) + _dot(alb[q], e2) for q in quads]
        x = [x[q] - _dot(lmat[q], x[q].astype(BF16)) for q in quads]
        pw = lmat
        step = 2
        while step < RW_CHUNK:
            pw = [_dot(pw[q], pw[q]).astype(BF16) for q in quads]
            x = [x[q] + _dot(pw[q], x[q].astype(BF16)) for q in quads]
            step *= 2
        xb = [x[q].astype(BF16) for q in quads]
        yr = [_dot(a_rk[q], vpb[q]) - _dot(a_rb[q], xb[q]) + _dot(rbb[q], e2) for q in quads]
        yrb = [yr[q].astype(BF16) for q in quads]

        heads = [(q, hh) for q in quads for hh in range(RW_QUAD)]
        hsl = [slice(hh * RW_CHUNK, (hh + 1) * RW_CHUNK) for hh in range(RW_QUAD)]
        h_aug = [st_s[q * RW_QUAD + hh] for q, hh in heads]
        hb = [h.astype(BF16) for h in h_aug]
        ys = [_dot(yrb[q][hsl[hh]], hb[n]) + yr[q][hsl[hh], 0:nd] for n, (q, hh) in enumerate(heads)]
        us = [_dot(xb[q][hsl[hh]], hb[n]) + x[q][hsl[hh], 0:nd] for n, (q, hh) in enumerate(heads)]
        kv = [_dg(kt[q][hsl[hh]], vpb[q][hsl[hh], 0:nd], TN_DIMS) for q, hh in heads]
        bu = [_dg(btl[q][hsl[hh]], us[n].astype(BF16), TN_DIMS) for n, (q, hh) in enumerate(heads)]
        for n, (q, hh) in enumerate(heads):
            st_s[n, nd:2 * nd, :] = pc_t[q][:, hh * 128:hh * 128 + nd] * h_aug[n][nd:2 * nd, :] + kv[n] - bu[n]
        for n, (q, hh) in enumerate(heads):
            y = ys[n]
            mu = jnp.mean(y, axis=-1, keepdims=True)
            d = y - mu
            var = jnp.mean(d * d, axis=-1, keepdims=True)
            yn = d * lax.rsqrt(var + RW_GN_EPS) * lng_ref[n] + lnb_ref[n]
            hs = hsl[hh]
            bonus = jnp.sum(r[q][hs] * kq[q][hs] * rk_ref[n], axis=-1, keepdims=True) * vpad[q][hs, 0:nd]
            orow = slice(n * RW_CHUNK, (n + 1) * RW_CHUNK)
            y_s[c, orow, :] = (yn + bonus) * g_s[c, orow, :]
        return carry

    lax.fori_loop(0, n_chunks, chunk_body, 0)

    for c in range(n_chunks):
        for h in range(RW_HEADS):
            o_ref[0, c * RW_CHUNK:(c + 1) * RW_CHUNK, h * RW_HEAD_DIM:(h + 1) * RW_HEAD_DIM] = (
                y_s[c, h * RW_CHUNK:(h + 1) * RW_CHUNK, :].astype(o_ref.dtype))


def _rwkv(z3, zm3, mu_r, mu_k, mu_v, mu_m, w0, wb, a0, ab, gb, kk_p, ka_p, rk_p, lng, lnb):
    bsz, seq, _ = z3.shape
    tb = RW_TB
    n_chunks = tb // RW_CHUNK
    cb = COL_R // RW_WIDTH

    def zspec(col_blk, width):
        return pl.BlockSpec((1, tb, width), lambda b, i: (b, i, col_blk))

    def pspec(col_blk, width):
        return pl.BlockSpec((1, 8, width), lambda b, i: (b, jnp.maximum(i * (tb // 8) - 1, 0), col_blk))

    def full(shape):
        return pl.BlockSpec(shape, lambda b, i: (0,) * len(shape))

    stage = pltpu.VMEM((n_chunks, RW_HEADS * RW_CHUNK, RW_HEAD_DIM), F32)
    stage_wide = pltpu.VMEM((n_chunks, RW_HEADS * RW_CHUNK, 2 * RW_HEAD_DIM), F32)
    return pl.pallas_call(
        functools.partial(_rwkv_kernel, n_chunks=n_chunks),
        grid=(bsz, seq // tb),
        in_specs=[zspec(cb, RW_WIDTH), zspec(cb + 1, RW_WIDTH), zspec(cb + 2, RW_WIDTH),
                  zspec(0, MISC_W),
                  pspec(cb, RW_WIDTH), pspec(cb + 1, RW_WIDTH), pspec(cb + 2, RW_WIDTH),
                  pspec(0, MISC_W),
                  full((1, RW_WIDTH)), full((1, RW_WIDTH)), full((1, RW_WIDTH)), full((1, MISC_W)),
                  full((1, RW_WIDTH)), full((128, RW_WIDTH)), full((1, RW_WIDTH)), full((128, RW_WIDTH)),
                  full((256, RW_WIDTH)),
                  full((1, RW_WIDTH)), full((1, RW_WIDTH)),
                  full((RW_HEADS, 1, RW_HEAD_DIM)), full((RW_HEADS, 1, RW_HEAD_DIM)),
                  full((RW_HEADS, 1, RW_HEAD_DIM))],
        out_specs=pl.BlockSpec((1, tb, RW_WIDTH), lambda b, i: (b, i, 0)),
        out_shape=jax.ShapeDtypeStruct((bsz, seq, RW_WIDTH), BF16),
        scratch_shapes=[stage, stage, stage_wide, stage, stage, stage, stage, stage,
                        pltpu.VMEM((RW_HEADS, 2 * RW_HEAD_DIM, RW_HEAD_DIM), F32)],
        compiler_params=_params("parallel", "arbitrary"),
        name="rwkv7",
    )(z3, z3, z3, zm3, z3, z3, z3, zm3, mu_r, mu_k, mu_v, mu_m, w0, wb, a0, ab, gb,
      kk_p, ka_p, rk_p, lng, lnb)


FX_CUM_BLOCK = 256
FX_HEAD_GROUP = 4


def _fox_cum_kernel(zf_ref, fb_ref, col_ref, *, seq):
    n = FX_CUM_BLOCK
    ri = lax.broadcasted_iota(jnp.int32, (n, n), 0)
    ci = lax.broadcasted_iota(jnp.int32, (n, n), 1)
    tri = jnp.where(ci <= ri, 1.0, 0.0).astype(BF16)
    carry = jnp.zeros((1, 128), F32)
    for j in range(seq // n):
        rs = slice(j * n, (j + 1) * n)
        x = zf_ref[0, rs, :] + fb_ref[...]
        lf = jnp.minimum(x, 0.0) - jnp.log(1.0 + jnp.exp(-jnp.abs(x)))
        hi = lf.astype(BF16)
        r1 = lf - hi.astype(F32)
        mid = r1.astype(BF16)
        lo = (r1 - mid.astype(F32)).astype(BF16)
        c = _dot(tri, hi) + _dot(tri, mid) + _dot(tri, lo) + carry
        col_ref[0, rs, :] = c
        carry = c[n - 1:n, :]


def _fox_cum(zm3, fb_pad):
    bsz, seq, _ = zm3.shape
    return pl.pallas_call(
        functools.partial(_fox_cum_kernel, seq=seq),
        grid=(bsz,),
        in_specs=[pl.BlockSpec((1, seq, 128), lambda b: (b, 0, MISC_F_LANE // 128)),
                  pl.BlockSpec((1, 128), lambda b: (0, 0))],
        out_specs=pl.BlockSpec((1, seq, 128), lambda b: (b, 0, 0)),
        out_shape=jax.ShapeDtypeStruct((bsz, seq, 128), F32),
        compiler_params=_params("parallel"),
        name="fox_cum",
    )(zm3, fb_pad)


FX_AUG = 128
FX_BIAS_LANE = 64


def _split3(x):
    hi = x.astype(BF16)
    r1 = x - hi.astype(F32)
    mid = r1.astype(BF16)
    lo = (r1 - mid.astype(F32)).astype(BF16)
    return jnp.concatenate([hi, mid, lo], axis=1)


def _foxt_kernel(q_ref, k_ref, v_ref, cc_ref, eall_ref, selq_ref, selk_ref, eye_ref, o_ref,
                 kaug_s, vt_s, m_s, l_s, acc_s, *, tq, tk, seq):
    i = pl.program_id(1)
    scale = FX_HEAD_DIM ** -0.5
    lane = lax.broadcasted_iota(jnp.int32, (1, FX_HEADS * FX_AUG), 1) % FX_AUG
    const_k = jnp.where((lane >= FX_BIAS_LANE) & (lane < FX_BIAS_LANE + 3), 1.0, 0.0)
    const_q = jnp.where((lane >= FX_BIAS_LANE + 3) & (lane < FX_BIAS_LANE + 6), -1.0, 0.0)
    hs_all = range(FX_HEADS)

    @pl.when(i == 0)
    def _():
        for j in range(seq // tk):
            rs = slice(j * tk, (j + 1) * tk)
            ck3 = _split3(cc_ref[0, rs, :])
            kaug_s[j] = (_dot(k_ref[0, rs, :], eall_ref[...]) + _dot(ck3, selk_ref[...]) + const_k).astype(BF16)
            vt_s[j] = _dg(eye_ref[...], v_ref[0, rs, :], NT_DIMS).astype(BF16)

    q0 = pl.multiple_of(i * tq, tq)
    cq3 = _split3(cc_ref[0, pl.ds(q0, tq), :])
    qaug = (_dot(q_ref[0] * scale, eall_ref[...]) + _dot(cq3, selq_ref[...]) + const_q).astype(BF16)
    qs = [qaug[:, h * FX_AUG:(h + 1) * FX_AUG] for h in hs_all]
    m_s[...] = jnp.full(m_s.shape, -1e30, F32)
    l_s[...] = jnp.zeros(l_s.shape, F32)
    acc_s[...] = jnp.zeros(acc_s.shape, F32)
    n_kv = (i * tq + tq + tk - 1) // tk
    key_idx = lax.broadcasted_iota(jnp.int32, (tk, tq), 0)
    qry_idx = i * tq + lax.broadcasted_iota(jnp.int32, (tk, tq), 1)

    def block(j, masked):
        kblk = kaug_s[j]
        vblk = vt_s[j]
        for g0 in range(0, FX_HEADS, FX_HEAD_GROUP):
            hs = range(g0, g0 + FX_HEAD_GROUP)
            st = {h: _dg(kblk[:, h * FX_AUG:(h + 1) * FX_AUG], qs[h], NT_DIMS) for h in hs}
            if masked:
                keep = key_idx + j * tk <= qry_idx
                st = {h: jnp.where(keep, st[h], -1e30) for h in hs}
            m_prev = {h: m_s[h:h + 1, :] for h in hs}
            m_new = {h: jnp.maximum(m_prev[h], jnp.max(st[h], axis=0, keepdims=True)) for h in hs}
            pt = {h: jnp.exp(st[h] - m_new[h]) for h in hs}
            alphas = {h: jnp.exp(m_prev[h] - m_new[h]) for h in hs}
            pv = {h: _dot(vblk[h * FX_HEAD_DIM:(h + 1) * FX_HEAD_DIM, :], pt[h].astype(BF16)) for h in hs}
            for h in hs:
                l_s[h:h + 1, :] = alphas[h] * l_s[h:h + 1, :] + jnp.sum(pt[h], axis=0, keepdims=True)
                acc_s[h] = alphas[h] * acc_s[h] + pv[h]
                m_s[h:h + 1, :] = m_new[h]

    def visible_body(j, carry):
        block(j, masked=False)
        return carry

    lax.fori_loop(0, n_kv - 1, visible_body, 0)
    block(n_kv - 1, masked=True)
    out_t = jnp.concatenate([acc_s[h] / l_s[h:h + 1, :] for h in hs_all], axis=0)
    o_ref[0] = out_t.T.astype(o_ref.dtype)


def _fox_tables():
    col = jnp.arange(FX_HEADS * FX_AUG)[None, :]
    head, d = col // FX_AUG, col % FX_AUG
    c = jnp.arange(FX_WIDTH)[:, None]
    e_all = ((c == head * FX_HEAD_DIM + d) & (d < FX_HEAD_DIM)).astype(BF16)
    r = jnp.arange(3 * 128)[:, None]
    piece, src_lane = r // 128, r % 128
    from_head = src_lane == FX_F_LOCAL + head
    selq = (from_head & (d == FX_BIAS_LANE + piece)).astype(BF16)
    selk = (from_head & (d == FX_BIAS_LANE + 3 + piece)).astype(BF16)
    return e_all, selq, selk, jnp.eye(FX_WIDTH, dtype=BF16)


def _foxt(zfx3, cum_col, tables, tq=256, tk=512):
    bsz, seq, _ = zfx3.shape
    assert tk % tq == 0 and seq % tk == 0

    def full(a):
        return pl.BlockSpec(a.shape, lambda b, i: (0,) * a.ndim)

    return pl.pallas_call(
        functools.partial(_foxt_kernel, tq=tq, tk=tk, seq=seq),
        grid=(bsz, seq // tq),
        in_specs=[pl.BlockSpec((1, tq, FX_WIDTH), lambda b, i: (b, i, 0)),
                  pl.BlockSpec((1, seq, FX_WIDTH), lambda b, i: (b, 0, 1)),
                  pl.BlockSpec((1, seq, FX_WIDTH), lambda b, i: (b, 0, 2)),
                  pl.BlockSpec((1, seq, 128), lambda b, i: (b, 0, 0))] + [full(t) for t in tables],
        out_specs=pl.BlockSpec((1, tq, FX_WIDTH), lambda b, i: (b, i, 0)),
        out_shape=jax.ShapeDtypeStruct((bsz, seq, FX_WIDTH), BF16),
        scratch_shapes=[pltpu.VMEM((seq // tk, tk, FX_HEADS * FX_AUG), BF16),
                        pltpu.VMEM((seq // tk, FX_WIDTH, tk), BF16),
                        pltpu.VMEM((FX_HEADS, tq), F32), pltpu.VMEM((FX_HEADS, tq), F32),
                        pltpu.VMEM((FX_HEADS, FX_HEAD_DIM, tq), F32)],
        compiler_params=_params("parallel", "arbitrary"),
        name="fox_attn",
    )(zfx3, zfx3, zfx3, cum_col, *tables)


def _merge_kernel(ya_ref, yb_ref, yc_ref, pa_ref, pb_ref, pc_ref, g0_ref, g1_ref, g2_ref, o_ref):
    acc = g0_ref[...].astype(F32) * _dot(ya_ref[...], pa_ref[...])
    acc = acc + g1_ref[...].astype(F32) * _dot(yb_ref[...], pb_ref[...])
    acc = acc + g2_ref[...].astype(F32) * _dot(yc_ref[...], pc_ref[...])
    o_ref[...] = acc.astype(o_ref.dtype)


def _merge(ya, yb, yc, pa, pb, pc, layer, gates, tm=1024, tn=512):
    m = ya.shape[0]
    per = D_MODEL // tn

    def yspec(width):
        return pl.BlockSpec((tm, width), lambda i, j: (i, 0))

    def wspec(width):
        return pl.BlockSpec((None, width, tn), lambda i, j: (layer, 0, j))

    def gspec(k):
        return pl.BlockSpec((tm, tn), lambda i, j: (i, k * per + j))

    return pl.pallas_call(
        _merge_kernel,
        grid=(m // tm, D_MODEL // tn),
        in_specs=[yspec(GM_WIDTH), yspec(RW_WIDTH), yspec(FX_WIDTH),
                  wspec(GM_WIDTH), wspec(RW_WIDTH), wspec(FX_WIDTH),
                  gspec(0), gspec(1), gspec(2)],
        out_specs=pl.BlockSpec((tm, tn), lambda i, j: (i, j)),
        out_shape=jax.ShapeDtypeStruct((m, D_MODEL), BF16),
        compiler_params=_params("parallel", "parallel"),
        name="merge",
    )(ya, yb, yc, pa, pb, pc, gates, gates, gates)


def _out_ln_kernel(m_ref, w_ref, x_ref, g_ref, b_ref, o32_ref, o16_ref):
    h = DEEPNORM_ALPHA * x_ref[...] + _dot(m_ref[...], w_ref[...])
    y = _layer_norm(h, g_ref[...], b_ref[...], LN_EPS)
    o32_ref[...] = y
    o16_ref[...] = y.astype(BF16)


def _out_ln(merged, w_out, layer, x, g, b, tm=512):
    m = x.shape[0]
    row = pl.BlockSpec((tm, D_MODEL), lambda i: (i, 0))
    vec = pl.BlockSpec((1, D_MODEL), lambda i: (0, 0))
    return pl.pallas_call(
        _out_ln_kernel,
        grid=(m // tm,),
        in_specs=[row, pl.BlockSpec((None, D_MODEL, D_MODEL), lambda i: (layer, 0, 0)), row, vec, vec],
        out_specs=[row, row],
        out_shape=[jax.ShapeDtypeStruct((m, D_MODEL), F32), jax.ShapeDtypeStruct((m, D_MODEL), BF16)],
        compiler_params=_params("parallel"),
        name="out_ln",
    )(merged, w_out, x, g, b)


def _router_kernel(x_ref, w_ref, b_ref, o_ref):
    logits = jnp.dot(x_ref[...], w_ref[...], preferred_element_type=F32,
                     precision=lax.Precision.HIGHEST) + b_ref[...]
    lane = lax.broadcasted_iota(jnp.int32, logits.shape, 1)
    logits = jnp.where(lane < MOE_EXPERTS, logits, -1e30)
    m1 = jnp.max(logits, axis=-1, keepdims=True)
    i1 = jnp.min(jnp.where(logits == m1, lane, 128), axis=-1, keepdims=True)
    rest = jnp.where(lane == i1, -1e30, logits)
    m2 = jnp.max(rest, axis=-1, keepdims=True)
    i2 = jnp.min(jnp.where(rest == m2, lane, 128), axis=-1, keepdims=True)
    e2 = jnp.exp(m2 - m1)
    w1 = 1.0 / (1.0 + e2)
    o_ref[...] = jnp.where(lane == i1, w1, 0.0) + jnp.where(lane == i2, e2 * w1, 0.0)


def _router(x1, w_pad, b_pad, tm=1024):
    m = x1.shape[0]
    return pl.pallas_call(
        _router_kernel,
        grid=(m // tm,),
        in_specs=[pl.BlockSpec((tm, D_MODEL), lambda i: (i, 0)),
                  pl.BlockSpec((D_MODEL, 128), lambda i: (0, 0)),
                  pl.BlockSpec((1, 128), lambda i: (0, 0))],
        out_specs=pl.BlockSpec((tm, 128), lambda i: (i, 0)),
        out_shape=jax.ShapeDtypeStruct((m, 128), F32),
        compiler_params=_params("parallel"),
        name="router",
    )(x1, w_pad, b_pad)


def _ffn_kernel(x_ref, c_ref, w1_ref, w3_ref, w2_ref, o_ref):
    @pl.when((pl.program_id(1) == 0) & (pl.program_id(2) == 0))
    def _():
        o_ref[...] = jnp.zeros_like(o_ref)

    x = x_ref[...]
    g = _dot(x, w1_ref[0])
    u = _dot(x, w3_ref[0])
    h = (g * _sigmoid(g)) * u * c_ref[0]
    o_ref[...] += _dot(h.astype(BF16), w2_ref[0])


def _ffn(xb, comb, w1, w3, w2, first, n_e, tm=1024, tf=512):
    m = xb.shape[0]
    f = w1.shape[-1]
    return pl.pallas_call(
        _ffn_kernel,
        grid=(m // tm, n_e, f // tf),
        in_specs=[pl.BlockSpec((tm, D_MODEL), lambda i, e, j: (i, 0)),
                  pl.BlockSpec((1, tm, 1), lambda i, e, j: (e, i, 0)),
                  pl.BlockSpec((1, D_MODEL, tf), lambda i, e, j: (first + e, 0, j)),
                  pl.BlockSpec((1, D_MODEL, tf), lambda i, e, j: (first + e, 0, j)),
                  pl.BlockSpec((1, tf, D_MODEL), lambda i, e, j: (first + e, j, 0))],
        out_specs=pl.BlockSpec((tm, D_MODEL), lambda i, e, j: (i, 0)),
        out_shape=jax.ShapeDtypeStruct((m, D_MODEL), F32),
        compiler_params=_params("parallel", "arbitrary", "arbitrary"),
        name="swiglu",
    )(xb, comb, w1, w3, w2)


MOE_BLK = 288
MOE_SUB = 1024
MOE_NSUB = 1
MOE_PASS = 4


def _moe_kernel(ntot_ref, bsub_ref, bloc_ref, x_ref, comb_ref, rank_ref, rankt_ref, w1_ref, w3_ref, w2_ref,
                o_ref, xs_s, acc_s, *, n_f, col_split):
    i, e, g, f = pl.program_id(0), pl.program_id(1), pl.program_id(2), pl.program_id(3)
    row = i * MOE_EXPERTS + e
    nb = jnp.clip(ntot_ref[row] - g * MOE_PASS, 0, MOE_PASS)

    @pl.when((e == 0) & (g == 0) & (f == 0))
    def _():
        o_ref[...] = jnp.zeros_like(o_ref)

    def sub_rows(b):
        sub = bsub_ref[row, g * MOE_PASS + b]
        return sub, pl.ds(pl.multiple_of(sub * MOE_SUB, MOE_SUB), MOE_SUB)

    @pl.when(f == 0)
    def _():
        slot = lax.broadcasted_iota(jnp.int32, (MOE_BLK, MOE_SUB), 0)

        def gather(b, carry):
            sub, rows = sub_rows(b)
            first = bloc_ref[row, g * MOE_PASS + b] * MOE_BLK
            rank_row = rankt_ref[sub, pl.ds(e, 1), :]
            onehot = jnp.where(rank_row == slot + first, 1.0, 0.0).astype(BF16)
            xs_s[b] = _dot(onehot, x_ref[rows, :]).astype(BF16)
            acc_s[b] = jnp.zeros(acc_s.shape[1:], F32)
            return carry

        lax.fori_loop(0, nb, gather, 0)

    def compute(b, carry):
        xb = xs_s[b]
        gate = _dot(xb, w1_ref[0])
        up = _dot(xb, w3_ref[0])
        h = (gate * _sigmoid(gate)) * up
        acc_s[b] += _dot(h.astype(BF16), w2_ref[0])
        return carry

    lax.fori_loop(0, nb, compute, 0)

    @pl.when(f == n_f - 1)
    def _():
        lane = lax.broadcasted_iota(jnp.int32, (MOE_SUB, 128), 1)
        sel = lane == e
        slot = lax.broadcasted_iota(jnp.int32, (MOE_SUB, MOE_BLK), 1)
        cw = D_MODEL // col_split

        def scatter(b, carry):
            _, rows = sub_rows(b)
            first = bloc_ref[row, g * MOE_PASS + b] * MOE_BLK
            c_col = jnp.sum(jnp.where(sel, comb_ref[rows, :], 0.0), axis=-1, keepdims=True)
            r_col = jnp.sum(jnp.where(sel, rank_ref[rows, :], 0), axis=-1, keepdims=True)
            onehot_t = jnp.where(r_col == slot + first, 1.0, 0.0).astype(BF16)
            for s in range(col_split):
                cs = slice(s * cw, (s + 1) * cw)
                o_ref[rows, cs] += _dot(onehot_t, acc_s[b, :, cs].astype(BF16)) * c_col
            return carry

        lax.fori_loop(0, nb, scatter, 0)


def _moe(xb, comb, plan, w1, w3, w2, layer, tf=1024):
    rank, rank_t, ntot, bsub, bloc = plan
    m = xb.shape[0]
    tm = MOE_SUB * MOE_NSUB
    _, n_e, _, f = w1.shape
    n_f = f // tf
    blocks_max = MOE_NSUB * pl.cdiv(MOE_SUB, MOE_BLK)
    n_pass = pl.cdiv(blocks_max, MOE_PASS)
    once = pl.Buffered(1)

    def wmap(sel):
        def index_map(i, e, g, j, ntot_ref, bsub_ref, bloc_ref):
            live = (g == 0) | (ntot_ref[i * MOE_EXPERTS + e] > g * MOE_PASS)
            jj = jnp.where(live, j, n_f - 1)
            return (layer, e, 0, jj) if sel == 0 else (layer, e, jj, 0)
        return index_map

    def tile(i, e, g, j, *_):
        return (i, 0)

    grid_spec = pltpu.PrefetchScalarGridSpec(
        num_scalar_prefetch=3,
        grid=(m // tm, n_e, n_pass, n_f),
        in_specs=[pl.BlockSpec((tm, D_MODEL), tile, pipeline_mode=once),
                  pl.BlockSpec((tm, 128), tile, pipeline_mode=once),
                  pl.BlockSpec((tm, 128), tile, pipeline_mode=once),
                  pl.BlockSpec((MOE_NSUB, 128, MOE_SUB), lambda i, e, g, j, *_: (i, 0, 0), pipeline_mode=once),
                  pl.BlockSpec((None, 1, D_MODEL, tf), wmap(0)),
                  pl.BlockSpec((None, 1, D_MODEL, tf), wmap(0)),
                  pl.BlockSpec((None, 1, tf, D_MODEL), wmap(1))],
        out_specs=pl.BlockSpec((tm, D_MODEL), tile, pipeline_mode=once),
        scratch_shapes=[pltpu.VMEM((MOE_PASS, MOE_BLK, D_MODEL), BF16),
                        pltpu.VMEM((MOE_PASS, MOE_BLK, D_MODEL), F32)],
    )
    return pl.pallas_call(
        functools.partial(_moe_kernel, n_f=n_f, col_split=2),
        grid_spec=grid_spec,
        out_shape=jax.ShapeDtypeStruct((m, D_MODEL), F32),
        compiler_params=_params("parallel", "arbitrary", "arbitrary", "arbitrary"),
        name="moe",
    )(ntot, bsub, bloc, xb, comb, rank, rank_t, w1, w3, w2)


def _moe_plan(comb):
    m = comb.shape[0]
    n_sub = m // MOE_SUB
    routed = (comb > 0.0).reshape(n_sub, MOE_SUB, 128).astype(jnp.int32)
    rank = jnp.cumsum(routed, axis=1) - routed
    rank = jnp.where(routed > 0, rank, -1)
    count = jnp.sum(routed, axis=1)[:, :MOE_EXPERTS]
    nblk = ((count + MOE_BLK - 1) // MOE_BLK).reshape(n_sub // MOE_NSUB, MOE_NSUB, MOE_EXPERTS)
    ends = jnp.swapaxes(jnp.cumsum(nblk, axis=1), 1, 2)
    starts = ends - jnp.swapaxes(nblk, 1, 2)
    ntot = ends[:, :, -1].reshape(-1)
    blocks_max = MOE_NSUB * pl.cdiv(MOE_SUB, MOE_BLK)
    bidx = jnp.arange(blocks_max)[None, None, :, None]
    bsub = jnp.sum((bidx >= ends[:, :, None, :]).astype(jnp.int32), axis=-1)
    bsub = jnp.minimum(bsub, MOE_NSUB - 1)
    sub_onehot = (bsub[..., None] == jnp.arange(MOE_NSUB)).astype(jnp.int32)
    bloc = bidx[..., 0] - jnp.sum(sub_onehot * starts[:, :, None, :], axis=-1)
    shape2 = (-1, blocks_max)
    return (rank.reshape(m, 128), jnp.swapaxes(rank, 1, 2), ntot.astype(jnp.int32),
            bsub.reshape(shape2).astype(jnp.int32), bloc.reshape(shape2).astype(jnp.int32))


def _ple_kernel(x_ref, wg_ref, bg_ref, p_ref, wp_ref, o_ref):
    gate = _sigmoid(_dot(x_ref[...], wg_ref[...]) + bg_ref[...])
    o_ref[...] = gate * _dot(p_ref[...].astype(BF16), wp_ref[...])


def _ple(x1b, wg, bg, p, wp, layer, tm=1024, tn=1024):
    m = x1b.shape[0]
    return pl.pallas_call(
        _ple_kernel,
        grid=(m // tm, D_MODEL // tn),
        in_specs=[pl.BlockSpec((tm, D_MODEL), lambda i, j: (i, 0)),
                  pl.BlockSpec((None, D_MODEL, tn), lambda i, j: (layer, 0, j)),
                  pl.BlockSpec((1, tn), lambda i, j: (0, j)),
                  pl.BlockSpec((None, tm, PLE_DIM), lambda i, j: (layer, i, 0)),
                  pl.BlockSpec((None, PLE_DIM, tn), lambda i, j: (layer, 0, j))],
        out_specs=pl.BlockSpec((tm, tn), lambda i, j: (i, j)),
        out_shape=jax.ShapeDtypeStruct((m, D_MODEL), F32),
        compiler_params=_params("parallel", "parallel"),
        name="ple",
    )(x1b, wg, bg, p, wp)


def _final_ln_kernel(x1_ref, ff_ref, ple_ref, g_ref, b_ref, o32_ref, o16_ref):
    h = DEEPNORM_ALPHA * x1_ref[...] + ff_ref[...] + ple_ref[...]
    y = _layer_norm(h, g_ref[...], b_ref[...], LN_EPS)
    o32_ref[...] = y
    o16_ref[...] = y.astype(BF16)


def _final_ln(x1, ff, ple, g, b, tm=512):
    m = x1.shape[0]
    row = pl.BlockSpec((tm, D_MODEL), lambda i: (i, 0))
    vec = pl.BlockSpec((1, D_MODEL), lambda i: (0, 0))
    return pl.pallas_call(
        _final_ln_kernel,
        grid=(m // tm,),
        in_specs=[row, row, row, vec, vec],
        out_specs=[row, row],
        out_shape=[jax.ShapeDtypeStruct((m, D_MODEL), F32), jax.ShapeDtypeStruct((m, D_MODEL), BF16)],
        compiler_params=_params("parallel"),
        name="final_ln",
    )(x1, ff, ple, g, b)


def _split_w_in(w_in):
    w_t = jnp.swapaxes(w_in, 1, 2)
    n_l, _, d = w_t.shape
    lora = w_t[:, COL_LORA:COL_FX]
    f = w_t[:, COL_F:COL_GATE]
    pad = jnp.zeros((n_l, MISC_W - lora.shape[1] - f.shape[1], d), w_t.dtype)
    misc = jnp.concatenate([lora, f, pad], axis=1)
    return (w_t[:, :COL_LORA].astype(BF16), misc.astype(BF16),
            w_t[:, COL_FX:COL_F].astype(BF16), w_t[:, COL_GATE:].astype(BF16))


def _pad_rows(w, lo, total):
    n_l, r, n = w.shape
    return jnp.concatenate([jnp.zeros((n_l, lo, n), w.dtype), w,
                            jnp.zeros((n_l, total - lo - r, n), w.dtype)], axis=1)


def kernel(x, p, w_in, gate_b, gm_ln_g, gm_ln_b, gm_ws, gm_bs, rw_mu, rw_w0, rw_wb, rw_a0, rw_ab, rw_gb, rw_kk, rw_ka, rw_rk, rw_lnx_g, rw_lnx_b, fx_fb, proj_a, proj_b, proj_c, w_out, ln1_g, ln1_b, ffn_w1, ffn_w3, ffn_w2, moe_router, moe_router_b, moe_w1, moe_w3, moe_w2, ple_gate_w, ple_gate_b, ple_proj, ln2_g, ln2_b):
    bsz, seq, d = x.shape
    m = bsz * seq
    n_l = w_in.shape[0]

    w_ar, w_misc, w_fx, w_gate = _split_w_in(w_in)
    mu_r = rw_mu[:, None, 0:RW_WIDTH]
    mu_k = rw_mu[:, None, RW_WIDTH:2 * RW_WIDTH]
    mu_v = rw_mu[:, None, 2 * RW_WIDTH:3 * RW_WIDTH]
    o = 3 * RW_WIDTH
    zl = lambda n: jnp.zeros((n_l, n), F32)
    n_lora = RW_LORA_W + RW_LORA_A + RW_LORA_G
    mu_m = jnp.concatenate([rw_mu[:, o:o + n_lora], zl(MISC_W - n_lora)], axis=-1)[:, None, :]
    wb_pad = _pad_rows(rw_wb, 0, 128).astype(BF16)
    ab_pad = _pad_rows(rw_ab, RW_LORA_W, 128).astype(BF16)
    gb_pad = _pad_rows(rw_gb, 0, 256).astype(BF16)
    fb_pad = jnp.concatenate([zl(FX_F_LOCAL), fx_fb, zl(128 - FX_F_LOCAL - FX_HEADS)], axis=-1)[:, None, :]
    head3 = lambda t: t.reshape(n_l, RW_HEADS, 1, RW_HEAD_DIM)
    gm_bs3 = gm_bs[..., None]
    router_w = jnp.concatenate([moe_router, jnp.zeros(moe_router.shape[:2] + (128 - MOE_EXPERTS,), F32)], axis=-1)
    router_b = jnp.concatenate([moe_router_b, jnp.zeros((moe_router_b.shape[0], 128 - MOE_EXPERTS), F32)],
                               axis=-1)[:, None, :]
    bf = lambda t: t.astype(BF16)
    proj_a16, proj_b16, proj_c16, w_out16 = bf(proj_a), bf(proj_b), bf(proj_c), bf(w_out)
    ffn_w1_16, ffn_w3_16, ffn_w2_16 = bf(ffn_w1), bf(ffn_w3), bf(ffn_w2)
    moe_w1_16, moe_w3_16, moe_w2_16 = bf(moe_w1), bf(moe_w3), bf(moe_w2)
    ple_gw16, ple_pw16 = bf(ple_gate_w), bf(ple_proj)
    ones_comb = jnp.ones((1, m, 1), F32)
    p3 = p.reshape(n_l, m, PLE_DIM)
    fox_tables = _fox_tables()

    x32 = x.reshape(m, d)
    x16 = x32.astype(BF16)
    for i in range(n_l):
        z = _matmul(x16, w_ar, i, F32, 1024, 1024, "in_proj_ar")
        zm = _matmul(x16, w_misc, i, F32, 1024, MISC_W, "in_proj_misc")
        zfx = _matmul(x16, w_fx, i, BF16, 1024, 3 * FX_WIDTH, "in_proj_fx")
        gates = _matmul(x16, w_gate, i, BF16, 1024, 1024, "in_proj_gate", gate_bias=gate_b[i][None])
        z3 = z.reshape(bsz, seq, z.shape[-1])
        zm3 = zm.reshape(bsz, seq, MISC_W)
        y_a = _gmlp(z, gm_ln_g[i][None], gm_ln_b[i][None], gm_ws[i], gm_bs3[i])
        y_b = _rwkv(z3, zm3, mu_r[i], mu_k[i], mu_v[i], mu_m[i], rw_w0[i][None], wb_pad[i], rw_a0[i][None],
                    ab_pad[i], gb_pad[i], rw_kk[i][None], rw_ka[i][None], head3(rw_rk)[i],
                    head3(rw_lnx_g)[i], head3(rw_lnx_b)[i]).reshape(m, RW_WIDTH)
        cum_col = _fox_cum(zm3, fb_pad[i])
        y_c = _foxt(zfx.reshape(bsz, seq, 3 * FX_WIDTH), cum_col, fox_tables).reshape(m, FX_WIDTH)
        merged = _merge(y_a, y_b, y_c, proj_a16, proj_b16, proj_c16, i, gates)
        x1, x1b = _out_ln(merged, w_out16, i, x32, ln1_g[i][None], ln1_b[i][None])
        j = i // 2
        if i % 2 == 0:
            ff = _ffn(x1b, ones_comb, ffn_w1_16, ffn_w3_16, ffn_w2_16, j, 1)
        else:
            comb = _router(x1, router_w[j], router_b[j])
            ff = _moe(x1b, comb, _moe_plan(comb), moe_w1_16, moe_w3_16, moe_w2_16, j)
        ple = _ple(x1b, ple_gw16, ple_gate_b[i][None], p3, ple_pw16, i)
        x32, x16 = _final_ln(x1, ff, ple, ln2_g[i][None], ln2_b[i][None])
    return x32.reshape(bsz, seq, d)
```

```python
import functools
import math

import jax
import jax.numpy as jnp
from jax import lax
from jax.experimental import pallas as pl
from jax.experimental.pallas import tpu as pltpu

F32 = jnp.float32
BF16 = jnp.bfloat16

D_MODEL = 2048
DEPTH = 4
PLE_DIM = 256
GM_CHUNK = 128
GM_GROUPS = 4
GM_WIDTH = 512
RW_HEADS = 16
RW_HEAD_DIM = 64
RW_WIDTH = 1024
RW_LORA_W = 64
RW_LORA_A = 64
RW_LORA_G = 160
RW_GN_EPS = 64e-5
FX_HEADS = 8
FX_HEAD_DIM = 64
FX_WIDTH = 512
MOE_EXPERTS = 8
DEEPNORM_ALPHA = (2.0 * DEPTH) ** 0.25
LN_EPS = 1e-5

COL_A = 0
COL_R = 1024
COL_LORA = 4096
COL_FX = 4384
COL_F = 5920
COL_GATE = 5928
MISC_W = 384
MISC_F_LANE = COL_FX - COL_LORA
FX_F_LOCAL = MISC_F_LANE % 128

RW_CHUNK = 64
RW_QUAD = 2
RW_TB = 256
LOGW_SCALE = math.exp(-0.5)

VMEM_LIMIT = 56 * 1024 * 1024

NT_DIMS = (((1,), (1,)), ((), ()))
TN_DIMS = (((0,), (0,)), ((), ()))


def _params(*sem):
    return pltpu.CompilerParams(dimension_semantics=sem, vmem_limit_bytes=VMEM_LIMIT)


def _dot(a, b):
    return jnp.dot(a, b, preferred_element_type=F32)


def _dg(a, b, dims):
    return lax.dot_general(a, b, dims, preferred_element_type=F32)


def _sigmoid(x):
    return 1.0 / (1.0 + jnp.exp(-x))


def _layer_norm(h, g, b, eps):
    mu = jnp.mean(h, axis=-1, keepdims=True)
    d = h - mu
    var = jnp.mean(d * d, axis=-1, keepdims=True)
    return d * lax.rsqrt(var + eps) * g + b


def _mm_kernel(x_ref, w_ref, o_ref):
    o_ref[...] = _dg(x_ref[...], w_ref[...], NT_DIMS).astype(o_ref.dtype)


def _matmul(x, w_t, layer, out_dtype, tm, tn, name):
    m, k = x.shape
    n = w_t.shape[1]
    return pl.pallas_call(
        _mm_kernel,
        grid=(m // tm, n // tn),
        in_specs=[pl.BlockSpec((tm, k), lambda i, j: (i, 0)),
                  pl.BlockSpec((None, tn, k), lambda i, j: (layer, j, 0))],
        out_specs=pl.BlockSpec((tm, tn), lambda i, j: (i, j)),
        out_shape=jax.ShapeDtypeStruct((m, n), out_dtype),
        compiler_params=_params("parallel", "parallel"),
        name=name,
    )(x, w_t)


def _gmlp_kernel(z_ref, g_ref, b_ref, ws_ref, bs_ref, o_ref, *, rows):
    z = z_ref[...].astype(F32)
    za = 0.5 * z * (1.0 + jnp.tanh(math.sqrt(2.0 / math.pi) * (z + 0.044715 * (z * z * z))))
    u = za[:, :GM_WIDTH]
    v = _layer_norm(za[:, GM_WIDTH:], g_ref[...], b_ref[...], LN_EPS).astype(BF16)
    t_idx = lax.broadcasted_iota(jnp.int32, (GM_CHUNK, GM_CHUNK), 0)
    s_idx = lax.broadcasted_iota(jnp.int32, (GM_CHUNK, GM_CHUNK), 1)
    causal = t_idx >= s_idx
    for g in range(GM_GROUPS):
        w = jnp.where(causal, ws_ref[g], 0.0).astype(BF16)
        cs = slice(g * GM_CHUNK, (g + 1) * GM_CHUNK)
        for c in range(rows // GM_CHUNK):
            rs = slice(c * GM_CHUNK, (c + 1) * GM_CHUNK)
            s = _dot(w, v[rs, cs]) + bs_ref[g]
            o_ref[rs, cs] = (u[rs, cs] * s).astype(o_ref.dtype)


def _gmlp(z, ln_g, ln_b, ws, bs, rows=512):
    m = z.shape[0]
    return pl.pallas_call(
        functools.partial(_gmlp_kernel, rows=rows),
        grid=(m // rows,),
        in_specs=[pl.BlockSpec((rows, 2 * GM_WIDTH), lambda i: (i, COL_A // (2 * GM_WIDTH))),
                  pl.BlockSpec((1, GM_WIDTH), lambda i: (0, 0)),
                  pl.BlockSpec((1, GM_WIDTH), lambda i: (0, 0)),
                  pl.BlockSpec((GM_GROUPS, GM_CHUNK, GM_CHUNK), lambda i: (0, 0, 0)),
                  pl.BlockSpec((GM_GROUPS, GM_CHUNK, 1), lambda i: (0, 0, 0))],
        out_specs=pl.BlockSpec((rows, GM_WIDTH), lambda i: (i, 0)),
        out_shape=jax.ShapeDtypeStruct((m, GM_WIDTH), BF16),
        compiler_params=_params("parallel"),
        name="gmlp",
    )(z, ln_g, ln_b, ws, bs)


def _token_shift(z_ref, p_ref, mu_ref, first):
    h = z_ref[0].astype(F32)
    last = p_ref.shape[1] - 1
    prev_row = jnp.where(first, 0.0, p_ref[0, last:last + 1, :].astype(F32))
    hp = pltpu.roll(h, 1, axis=0)
    row = lax.broadcasted_iota(jnp.int32, h.shape, 0)
    hp = jnp.where(row == 0, prev_row, hp)
    return h + (hp - h) * mu_ref[...]


def _stage(dst_ref, val, n_chunks):
    for c in range(n_chunks):
        for h in range(RW_HEADS):
            dst_ref[c, h * RW_CHUNK:(h + 1) * RW_CHUNK, 0:RW_HEAD_DIM] = (
                val[c * RW_CHUNK:(c + 1) * RW_CHUNK, h * RW_HEAD_DIM:(h + 1) * RW_HEAD_DIM])


def _rwkv_kernel(zr_ref, zk_ref, zv_ref, zm_ref, pr_ref, pk_ref, pv_ref, pm_ref,
                 mur_ref, muk_ref, muv_ref, mum_ref, w0_ref, wb_ref, a0_ref, ab_ref, gb_ref,
                 kk_ref, ka_ref, rk_ref, lng_ref, lnb_ref, o_ref,
                 r_s, k_s, v_s, kk_s, a_s, lw_s, g_s, y_s, st_s, *, n_chunks):
    first = pl.program_id(1) == 0
    qrows = RW_QUAD * RW_CHUNK
    nd = RW_HEAD_DIM

    @pl.when(first)
    def _():
        st_s[...] = jnp.zeros_like(st_s)
        v_s[...] = jnp.zeros_like(v_s)

    m = _token_shift(zm_ref, pm_ref, mum_ref, first)
    m01 = m[:, 0:128]
    lw_lin = _dot(jnp.tanh(m01).astype(BF16), wb_ref[...]) + w0_ref[...]
    _stage(lw_s, -LOGW_SCALE * _sigmoid(lw_lin), n_chunks)
    a = _sigmoid(a0_ref[...] + _dot(m01.astype(BF16), ab_ref[...]))
    _stage(a_s, a, n_chunks)
    _stage(g_s, _dot(_sigmoid(m[:, 128:384]).astype(BF16), gb_ref[...]), n_chunks)
    k = _token_shift(zk_ref, pk_ref, muk_ref, first)
    _stage(kk_s, k * kk_ref[...], n_chunks)
    _stage(k_s, k * (1.0 + (a - 1.0) * ka_ref[...]), n_chunks)
    _stage(r_s, _token_shift(zr_ref, pr_ref, mur_ref, first), n_chunks)
    _stage(v_s, _token_shift(zv_ref, pv_ref, muv_ref, first), n_chunks)

    ri = lax.broadcasted_iota(jnp.int32, (qrows, qrows), 0)
    ci = lax.broadcasted_iota(jnp.int32, (qrows, qrows), 1)
    shift = RW_CHUNK.bit_length() - 1
    same = (ri >> shift) == (ci >> shift)
    strict = same & (ci < ri)
    incl = same & (ci <= ri)
    tri_bd = jnp.where(incl, 1.0, 0.0).astype(BF16)
    ones_sel = jnp.where(
        (lax.broadcasted_iota(jnp.int32, (qrows, RW_QUAD * 128), 0) >> shift)
        == (lax.broadcasted_iota(jnp.int32, (qrows, RW_QUAD * 128), 1) >> 7), 1.0, 0.0).astype(BF16)
    e2 = jnp.where(lax.broadcasted_iota(jnp.int32, (nd, 2 * nd), 1)
                   == lax.broadcasted_iota(jnp.int32, (nd, 2 * nd), 0) + nd, 1.0, 0.0).astype(BF16)

    def chunk_body(c, carry):
        quads = range(RW_HEADS // RW_QUAD)
        rows = [slice(q * qrows, (q + 1) * qrows) for q in quads]
        r = [r_s[c, rows[q], :] for q in quads]
        kq = [k_s[c, rows[q], :] for q in quads]
        vpad = [v_s[c, rows[q], :] for q in quads]
        vpb = [vpad[q].astype(BF16) for q in quads]
        kkr = [kk_s[c, rows[q], :] for q in quads]
        lw = [lw_s[c, rows[q], :] for q in quads]
        hi = [lw[q].astype(BF16) for q in quads]
        lo = [(lw[q] - hi[q].astype(F32)).astype(BF16) for q in quads]
        cum = [_dot(tri_bd, hi[q]) + _dot(tri_bd, lo[q]) for q in quads]
        tot = [jnp.concatenate([jnp.broadcast_to(cum[q][(hh + 1) * RW_CHUNK - 1:(hh + 1) * RW_CHUNK, :], (RW_CHUNK, nd))
                                for hh in range(RW_QUAD)], axis=0) for q in quads]
        pc_t = [jnp.exp(_dg(hi[q], ones_sel, TN_DIMS) + _dg(lo[q], ones_sel, TN_DIMS)) for q in quads]
        nrm = [jnp.sqrt(jnp.sum(kkr[q] * kkr[q], axis=-1, keepdims=True)) for q in quads]
        kk = [kkr[q] / jnp.maximum(nrm[q], 1e-12) for q in quads]
        bt0 = [kk[q] * a_s[c, rows[q], :] for q in quads]
        cum_ex = [cum[q] - (hi[q].astype(F32) + lo[q].astype(F32)) for q in quads]
        e_neg = [jnp.exp(-cum[q]) for q in quads]
        alb = [(kk[q] * jnp.exp(cum_ex[q])).astype(BF16) for q in quads]
        rbb = [(r[q] * jnp.exp(cum[q])).astype(BF16) for q in quads]
        kh = [(kq[q] * e_neg[q]).astype(BF16) for q in quads]
        bh = [(bt0[q] * e_neg[q]).astype(BF16) for q in quads]
        e_tail = [jnp.exp(tot[q] - cum[q]) for q in quads]
        kt = [(kq[q] * e_tail[q]).astype(BF16) for q in quads]
        btl = [(bt0[q] * e_tail[q]).astype(BF16) for q in quads]
        lmat = [jnp.where(strict, _dg(alb[q], bh[q], NT_DIMS), 0.0).astype(BF16) for q in quads]
        a_ak = [jnp.where(strict, _dg(alb[q], kh[q], NT_DIMS), 0.0).astype(BF16) for q in quads]
        a_rb = [jnp.where(incl, _dg(rbb[q], bh[q], NT_DIMS), 0.0).astype(BF16) for q in quads]
        a_rk = [jnp.where(incl, _dg(rbb[q], kh[q], NT_DIMS), 0.0).astype(BF16) for q in quads]
        x = [_dot(a_ak[q], vpb[q]) + _dot(alb[q], e2) for q in quads]
        x = [x[q] - _dot(lmat[q], x[q].astype(BF16)) for q in quads]
        pw = lmat
        step = 2
        while step < RW_CHUNK:
            pw = [_dot(pw[q], pw[q]).astype(BF16) for q in quads]
            x = [x[q] + _dot(pw[q], x[q].astype(BF16)) for q in quads]
            step *= 2
        xb = [x[q].astype(BF16) for q in quads]
        yr = [_dot(a_rk[q], vpb[q]) - _dot(a_rb[q], xb[q]) + _dot(rbb[q], e2) for q in quads]
        yrb = [yr[q].astype(BF16) for q in quads]

        heads = [(q, hh) for q in quads for hh in range(RW_QUAD)]
        hsl = [slice(hh * RW_CHUNK, (hh + 1) * RW_CHUNK) for hh in range(RW_QUAD)]
        h_aug = [st_s[q * RW_QUAD + hh] for q, hh in heads]
        hb = [h.astype(BF16) for h in h_aug]
        ys = [_dot(yrb[q][hsl[hh]], hb[n]) + yr[q][hsl[hh], 0:nd] for n, (q, hh) in enumerate(heads)]
        us = [_dot(xb[q][hsl[hh]], hb[n]) + x[q][hsl[hh], 0:nd] for n, (q, hh) in enumerate(heads)]
        kv = [_dg(kt[q][hsl[hh]], vpb[q][hsl[hh], 0:nd], TN_DIMS) for q, hh in heads]
        bu = [_dg(btl[q][hsl[hh]], us[n].astype(BF16), TN_DIMS) for n, (q, hh) in enumerate(heads)]
        for n, (q, hh) in enumerate(heads):
            st_s[n, nd:2 * nd, :] = pc_t[q][:, hh * 128:hh * 128 + nd] * h_aug[n][nd:2 * nd, :] + kv[n] - bu[n]
        for n, (q, hh) in enumerate(heads):
            y = ys[n]
            mu = jnp.mean(y, axis=-1, keepdims=True)
            d = y - mu
            var = jnp.mean(d * d, axis=-1, keepdims=True)
            yn = d * lax.rsqrt(var + RW_GN_EPS) * lng_ref[n] + lnb_ref[n]
            hs = hsl[hh]
            bonus = jnp.sum(r[q][hs] * kq[q][hs] * rk_ref[n], axis=-1, keepdims=True) * vpad[q][hs, 0:nd]
            orow = slice(n * RW_CHUNK, (n + 1) * RW_CHUNK)
            y_s[c, orow, :] = (yn + bonus) * g_s[c, orow, :]
        return carry

    lax.fori_loop(0, n_chunks, chunk_body, 0)

    for c in range(n_chunks):
        for h in range(RW_HEADS):
            o_ref[0, c * RW_CHUNK:(c + 1) * RW_CHUNK, h * RW_HEAD_DIM:(h + 1) * RW_HEAD_DIM] = (
                y_s[c, h * RW_CHUNK:(h + 1) * RW_CHUNK, :].astype(o_ref.dtype))


def _rwkv(z3, zm3, mu_r, mu_k, mu_v, mu_m, w0, wb, a0, ab, gb, kk_p, ka_p, rk_p, lng, lnb):
    bsz, seq, _ = z3.shape
    tb = RW_TB
    n_chunks = tb // RW_CHUNK
    cb = COL_R // RW_WIDTH

    def zspec(col_blk, width):
        return pl.BlockSpec((1, tb, width), lambda b, i: (b, i, col_blk))

    zrows = 8 * 4 // z3.dtype.itemsize

    def pspec(col_blk, width, rows):
        return pl.BlockSpec((1, rows, width), lambda b, i: (b, jnp.maximum(i * (tb // rows) - 1, 0), col_blk))

    def full(shape):
        return pl.BlockSpec(shape, lambda b, i: (0,) * len(shape))

    stage = pltpu.VMEM((n_chunks, RW_HEADS * RW_CHUNK, RW_HEAD_DIM), F32)
    stage_wide = pltpu.VMEM((n_chunks, RW_HEADS * RW_CHUNK, 2 * RW_HEAD_DIM), F32)
    return pl.pallas_call(
        functools.partial(_rwkv_kernel, n_chunks=n_chunks),
        grid=(bsz, seq // tb),
        in_specs=[zspec(cb, RW_WIDTH), zspec(cb + 1, RW_WIDTH), zspec(cb + 2, RW_WIDTH),
                  zspec(0, MISC_W),
                  pspec(cb, RW_WIDTH, zrows), pspec(cb + 1, RW_WIDTH, zrows), pspec(cb + 2, RW_WIDTH, zrows),
                  pspec(0, MISC_W, 8),
                  full((1, RW_WIDTH)), full((1, RW_WIDTH)), full((1, RW_WIDTH)), full((1, MISC_W)),
                  full((1, RW_WIDTH)), full((128, RW_WIDTH)), full((1, RW_WIDTH)), full((128, RW_WIDTH)),
                  full((256, RW_WIDTH)),
                  full((1, RW_WIDTH)), full((1, RW_WIDTH)),
                  full((RW_HEADS, 1, RW_HEAD_DIM)), full((RW_HEADS, 1, RW_HEAD_DIM)),
                  full((RW_HEADS, 1, RW_HEAD_DIM))],
        out_specs=pl.BlockSpec((1, tb, RW_WIDTH), lambda b, i: (b, i, 0)),
        out_shape=jax.ShapeDtypeStruct((bsz, seq, RW_WIDTH), BF16),
        scratch_shapes=[stage, stage, stage_wide, stage, stage, stage, stage, stage,
                        pltpu.VMEM((RW_HEADS, 2 * RW_HEAD_DIM, RW_HEAD_DIM), F32)],
        compiler_params=_params("parallel", "arbitrary"),
        name="rwkv7",
    )(z3, z3, z3, zm3, z3, z3, z3, zm3, mu_r, mu_k, mu_v, mu_m, w0, wb, a0, ab, gb,
      kk_p, ka_p, rk_p, lng, lnb)


FX_CUM_BLOCK = 256
FX_HEAD_GROUP = 8


def _fox_cum_kernel(zf_ref, fb_ref, col_ref, *, seq):
    n = FX_CUM_BLOCK
    ri = lax.broadcasted_iota(jnp.int32, (n, n), 0)
    ci = lax.broadcasted_iota(jnp.int32, (n, n), 1)
    tri = jnp.where(ci <= ri, 1.0, 0.0).astype(BF16)
    carry = jnp.zeros((1, 128), F32)
    for j in range(seq // n):
        rs = slice(j * n, (j + 1) * n)
        x = zf_ref[0, rs, :] + fb_ref[...]
        lf = jnp.minimum(x, 0.0) - jnp.log(1.0 + jnp.exp(-jnp.abs(x)))
        hi = lf.astype(BF16)
        r1 = lf - hi.astype(F32)
        mid = r1.astype(BF16)
        lo = (r1 - mid.astype(F32)).astype(BF16)
        c = _dot(tri, hi) + _dot(tri, mid) + _dot(tri, lo) + carry
        col_ref[0, rs, :] = c
        carry = c[n - 1:n, :]


def _fox_cum(zm3, fb_pad):
    bsz, seq, _ = zm3.shape
    return pl.pallas_call(
        functools.partial(_fox_cum_kernel, seq=seq),
        grid=(bsz,),
        in_specs=[pl.BlockSpec((1, seq, 128), lambda b: (b, 0, MISC_F_LANE // 128)),
                  pl.BlockSpec((1, 128), lambda b: (0, 0))],
        out_specs=pl.BlockSpec((1, seq, 128), lambda b: (b, 0, 0)),
        out_shape=jax.ShapeDtypeStruct((bsz, seq, 128), F32),
        compiler_params=_params("parallel"),
        name="fox_cum",
    )(zm3, fb_pad)


FX_AUG = 128
FX_BIAS_LANE = 64


def _split3(x):
    hi = x.astype(BF16)
    r1 = x - hi.astype(F32)
    mid = r1.astype(BF16)
    lo = (r1 - mid.astype(F32)).astype(BF16)
    return jnp.concatenate([hi, mid, lo], axis=1)


def _foxt_kernel(q_ref, k_ref, v_ref, cc_ref, eall_ref, selq_ref, selk_ref, eye_ref, o_ref,
                 kaug_s, vt_s, m_s, l_s, acc_s, *, tq, tk, seq):
    i = pl.program_id(1)
    scale = FX_HEAD_DIM ** -0.5
    lane = lax.broadcasted_iota(jnp.int32, (1, FX_HEADS * FX_AUG), 1) % FX_AUG
    const_k = jnp.where((lane >= FX_BIAS_LANE) & (lane < FX_BIAS_LANE + 3), 1.0, 0.0)
    const_q = jnp.where((lane >= FX_BIAS_LANE + 3) & (lane < FX_BIAS_LANE + 6), -1.0, 0.0)
    hs_all = range(FX_HEADS)

    @pl.when(i == 0)
    def _():
        for j in range(seq // tk):
            rs = slice(j * tk, (j + 1) * tk)
            ck3 = _split3(cc_ref[0, rs, :])
            kaug_s[j] = (_dot(k_ref[0, rs, :], eall_ref[...]) + _dot(ck3, selk_ref[...]) + const_k).astype(BF16)
            vt_s[j] = _dg(eye_ref[...], v_ref[0, rs, :], NT_DIMS).astype(BF16)

    q0 = pl.multiple_of(i * tq, tq)
    cq3 = _split3(cc_ref[0, pl.ds(q0, tq), :])
    qaug = (_dot(q_ref[0] * scale, eall_ref[...]) + _dot(cq3, selq_ref[...]) + const_q).astype(BF16)
    qs = [qaug[:, h * FX_AUG:(h + 1) * FX_AUG] for h in hs_all]
    m_s[...] = jnp.full(m_s.shape, -1e30, F32)
    l_s[...] = jnp.zeros(l_s.shape, F32)
    acc_s[...] = jnp.zeros(acc_s.shape, F32)
    n_kv = (i * tq + tq + tk - 1) // tk
    key_idx = lax.broadcasted_iota(jnp.int32, (tk, tq), 0)
    qry_idx = i * tq + lax.broadcasted_iota(jnp.int32, (tk, tq), 1)

    def block(j, masked):
        kblk = kaug_s[j]
        vblk = vt_s[j]
        for g0 in range(0, FX_HEADS, FX_HEAD_GROUP):
            hs = range(g0, g0 + FX_HEAD_GROUP)
            st = {h: _dg(kblk[:, h * FX_AUG:(h + 1) * FX_AUG], qs[h], NT_DIMS) for h in hs}
            if masked:
                keep = key_idx + j * tk <= qry_idx
                st = {h: jnp.where(keep, st[h], -1e30) for h in hs}
            m_prev = {h: m_s[h:h + 1, :] for h in hs}
            m_new = {h: jnp.maximum(m_prev[h], jnp.max(st[h], axis=0, keepdims=True)) for h in hs}
            pt = {h: jnp.exp(st[h] - m_new[h]) for h in hs}
            alphas = {h: jnp.exp(m_prev[h] - m_new[h]) for h in hs}
            pv = {h: _dot(vblk[h * FX_HEAD_DIM:(h + 1) * FX_HEAD_DIM, :], pt[h].astype(BF16)) for h in hs}
            for h in hs:
                l_s[h:h + 1, :] = alphas[h] * l_s[h:h + 1, :] + jnp.sum(pt[h], axis=0, keepdims=True)
                acc_s[h] = alphas[h] * acc_s[h] + pv[h]
                m_s[h:h + 1, :] = m_new[h]

    def visible_body(j, carry):
        block(j, masked=False)
        return carry

    lax.fori_loop(0, n_kv - 1, visible_body, 0)
    block(n_kv - 1, masked=True)
    out_t = jnp.concatenate([acc_s[h] / l_s[h:h + 1, :] for h in hs_all], axis=0)
    o_ref[0] = out_t.T.astype(o_ref.dtype)


def _fox_tables():
    col = jnp.arange(FX_HEADS * FX_AUG)[None, :]
    head, d = col // FX_AUG, col % FX_AUG
    c = jnp.arange(FX_WIDTH)[:, None]
    e_all = ((c == head * FX_HEAD_DIM + d) & (d < FX_HEAD_DIM)).astype(BF16)
    r = jnp.arange(3 * 128)[:, None]
    piece, src_lane = r // 128, r % 128
    from_head = src_lane == FX_F_LOCAL + head
    selq = (from_head & (d == FX_BIAS_LANE + piece)).astype(BF16)
    selk = (from_head & (d == FX_BIAS_LANE + 3 + piece)).astype(BF16)
    return e_all, selq, selk, jnp.eye(FX_WIDTH, dtype=BF16)


def _foxt(zfx3, cum_col, tables, tq=256, tk=512):
    bsz, seq, _ = zfx3.shape
    assert tk % tq == 0 and seq % tk == 0

    def full(a):
        return pl.BlockSpec(a.shape, lambda b, i: (0,) * a.ndim)

    return pl.pallas_call(
        functools.partial(_foxt_kernel, tq=tq, tk=tk, seq=seq),
        grid=(bsz, seq // tq),
        in_specs=[pl.BlockSpec((1, tq, FX_WIDTH), lambda b, i: (b, i, 0)),
                  pl.BlockSpec((1, seq, FX_WIDTH), lambda b, i: (b, 0, 1)),
                  pl.BlockSpec((1, seq, FX_WIDTH), lambda b, i: (b, 0, 2)),
                  pl.BlockSpec((1, seq, 128), lambda b, i: (b, 0, 0))] + [full(t) for t in tables],
        out_specs=pl.BlockSpec((1, tq, FX_WIDTH), lambda b, i: (b, i, 0)),
        out_shape=jax.ShapeDtypeStruct((bsz, seq, FX_WIDTH), BF16),
        scratch_shapes=[pltpu.VMEM((seq // tk, tk, FX_HEADS * FX_AUG), BF16),
                        pltpu.VMEM((seq // tk, FX_WIDTH, tk), BF16),
                        pltpu.VMEM((FX_HEADS, tq), F32), pltpu.VMEM((FX_HEADS, tq), F32),
                        pltpu.VMEM((FX_HEADS, FX_HEAD_DIM, tq), F32)],
        compiler_params=_params("parallel", "arbitrary"),
        name="fox_attn",
    )(zfx3, zfx3, zfx3, cum_col, *tables)


def _merge_kernel(x_ref, wg0_ref, wg1_ref, wg2_ref, b0_ref, b1_ref, b2_ref,
                  ya_ref, yb_ref, yc_ref, pa_ref, pb_ref, pc_ref, o_ref):
    x = x_ref[...]

    def gate(w_ref, b_ref):
        return _sigmoid(_dg(x, w_ref[...], NT_DIMS) + b_ref[...])

    acc = gate(wg0_ref, b0_ref) * _dot(ya_ref[...], pa_ref[...])
    acc = acc + gate(wg1_ref, b1_ref) * _dot(yb_ref[...], pb_ref[...])
    acc = acc + gate(wg2_ref, b2_ref) * _dot(yc_ref[...], pc_ref[...])
    o_ref[...] = acc.astype(o_ref.dtype)


def _merge(x16, w_gate, gate_b, ya, yb, yc, pa, pb, pc, layer, tm=1024, tn=512):
    m = ya.shape[0]
    per = D_MODEL // tn

    def yspec(width):
        return pl.BlockSpec((tm, width), lambda i, j: (i, 0))

    def wspec(width):
        return pl.BlockSpec((None, width, tn), lambda i, j: (layer, 0, j))

    def gspec(k):
        return pl.BlockSpec((None, tn, D_MODEL), lambda i, j: (layer, k * per + j, 0))

    def bspec(k):
        return pl.BlockSpec((1, tn), lambda i, j: (0, k * per + j))

    return pl.pallas_call(
        _merge_kernel,
        grid=(m // tm, D_MODEL // tn),
        in_specs=[yspec(D_MODEL), gspec(0), gspec(1), gspec(2), bspec(0), bspec(1), bspec(2),
                  yspec(GM_WIDTH), yspec(RW_WIDTH), yspec(FX_WIDTH),
                  wspec(GM_WIDTH), wspec(RW_WIDTH), wspec(FX_WIDTH)],
        out_specs=pl.BlockSpec((tm, tn), lambda i, j: (i, j)),
        out_shape=jax.ShapeDtypeStruct((m, D_MODEL), BF16),
        compiler_params=_params("parallel", "parallel"),
        name="merge",
    )(x16, w_gate, w_gate, w_gate, gate_b, gate_b, gate_b, ya, yb, yc, pa, pb, pc)


def _out_ln_kernel(m_ref, w_ref, x_ref, g_ref, b_ref, o32_ref, o16_ref):
    h = DEEPNORM_ALPHA * x_ref[...] + _dot(m_ref[...], w_ref[...])
    y = _layer_norm(h, g_ref[...], b_ref[...], LN_EPS)
    o32_ref[...] = y
    o16_ref[...] = y.astype(BF16)


def _out_ln(merged, w_out, layer, x, g, b, tm=512):
    m = x.shape[0]
    row = pl.BlockSpec((tm, D_MODEL), lambda i: (i, 0))
    vec = pl.BlockSpec((1, D_MODEL), lambda i: (0, 0))
    return pl.pallas_call(
        _out_ln_kernel,
        grid=(m // tm,),
        in_specs=[row, pl.BlockSpec((None, D_MODEL, D_MODEL), lambda i: (layer, 0, 0)), row, vec, vec],
        out_specs=[row, row],
        out_shape=[jax.ShapeDtypeStruct((m, D_MODEL), F32), jax.ShapeDtypeStruct((m, D_MODEL), BF16)],
        compiler_params=_params("parallel"),
        name="out_ln",
    )(merged, w_out, x, g, b)


def _router_kernel(x_ref, w_ref, b_ref, o_ref):
    logits = jnp.dot(x_ref[...], w_ref[...], preferred_element_type=F32,
                     precision=lax.Precision.HIGHEST) + b_ref[...]
    lane = lax.broadcasted_iota(jnp.int32, logits.shape, 1)
    logits = jnp.where(lane < MOE_EXPERTS, logits, -1e30)
    m1 = jnp.max(logits, axis=-1, keepdims=True)
    i1 = jnp.min(jnp.where(logits == m1, lane, 128), axis=-1, keepdims=True)
    rest = jnp.where(lane == i1, -1e30, logits)
    m2 = jnp.max(rest, axis=-1, keepdims=True)
    i2 = jnp.min(jnp.where(rest == m2, lane, 128), axis=-1, keepdims=True)
    e2 = jnp.exp(m2 - m1)
    w1 = 1.0 / (1.0 + e2)
    o_ref[...] = jnp.where(lane == i1, w1, 0.0) + jnp.where(lane == i2, e2 * w1, 0.0)


def _router(x1, w_pad, b_pad, tm=1024):
    m = x1.shape[0]
    return pl.pallas_call(
        _router_kernel,
        grid=(m // tm,),
        in_specs=[pl.BlockSpec((tm, D_MODEL), lambda i: (i, 0)),
                  pl.BlockSpec((D_MODEL, 128), lambda i: (0, 0)),
                  pl.BlockSpec((1, 128), lambda i: (0, 0))],
        out_specs=pl.BlockSpec((tm, 128), lambda i: (i, 0)),
        out_shape=jax.ShapeDtypeStruct((m, 128), F32),
        compiler_params=_params("parallel"),
        name="router",
    )(x1, w_pad, b_pad)


def _ffn_kernel(x_ref, c_ref, w1_ref, w3_ref, w2_ref, o_ref):
    @pl.when((pl.program_id(1) == 0) & (pl.program_id(2) == 0))
    def _():
        o_ref[...] = jnp.zeros_like(o_ref)

    x = x_ref[...]
    g = _dot(x, w1_ref[0])
    u = _dot(x, w3_ref[0])
    h = (g * _sigmoid(g)) * u * c_ref[0]
    o_ref[...] += _dot(h.astype(BF16), w2_ref[0])


def _ffn(xb, comb, w1, w3, w2, first, n_e, tm=1024, tf=512):
    m = xb.shape[0]
    f = w1.shape[-1]
    return pl.pallas_call(
        _ffn_kernel,
        grid=(m // tm, n_e, f // tf),
        in_specs=[pl.BlockSpec((tm, D_MODEL), lambda i, e, j: (i, 0)),
                  pl.BlockSpec((1, tm, 1), lambda i, e, j: (e, i, 0)),
                  pl.BlockSpec((1, D_MODEL, tf), lambda i, e, j: (first + e, 0, j)),
                  pl.BlockSpec((1, D_MODEL, tf), lambda i, e, j: (first + e, 0, j)),
                  pl.BlockSpec((1, tf, D_MODEL), lambda i, e, j: (first + e, j, 0))],
        out_specs=pl.BlockSpec((tm, D_MODEL), lambda i, e, j: (i, 0)),
        out_shape=jax.ShapeDtypeStruct((m, D_MODEL), F32),
        compiler_params=_params("parallel", "arbitrary", "arbitrary"),
        name="swiglu",
    )(xb, comb, w1, w3, w2)


MOE_BLK = 288
MOE_SUB = 1024
MOE_NSUB = 1
MOE_PASS = 4


def _moe_kernel(ntot_ref, bsub_ref, bloc_ref, x_ref, comb_ref, rank_ref, rankt_ref, w1_ref, w3_ref, w2_ref,
                o_ref, xs_s, acc_s, *, n_f, col_split):
    i, e, g, f = pl.program_id(0), pl.program_id(1), pl.program_id(2), pl.program_id(3)
    row = i * MOE_EXPERTS + e
    nb = jnp.clip(ntot_ref[row] - g * MOE_PASS, 0, MOE_PASS)

    @pl.when((e == 0) & (g == 0) & (f == 0))
    def _():
        o_ref[...] = jnp.zeros_like(o_ref)

    def sub_rows(b):
        sub = bsub_ref[row, g * MOE_PASS + b]
        return sub, pl.ds(pl.multiple_of(sub * MOE_SUB, MOE_SUB), MOE_SUB)

    @pl.when(f == 0)
    def _():
        slot = lax.broadcasted_iota(jnp.int32, (MOE_BLK, MOE_SUB), 0)

        def gather(b, carry):
            sub, rows = sub_rows(b)
            first = bloc_ref[row, g * MOE_PASS + b] * MOE_BLK
            rank_row = rankt_ref[sub, pl.ds(e, 1), :]
            onehot = jnp.where(rank_row == slot + first, 1.0, 0.0).astype(BF16)
            xs_s[b] = _dot(onehot, x_ref[rows, :]).astype(BF16)
            acc_s[b] = jnp.zeros(acc_s.shape[1:], F32)
            return carry

        lax.fori_loop(0, nb, gather, 0)

    def compute(b, carry):
        xb = xs_s[b]
        gate = _dot(xb, w1_ref[0])
        up = _dot(xb, w3_ref[0])
        h = (gate * _sigmoid(gate)) * up
        acc_s[b] += _dot(h.astype(BF16), w2_ref[0])
        return carry

    lax.fori_loop(0, nb, compute, 0)

    @pl.when(f == n_f - 1)
    def _():
        lane = lax.broadcasted_iota(jnp.int32, (MOE_SUB, 128), 1)
        sel = lane == e
        slot = lax.broadcasted_iota(jnp.int32, (MOE_SUB, MOE_BLK), 1)
        cw = D_MODEL // col_split

        def scatter(b, carry):
            _, rows = sub_rows(b)
            first = bloc_ref[row, g * MOE_PASS + b] * MOE_BLK
            c_col = jnp.sum(jnp.where(sel, comb_ref[rows, :], 0.0), axis=-1, keepdims=True)
            r_col = jnp.sum(jnp.where(sel, rank_ref[rows, :], 0), axis=-1, keepdims=True)
            onehot_t = jnp.where(r_col == slot + first, 1.0, 0.0).astype(BF16)
            for s in range(col_split):
                cs = slice(s * cw, (s + 1) * cw)
                o_ref[rows, cs] += _dot(onehot_t, acc_s[b, :, cs].astype(BF16)) * c_col
            return carry

        lax.fori_loop(0, nb, scatter, 0)


def _moe(xb, comb, plan, w1, w3, w2, layer, tf=1024):
    rank, rank_t, ntot, bsub, bloc = plan
    m = xb.shape[0]
    tm = MOE_SUB * MOE_NSUB
    _, n_e, _, f = w1.shape
    n_f = f // tf
    blocks_max = MOE_NSUB * pl.cdiv(MOE_SUB, MOE_BLK)
    n_pass = pl.cdiv(blocks_max, MOE_PASS)
    once = pl.Buffered(1)

    def wmap(sel):
        def index_map(i, e, g, j, ntot_ref, bsub_ref, bloc_ref):
            live = (g == 0) | (ntot_ref[i * MOE_EXPERTS + e] > g * MOE_PASS)
            jj = jnp.where(live, j, n_f - 1)
            return (layer, e, 0, jj) if sel == 0 else (layer, e, jj, 0)
        return index_map

    def tile(i, e, g, j, *_):
        return (i, 0)

    grid_spec = pltpu.PrefetchScalarGridSpec(
        num_scalar_prefetch=3,
        grid=(m // tm, n_e, n_pass, n_f),
        in_specs=[pl.BlockSpec((tm, D_MODEL), tile, pipeline_mode=once),
                  pl.BlockSpec((tm, 128), tile, pipeline_mode=once),
                  pl.BlockSpec((tm, 128), tile, pipeline_mode=once),
                  pl.BlockSpec((MOE_NSUB, 128, MOE_SUB), lambda i, e, g, j, *_: (i, 0, 0), pipeline_mode=once),
                  pl.BlockSpec((None, 1, D_MODEL, tf), wmap(0)),
                  pl.BlockSpec((None, 1, D_MODEL, tf), wmap(0)),
                  pl.BlockSpec((None, 1, tf, D_MODEL), wmap(1))],
        out_specs=pl.BlockSpec((tm, D_MODEL), tile, pipeline_mode=once),
        scratch_shapes=[pltpu.VMEM((MOE_PASS, MOE_BLK, D_MODEL), BF16),
                        pltpu.VMEM((MOE_PASS, MOE_BLK, D_MODEL), F32)],
    )
    return pl.pallas_call(
        functools.partial(_moe_kernel, n_f=n_f, col_split=2),
        grid_spec=grid_spec,
        out_shape=jax.ShapeDtypeStruct((m, D_MODEL), F32),
        compiler_params=_params("parallel", "arbitrary", "arbitrary", "arbitrary"),
        name="moe",
    )(ntot, bsub, bloc, xb, comb, rank, rank_t, w1, w3, w2)


def _moe_plan(comb):
    m = comb.shape[0]
    n_sub = m // MOE_SUB
    routed = (comb > 0.0).reshape(n_sub, MOE_SUB, 128).astype(jnp.int32)
    rank = jnp.cumsum(routed, axis=1) - routed
    rank = jnp.where(routed > 0, rank, -1)
    count = jnp.sum(routed, axis=1)[:, :MOE_EXPERTS]
    nblk = ((count + MOE_BLK - 1) // MOE_BLK).reshape(n_sub // MOE_NSUB, MOE_NSUB, MOE_EXPERTS)
    ends = jnp.swapaxes(jnp.cumsum(nblk, axis=1), 1, 2)
    starts = ends - jnp.swapaxes(nblk, 1, 2)
    ntot = ends[:, :, -1].reshape(-1)
    blocks_max = MOE_NSUB * pl.cdiv(MOE_SUB, MOE_BLK)
    bidx = jnp.arange(blocks_max)[None, None, :, None]
    bsub = jnp.sum((bidx >= ends[:, :, None, :]).astype(jnp.int32), axis=-1)
    bsub = jnp.minimum(bsub, MOE_NSUB - 1)
    sub_onehot = (bsub[..., None] == jnp.arange(MOE_NSUB)).astype(jnp.int32)
    bloc = bidx[..., 0] - jnp.sum(sub_onehot * starts[:, :, None, :], axis=-1)
    shape2 = (-1, blocks_max)
    return (rank.reshape(m, 128), jnp.swapaxes(rank, 1, 2), ntot.astype(jnp.int32),
            bsub.reshape(shape2).astype(jnp.int32), bloc.reshape(shape2).astype(jnp.int32))


def _ple_kernel(x_ref, wg_ref, bg_ref, p_ref, wp_ref, o_ref):
    gate = _sigmoid(_dot(x_ref[...], wg_ref[...]) + bg_ref[...])
    o_ref[...] = gate * _dot(p_ref[...].astype(BF16), wp_ref[...])


def _ple(x1b, wg, bg, p, wp, layer, tm=1024, tn=1024):
    m = x1b.shape[0]
    return pl.pallas_call(
        _ple_kernel,
        grid=(m // tm, D_MODEL // tn),
        in_specs=[pl.BlockSpec((tm, D_MODEL), lambda i, j: (i, 0)),
                  pl.BlockSpec((None, D_MODEL, tn), lambda i, j: (layer, 0, j)),
                  pl.BlockSpec((1, tn), lambda i, j: (0, j)),
                  pl.BlockSpec((None, tm, PLE_DIM), lambda i, j: (layer, i, 0)),
                  pl.BlockSpec((None, PLE_DIM, tn), lambda i, j: (layer, 0, j))],
        out_specs=pl.BlockSpec((tm, tn), lambda i, j: (i, j)),
        out_shape=jax.ShapeDtypeStruct((m, D_MODEL), F32),
        compiler_params=_params("parallel", "parallel"),
        name="ple",
    )(x1b, wg, bg, p, wp)


def _final_ln_kernel(x1_ref, ff_ref, ple_ref, g_ref, b_ref, o32_ref, o16_ref):
    h = DEEPNORM_ALPHA * x1_ref[...] + ff_ref[...] + ple_ref[...]
    y = _layer_norm(h, g_ref[...], b_ref[...], LN_EPS)
    o32_ref[...] = y
    o16_ref[...] = y.astype(BF16)


def _final_ln(x1, ff, ple, g, b, tm=512):
    m = x1.shape[0]
    row = pl.BlockSpec((tm, D_MODEL), lambda i: (i, 0))
    vec = pl.BlockSpec((1, D_MODEL), lambda i: (0, 0))
    return pl.pallas_call(
        _final_ln_kernel,
        grid=(m // tm,),
        in_specs=[row, row, row, vec, vec],
        out_specs=[row, row],
        out_shape=[jax.ShapeDtypeStruct((m, D_MODEL), F32), jax.ShapeDtypeStruct((m, D_MODEL), BF16)],
        compiler_params=_params("parallel"),
        name="final_ln",
    )(x1, ff, ple, g, b)


def _split_w_in(w_in):
    w_t = jnp.swapaxes(w_in, 1, 2)
    n_l, _, d = w_t.shape
    lora = w_t[:, COL_LORA:COL_FX]
    f = w_t[:, COL_F:COL_GATE]
    pad = jnp.zeros((n_l, MISC_W - lora.shape[1] - f.shape[1], d), w_t.dtype)
    misc = jnp.concatenate([lora, f, pad], axis=1)
    return (w_t[:, :COL_LORA].astype(BF16), misc.astype(BF16),
            w_t[:, COL_FX:COL_F].astype(BF16), w_t[:, COL_GATE:].astype(BF16))


def _pad_rows(w, lo, total):
    n_l, r, n = w.shape
    return jnp.concatenate([jnp.zeros((n_l, lo, n), w.dtype), w,
                            jnp.zeros((n_l, total - lo - r, n), w.dtype)], axis=1)


def kernel(x, p, w_in, gate_b, gm_ln_g, gm_ln_b, gm_ws, gm_bs, rw_mu, rw_w0, rw_wb, rw_a0, rw_ab, rw_gb, rw_kk, rw_ka, rw_rk, rw_lnx_g, rw_lnx_b, fx_fb, proj_a, proj_b, proj_c, w_out, ln1_g, ln1_b, ffn_w1, ffn_w3, ffn_w2, moe_router, moe_router_b, moe_w1, moe_w3, moe_w2, ple_gate_w, ple_gate_b, ple_proj, ln2_g, ln2_b):
    bsz, seq, d = x.shape
    m = bsz * seq
    n_l = w_in.shape[0]

    w_ar, w_misc, w_fx, w_gate = _split_w_in(w_in)
    mu_r = rw_mu[:, None, 0:RW_WIDTH]
    mu_k = rw_mu[:, None, RW_WIDTH:2 * RW_WIDTH]
    mu_v = rw_mu[:, None, 2 * RW_WIDTH:3 * RW_WIDTH]
    o = 3 * RW_WIDTH
    zl = lambda n: jnp.zeros((n_l, n), F32)
    n_lora = RW_LORA_W + RW_LORA_A + RW_LORA_G
    mu_m = jnp.concatenate([rw_mu[:, o:o + n_lora], zl(MISC_W - n_lora)], axis=-1)[:, None, :]
    wb_pad = _pad_rows(rw_wb, 0, 128).astype(BF16)
    ab_pad = _pad_rows(rw_ab, RW_LORA_W, 128).astype(BF16)
    gb_pad = _pad_rows(rw_gb, 0, 256).astype(BF16)
    fb_pad = jnp.concatenate([zl(FX_F_LOCAL), fx_fb, zl(128 - FX_F_LOCAL - FX_HEADS)], axis=-1)[:, None, :]
    head3 = lambda t: t.reshape(n_l, RW_HEADS, 1, RW_HEAD_DIM)
    gm_bs3 = gm_bs[..., None]
    router_w = jnp.concatenate([moe_router, jnp.zeros(moe_router.shape[:2] + (128 - MOE_EXPERTS,), F32)], axis=-1)
    router_b = jnp.concatenate([moe_router_b, jnp.zeros((moe_router_b.shape[0], 128 - MOE_EXPERTS), F32)],
                               axis=-1)[:, None, :]
    bf = lambda t: t.astype(BF16)
    proj_a16, proj_b16, proj_c16, w_out16 = bf(proj_a), bf(proj_b), bf(proj_c), bf(w_out)
    ffn_w1_16, ffn_w3_16, ffn_w2_16 = bf(ffn_w1), bf(ffn_w3), bf(ffn_w2)
    moe_w1_16, moe_w3_16, moe_w2_16 = bf(moe_w1), bf(moe_w3), bf(moe_w2)
    ple_gw16, ple_pw16 = bf(ple_gate_w), bf(ple_proj)
    ones_comb = jnp.ones((1, m, 1), F32)
    p3 = p.reshape(n_l, m, PLE_DIM)
    fox_tables = _fox_tables()

    x32 = x.reshape(m, d)
    x16 = x32.astype(BF16)
    for i in range(n_l):
        z = _matmul(x16, w_ar, i, BF16, 1024, 2048, "in_proj_ar")
        zm = _matmul(x16, w_misc, i, F32, 1024, MISC_W, "in_proj_misc")
        zfx = _matmul(x16, w_fx, i, BF16, 1024, 3 * FX_WIDTH, "in_proj_fx")
        z3 = z.reshape(bsz, seq, z.shape[-1])
        zm3 = zm.reshape(bsz, seq, MISC_W)
        y_a = _gmlp(z, gm_ln_g[i][None], gm_ln_b[i][None], gm_ws[i], gm_bs3[i])
        y_b = _rwkv(z3, zm3, mu_r[i], mu_k[i], mu_v[i], mu_m[i], rw_w0[i][None], wb_pad[i], rw_a0[i][None],
                    ab_pad[i], gb_pad[i], rw_kk[i][None], rw_ka[i][None], head3(rw_rk)[i],
                    head3(rw_lnx_g)[i], head3(rw_lnx_b)[i]).reshape(m, RW_WIDTH)
        cum_col = _fox_cum(zm3, fb_pad[i])
        y_c = _foxt(zfx.reshape(bsz, seq, 3 * FX_WIDTH), cum_col, fox_tables).reshape(m, FX_WIDTH)
        merged = _merge(x16, w_gate, gate_b[i][None], y_a, y_b, y_c, proj_a16, proj_b16, proj_c16, i)
        x1, x1b = _out_ln(merged, w_out16, i, x32, ln1_g[i][None], ln1_b[i][None])
        j = i // 2
        if i % 2 == 0:
            ff = _ffn(x1b, ones_comb, ffn_w1_16, ffn_w3_16, ffn_w2_16, j, 1)
        else:
            comb = _router(x1, router_w[j], router_b[j])
            ff = _moe(x1b, comb, _moe_plan(comb), moe_w1_16, moe_w3_16, moe_w2_16, j)
        ple = _ple(x1b, ple_gw16, ple_gate_b[i][None], p3, ple_pw16, i)
        x32, x16 = _final_ln(x1, ff, ple, ln2_g[i][None], ln2_b[i][None])
    return x32.reshape(bsz, seq, d)
```

```python
import functools
import math

import jax
import jax.numpy as jnp
from jax import lax
from jax.experimental import pallas as pl
from jax.experimental.pallas import tpu as pltpu

F32 = jnp.float32
BF16 = jnp.bfloat16

D_MODEL = 2048
DEPTH = 4
PLE_DIM = 256
GM_CHUNK = 128
GM_GROUPS = 4
GM_WIDTH = 512
RW_HEADS = 16
RW_HEAD_DIM = 64
RW_WIDTH = 1024
RW_LORA_W = 64
RW_LORA_A = 64
RW_LORA_G = 160
RW_GN_EPS = 64e-5
FX_HEADS = 8
FX_HEAD_DIM = 64
FX_WIDTH = 512
MOE_EXPERTS = 8
DEEPNORM_ALPHA = (2.0 * DEPTH) ** 0.25
LN_EPS = 1e-5

COL_A = 0
COL_R = 1024
COL_LORA = 4096
COL_FX = 4384
COL_F = 5920
COL_GATE = 5928
MISC_W = 384
MISC_F_LANE = COL_FX - COL_LORA
FX_F_LOCAL = MISC_F_LANE % 128

RW_CHUNK = 64
RW_QUAD = 2
RW_TB = 256
LOGW_SCALE = math.exp(-0.5)

VMEM_LIMIT = 56 * 1024 * 1024

NT_DIMS = (((1,), (1,)), ((), ()))
TN_DIMS = (((0,), (0,)), ((), ()))


def _params(*sem):
    return pltpu.CompilerParams(dimension_semantics=sem, vmem_limit_bytes=VMEM_LIMIT)


def _dot(a, b):
    return jnp.dot(a, b, preferred_element_type=F32)


def _dg(a, b, dims):
    return lax.dot_general(a, b, dims, preferred_element_type=F32)


def _sigmoid(x):
    return 1.0 / (1.0 + jnp.exp(-x))


def _layer_norm(h, g, b, eps):
    mu = jnp.mean(h, axis=-1, keepdims=True)
    d = h - mu
    var = jnp.mean(d * d, axis=-1, keepdims=True)
    return d * lax.rsqrt(var + eps) * g + b


def _mm_kernel(x_ref, w_ref, o_ref):
    o_ref[...] = _dg(x_ref[...], w_ref[...], NT_DIMS).astype(o_ref.dtype)


def _matmul(x, w_t, layer, out_dtype, tm, tn, name):
    m, k = x.shape
    n = w_t.shape[1]
    return pl.pallas_call(
        _mm_kernel,
        grid=(m // tm, n // tn),
        in_specs=[pl.BlockSpec((tm, k), lambda i, j: (i, 0)),
                  pl.BlockSpec((None, tn, k), lambda i, j: (layer, j, 0))],
        out_specs=pl.BlockSpec((tm, tn), lambda i, j: (i, j)),
        out_shape=jax.ShapeDtypeStruct((m, n), out_dtype),
        compiler_params=_params("parallel", "parallel"),
        name=name,
    )(x, w_t)


def _gmlp_kernel(z_ref, g_ref, b_ref, ws_ref, bs_ref, o_ref, *, rows):
    z = z_ref[...].astype(F32)
    za = 0.5 * z * (1.0 + jnp.tanh(math.sqrt(2.0 / math.pi) * (z + 0.044715 * (z * z * z))))
    u = za[:, :GM_WIDTH]
    v = _layer_norm(za[:, GM_WIDTH:], g_ref[...], b_ref[...], LN_EPS).astype(BF16)
    t_idx = lax.broadcasted_iota(jnp.int32, (GM_CHUNK, GM_CHUNK), 0)
    s_idx = lax.broadcasted_iota(jnp.int32, (GM_CHUNK, GM_CHUNK), 1)
    causal = t_idx >= s_idx
    for g in range(GM_GROUPS):
        w = jnp.where(causal, ws_ref[g], 0.0).astype(BF16)
        cs = slice(g * GM_CHUNK, (g + 1) * GM_CHUNK)
        for c in range(rows // GM_CHUNK):
            rs = slice(c * GM_CHUNK, (c + 1) * GM_CHUNK)
            s = _dot(w, v[rs, cs]) + bs_ref[g]
            o_ref[rs, cs] = (u[rs, cs] * s).astype(o_ref.dtype)


def _gmlp(z, ln_g, ln_b, ws, bs, rows=512):
    m = z.shape[0]
    return pl.pallas_call(
        functools.partial(_gmlp_kernel, rows=rows),
        grid=(m // rows,),
        in_specs=[pl.BlockSpec((rows, 2 * GM_WIDTH), lambda i: (i, COL_A // (2 * GM_WIDTH))),
                  pl.BlockSpec((1, GM_WIDTH), lambda i: (0, 0)),
                  pl.BlockSpec((1, GM_WIDTH), lambda i: (0, 0)),
                  pl.BlockSpec((GM_GROUPS, GM_CHUNK, GM_CHUNK), lambda i: (0, 0, 0)),
                  pl.BlockSpec((GM_GROUPS, GM_CHUNK, 1), lambda i: (0, 0, 0))],
        out_specs=pl.BlockSpec((rows, GM_WIDTH), lambda i: (i, 0)),
        out_shape=jax.ShapeDtypeStruct((m, GM_WIDTH), BF16),
        compiler_params=_params("parallel"),
        name="gmlp",
    )(z, ln_g, ln_b, ws, bs)


def _token_shift(z_ref, p_ref, mu_ref, first):
    h = z_ref[0].astype(F32)
    last = p_ref.shape[1] - 1
    prev_row = jnp.where(first, 0.0, p_ref[0, last:last + 1, :].astype(F32))
    hp = pltpu.roll(h, 1, axis=0)
    row = lax.broadcasted_iota(jnp.int32, h.shape, 0)
    hp = jnp.where(row == 0, prev_row, hp)
    return h + (hp - h) * mu_ref[...]


def _stage(dst_ref, val, n_chunks):
    for c in range(n_chunks):
        for h in range(RW_HEADS):
            dst_ref[c, h * RW_CHUNK:(h + 1) * RW_CHUNK, 0:RW_HEAD_DIM] = (
                val[c * RW_CHUNK:(c + 1) * RW_CHUNK, h * RW_HEAD_DIM:(h + 1) * RW_HEAD_DIM])


def _rwkv_kernel(zr_ref, zk_ref, zv_ref, zm_ref, pr_ref, pk_ref, pv_ref, pm_ref,
                 mur_ref, muk_ref, muv_ref, mum_ref, w0_ref, wb_ref, a0_ref, ab_ref, gb_ref,
                 kk_ref, ka_ref, rk_ref, lng_ref, lnb_ref, o_ref,
                 r_s, k_s, v_s, kk_s, a_s, lw_s, g_s, y_s, st_s, *, n_chunks):
    first = pl.program_id(1) == 0
    qrows = RW_QUAD * RW_CHUNK
    nd = RW_HEAD_DIM

    @pl.when(first)
    def _():
        st_s[...] = jnp.zeros_like(st_s)
        v_s[...] = jnp.zeros_like(v_s)

    m = _token_shift(zm_ref, pm_ref, mum_ref, first)
    m01 = m[:, 0:128]
    lw_lin = _dot(jnp.tanh(m01).astype(BF16), wb_ref[...]) + w0_ref[...]
    _stage(lw_s, -LOGW_SCALE * _sigmoid(lw_lin), n_chunks)
    a = _sigmoid(a0_ref[...] + _dot(m01.astype(BF16), ab_ref[...]))
    _stage(a_s, a, n_chunks)
    _stage(g_s, _dot(_sigmoid(m[:, 128:384]).astype(BF16), gb_ref[...]), n_chunks)
    k = _token_shift(zk_ref, pk_ref, muk_ref, first)
    _stage(kk_s, k * kk_ref[...], n_chunks)
    _stage(k_s, k * (1.0 + (a - 1.0) * ka_ref[...]), n_chunks)
    _stage(r_s, _token_shift(zr_ref, pr_ref, mur_ref, first), n_chunks)
    _stage(v_s, _token_shift(zv_ref, pv_ref, muv_ref, first), n_chunks)

    ri = lax.broadcasted_iota(jnp.int32, (qrows, qrows), 0)
    ci = lax.broadcasted_iota(jnp.int32, (qrows, qrows), 1)
    shift = RW_CHUNK.bit_length() - 1
    same = (ri >> shift) == (ci >> shift)
    strict = same & (ci < ri)
    incl = same & (ci <= ri)
    tri_bd = jnp.where(incl, 1.0, 0.0).astype(BF16)
    ones_sel = jnp.where(
        (lax.broadcasted_iota(jnp.int32, (qrows, RW_QUAD * 128), 0) >> shift)
        == (lax.broadcasted_iota(jnp.int32, (qrows, RW_QUAD * 128), 1) >> 7), 1.0, 0.0).astype(BF16)
    e2 = jnp.where(lax.broadcasted_iota(jnp.int32, (nd, 2 * nd), 1)
                   == lax.broadcasted_iota(jnp.int32, (nd, 2 * nd), 0) + nd, 1.0, 0.0).astype(BF16)

    def chunk_body(c, carry):
        quads = range(RW_HEADS // RW_QUAD)
        rows = [slice(q * qrows, (q + 1) * qrows) for q in quads]
        r = [r_s[c, rows[q], :] for q in quads]
        kq = [k_s[c, rows[q], :] for q in quads]
        vpad = [v_s[c, rows[q], :] for q in quads]
        vpb = [vpad[q].astype(BF16) for q in quads]
        kkr = [kk_s[c, rows[q], :] for q in quads]
        lw = [lw_s[c, rows[q], :] for q in quads]
        hi = [lw[q].astype(BF16) for q in quads]
        lo = [(lw[q] - hi[q].astype(F32)).astype(BF16) for q in quads]
        cum = [_dot(tri_bd, hi[q]) + _dot(tri_bd, lo[q]) for q in quads]
        tot = [jnp.concatenate([jnp.broadcast_to(cum[q][(hh + 1) * RW_CHUNK - 1:(hh + 1) * RW_CHUNK, :], (RW_CHUNK, nd))
                                for hh in range(RW_QUAD)], axis=0) for q in quads]
        pc_t = [jnp.exp(_dg(hi[q], ones_sel, TN_DIMS) + _dg(lo[q], ones_sel, TN_DIMS)) for q in quads]
        nrm = [jnp.sqrt(jnp.sum(kkr[q] * kkr[q], axis=-1, keepdims=True)) for q in quads]
        kk = [kkr[q] / jnp.maximum(nrm[q], 1e-12) for q in quads]
        bt0 = [kk[q] * a_s[c, rows[q], :] for q in quads]
        cum_ex = [cum[q] - (hi[q].astype(F32) + lo[q].astype(F32)) for q in quads]
        e_neg = [jnp.exp(-cum[q]) for q in quads]
        alb = [(kk[q] * jnp.exp(cum_ex[q])).astype(BF16) for q in quads]
        rbb = [(r[q] * jnp.exp(cum[q])).astype(BF16) for q in quads]
        kh = [(kq[q] * e_neg[q]).astype(BF16) for q in quads]
        bh = [(bt0[q] * e_neg[q]).astype(BF16) for q in quads]
        e_tail = [jnp.exp(tot[q] - cum[q]) for q in quads]
        kt = [(kq[q] * e_tail[q]).astype(BF16) for q in quads]
        btl = [(bt0[q] * e_tail[q]).astype(BF16) for q in quads]
        lmat = [jnp.where(strict, _dg(alb[q], bh[q], NT_DIMS), 0.0).astype(BF16) for q in quads]
        a_ak = [jnp.where(strict, _dg(alb[q], kh[q], NT_DIMS), 0.0).astype(BF16) for q in quads]
        a_rb = [jnp.where(incl, _dg(rbb[q], bh[q], NT_DIMS), 0.0).astype(BF16) for q in quads]
        a_rk = [jnp.where(incl, _dg(rbb[q], kh[q], NT_DIMS), 0.0).astype(BF16) for q in quads]
        x = [_dot(a_ak[q], vpb[q]) + _dot(alb[q], e2) for q in quads]
        x = [x[q] - _dot(lmat[q], x[q].astype(BF16)) for q in quads]
        pw = lmat
        step = 2
        while step < RW_CHUNK:
            pw = [_dot(pw[q], pw[q]).astype(BF16) for q in quads]
            x = [x[q] + _dot(pw[q], x[q].astype(BF16)) for q in quads]
            step *= 2
        xb = [x[q].astype(BF16) for q in quads]
        yr = [_dot(a_rk[q], vpb[q]) - _dot(a_rb[q], xb[q]) + _dot(rbb[q], e2) for q in quads]
        yrb = [yr[q].astype(BF16) for q in quads]

        heads = [(q, hh) for q in quads for hh in range(RW_QUAD)]
        hsl = [slice(hh * RW_CHUNK, (hh + 1) * RW_CHUNK) for hh in range(RW_QUAD)]
        h_aug = [st_s[q * RW_QUAD + hh] for q, hh in heads]
        hb = [h.astype(BF16) for h in h_aug]
        ys = [_dot(yrb[q][hsl[hh]], hb[n]) + yr[q][hsl[hh], 0:nd] for n, (q, hh) in enumerate(heads)]
        us = [_dot(xb[q][hsl[hh]], hb[n]) + x[q][hsl[hh], 0:nd] for n, (q, hh) in enumerate(heads)]
        kv = [_dg(kt[q][hsl[hh]], vpb[q][hsl[hh], 0:nd], TN_DIMS) for q, hh in heads]
        bu = [_dg(btl[q][hsl[hh]], us[n].astype(BF16), TN_DIMS) for n, (q, hh) in enumerate(heads)]
        for n, (q, hh) in enumerate(heads):
            st_s[n, nd:2 * nd, :] = pc_t[q][:, hh * 128:hh * 128 + nd] * h_aug[n][nd:2 * nd, :] + kv[n] - bu[n]
        for n, (q, hh) in enumerate(heads):
            y = ys[n]
            mu = jnp.mean(y, axis=-1, keepdims=True)
            d = y - mu
            var = jnp.mean(d * d, axis=-1, keepdims=True)
            yn = d * lax.rsqrt(var + RW_GN_EPS) * lng_ref[n] + lnb_ref[n]
            hs = hsl[hh]
            bonus = jnp.sum(r[q][hs] * kq[q][hs] * rk_ref[n], axis=-1, keepdims=True) * vpad[q][hs, 0:nd]
            orow = slice(n * RW_CHUNK, (n + 1) * RW_CHUNK)
            y_s[c, orow, :] = (yn + bonus) * g_s[c, orow, :]
        return carry

    lax.fori_loop(0, n_chunks, chunk_body, 0)

    for c in range(n_chunks):
        for h in range(RW_HEADS):
            o_ref[0, c * RW_CHUNK:(c + 1) * RW_CHUNK, h * RW_HEAD_DIM:(h + 1) * RW_HEAD_DIM] = (
                y_s[c, h * RW_CHUNK:(h + 1) * RW_CHUNK, :].astype(o_ref.dtype))


def _rwkv(z3, zm3, mu_r, mu_k, mu_v, mu_m, w0, wb, a0, ab, gb, kk_p, ka_p, rk_p, lng, lnb):
    bsz, seq, _ = z3.shape
    tb = RW_TB
    n_chunks = tb // RW_CHUNK
    cb = COL_R // RW_WIDTH

    def zspec(col_blk, width):
        return pl.BlockSpec((1, tb, width), lambda b, i: (b, i, col_blk))

    zrows = 8 * 4 // z3.dtype.itemsize

    def pspec(col_blk, width, rows):
        return pl.BlockSpec((1, rows, width), lambda b, i: (b, jnp.maximum(i * (tb // rows) - 1, 0), col_blk))

    def full(shape):
        return pl.BlockSpec(shape, lambda b, i: (0,) * len(shape))

    stage = pltpu.VMEM((n_chunks, RW_HEADS * RW_CHUNK, RW_HEAD_DIM), F32)
    stage_wide = pltpu.VMEM((n_chunks, RW_HEADS * RW_CHUNK, 2 * RW_HEAD_DIM), F32)
    return pl.pallas_call(
        functools.partial(_rwkv_kernel, n_chunks=n_chunks),
        grid=(bsz, seq // tb),
        in_specs=[zspec(cb, RW_WIDTH), zspec(cb + 1, RW_WIDTH), zspec(cb + 2, RW_WIDTH),
                  zspec(0, MISC_W),
                  pspec(cb, RW_WIDTH, zrows), pspec(cb + 1, RW_WIDTH, zrows), pspec(cb + 2, RW_WIDTH, zrows),
                  pspec(0, MISC_W, 8),
                  full((1, RW_WIDTH)), full((1, RW_WIDTH)), full((1, RW_WIDTH)), full((1, MISC_W)),
                  full((1, RW_WIDTH)), full((128, RW_WIDTH)), full((1, RW_WIDTH)), full((128, RW_WIDTH)),
                  full((256, RW_WIDTH)),
                  full((1, RW_WIDTH)), full((1, RW_WIDTH)),
                  full((RW_HEADS, 1, RW_HEAD_DIM)), full((RW_HEADS, 1, RW_HEAD_DIM)),
                  full((RW_HEADS, 1, RW_HEAD_DIM))],
        out_specs=pl.BlockSpec((1, tb, RW_WIDTH), lambda b, i: (b, i, 0)),
        out_shape=jax.ShapeDtypeStruct((bsz, seq, RW_WIDTH), BF16),
        scratch_shapes=[stage, stage, stage_wide, stage, stage, stage, stage, stage,
                        pltpu.VMEM((RW_HEADS, 2 * RW_HEAD_DIM, RW_HEAD_DIM), F32)],
        compiler_params=_params("parallel", "arbitrary"),
        name="rwkv7",
    )(z3, z3, z3, zm3, z3, z3, z3, zm3, mu_r, mu_k, mu_v, mu_m, w0, wb, a0, ab, gb,
      kk_p, ka_p, rk_p, lng, lnb)


FX_CUM_BLOCK = 256
FX_HEAD_GROUP = 8


def _fox_cum_kernel(zf_ref, fb_ref, col_ref, *, seq):
    n = FX_CUM_BLOCK
    ri = lax.broadcasted_iota(jnp.int32, (n, n), 0)
    ci = lax.broadcasted_iota(jnp.int32, (n, n), 1)
    tri = jnp.where(ci <= ri, 1.0, 0.0).astype(BF16)
    carry = jnp.zeros((1, 128), F32)
    for j in range(seq // n):
        rs = slice(j * n, (j + 1) * n)
        x = zf_ref[0, rs, :] + fb_ref[...]
        lf = jnp.minimum(x, 0.0) - jnp.log(1.0 + jnp.exp(-jnp.abs(x)))
        hi = lf.astype(BF16)
        r1 = lf - hi.astype(F32)
        mid = r1.astype(BF16)
        lo = (r1 - mid.astype(F32)).astype(BF16)
        c = _dot(tri, hi) + _dot(tri, mid) + _dot(tri, lo) + carry
        col_ref[0, rs, :] = c
        carry = c[n - 1:n, :]


def _fox_cum(zm3, fb_pad):
    bsz, seq, _ = zm3.shape
    return pl.pallas_call(
        functools.partial(_fox_cum_kernel, seq=seq),
        grid=(bsz,),
        in_specs=[pl.BlockSpec((1, seq, 128), lambda b: (b, 0, MISC_F_LANE // 128)),
                  pl.BlockSpec((1, 128), lambda b: (0, 0))],
        out_specs=pl.BlockSpec((1, seq, 128), lambda b: (b, 0, 0)),
        out_shape=jax.ShapeDtypeStruct((bsz, seq, 128), F32),
        compiler_params=_params("parallel"),
        name="fox_cum",
    )(zm3, fb_pad)


FX_AUG = 128
FX_BIAS_LANE = 64


def _split3(x):
    hi = x.astype(BF16)
    r1 = x - hi.astype(F32)
    mid = r1.astype(BF16)
    lo = (r1 - mid.astype(F32)).astype(BF16)
    return jnp.concatenate([hi, mid, lo], axis=1)


def _foxt_kernel(q_ref, k_ref, v_ref, cc_ref, eall_ref, selq_ref, selk_ref, eye_ref, o_ref,
                 kaug_s, vt_s, m_s, l_s, acc_s, *, tq, tk, seq):
    i = pl.program_id(1)
    scale = FX_HEAD_DIM ** -0.5
    lane = lax.broadcasted_iota(jnp.int32, (1, FX_HEADS * FX_AUG), 1) % FX_AUG
    const_k = jnp.where((lane >= FX_BIAS_LANE) & (lane < FX_BIAS_LANE + 3), 1.0, 0.0)
    const_q = jnp.where((lane >= FX_BIAS_LANE + 3) & (lane < FX_BIAS_LANE + 6), -1.0, 0.0)
    hs_all = range(FX_HEADS)

    @pl.when(i == 0)
    def _():
        for j in range(seq // tk):
            rs = slice(j * tk, (j + 1) * tk)
            ck3 = _split3(cc_ref[0, rs, :])
            kaug_s[j] = (_dot(k_ref[0, rs, :], eall_ref[...]) + _dot(ck3, selk_ref[...]) + const_k).astype(BF16)
            vt_s[j] = _dg(eye_ref[...], v_ref[0, rs, :], NT_DIMS).astype(BF16)

    q0 = pl.multiple_of(i * tq, tq)
    cq3 = _split3(cc_ref[0, pl.ds(q0, tq), :])
    qaug = (_dot(q_ref[0] * scale, eall_ref[...]) + _dot(cq3, selq_ref[...]) + const_q).astype(BF16)
    qs = [qaug[:, h * FX_AUG:(h + 1) * FX_AUG] for h in hs_all]
    m_s[...] = jnp.full(m_s.shape, -1e30, F32)
    l_s[...] = jnp.zeros(l_s.shape, F32)
    acc_s[...] = jnp.zeros(acc_s.shape, F32)
    n_kv = (i * tq + tq + tk - 1) // tk
    key_idx = lax.broadcasted_iota(jnp.int32, (tk, tq), 0)
    qry_idx = i * tq + lax.broadcasted_iota(jnp.int32, (tk, tq), 1)

    def block(j, masked):
        kblk = kaug_s[j]
        vblk = vt_s[j]
        for g0 in range(0, FX_HEADS, FX_HEAD_GROUP):
            hs = range(g0, g0 + FX_HEAD_GROUP)
            st = {h: _dg(kblk[:, h * FX_AUG:(h + 1) * FX_AUG], qs[h], NT_DIMS) for h in hs}
            if masked:
                keep = key_idx + j * tk <= qry_idx
                st = {h: jnp.where(keep, st[h], -1e30) for h in hs}
            m_prev = {h: m_s[h:h + 1, :] for h in hs}
            m_new = {h: jnp.maximum(m_prev[h], jnp.max(st[h], axis=0, keepdims=True)) for h in hs}
            pt = {h: jnp.exp(st[h] - m_new[h]) for h in hs}
            alphas = {h: jnp.exp(m_prev[h] - m_new[h]) for h in hs}
            pv = {h: _dot(vblk[h * FX_HEAD_DIM:(h + 1) * FX_HEAD_DIM, :], pt[h].astype(BF16)) for h in hs}
            for h in hs:
                l_s[h:h + 1, :] = alphas[h] * l_s[h:h + 1, :] + jnp.sum(pt[h], axis=0, keepdims=True)
                acc_s[h] = alphas[h] * acc_s[h] + pv[h]
                m_s[h:h + 1, :] = m_new[h]

    def visible_body(j, carry):
        block(j, masked=False)
        return carry

    lax.fori_loop(0, n_kv - 1, visible_body, 0)
    block(n_kv - 1, masked=True)
    out_t = jnp.concatenate([acc_s[h] / l_s[h:h + 1, :] for h in hs_all], axis=0)
    o_ref[0] = out_t.T.astype(o_ref.dtype)


def _fox_tables():
    col = jnp.arange(FX_HEADS * FX_AUG)[None, :]
    head, d = col // FX_AUG, col % FX_AUG
    c = jnp.arange(FX_WIDTH)[:, None]
    e_all = ((c == head * FX_HEAD_DIM + d) & (d < FX_HEAD_DIM)).astype(BF16)
    r = jnp.arange(3 * 128)[:, None]
    piece, src_lane = r // 128, r % 128
    from_head = src_lane == FX_F_LOCAL + head
    selq = (from_head & (d == FX_BIAS_LANE + piece)).astype(BF16)
    selk = (from_head & (d == FX_BIAS_LANE + 3 + piece)).astype(BF16)
    return e_all, selq, selk, jnp.eye(FX_WIDTH, dtype=BF16)


def _foxt(zfx3, cum_col, tables, tq=256, tk=512):
    bsz, seq, _ = zfx3.shape
    assert tk % tq == 0 and seq % tk == 0

    def full(a):
        return pl.BlockSpec(a.shape, lambda b, i: (0,) * a.ndim)

    return pl.pallas_call(
        functools.partial(_foxt_kernel, tq=tq, tk=tk, seq=seq),
        grid=(bsz, seq // tq),
        in_specs=[pl.BlockSpec((1, tq, FX_WIDTH), lambda b, i: (b, i, 0)),
                  pl.BlockSpec((1, seq, FX_WIDTH), lambda b, i: (b, 0, 1)),
                  pl.BlockSpec((1, seq, FX_WIDTH), lambda b, i: (b, 0, 2)),
                  pl.BlockSpec((1, seq, 128), lambda b, i: (b, 0, 0))] + [full(t) for t in tables],
        out_specs=pl.BlockSpec((1, tq, FX_WIDTH), lambda b, i: (b, i, 0)),
        out_shape=jax.ShapeDtypeStruct((bsz, seq, FX_WIDTH), BF16),
        scratch_shapes=[pltpu.VMEM((seq // tk, tk, FX_HEADS * FX_AUG), BF16),
                        pltpu.VMEM((seq // tk, FX_WIDTH, tk), BF16),
                        pltpu.VMEM((FX_HEADS, tq), F32), pltpu.VMEM((FX_HEADS, tq), F32),
                        pltpu.VMEM((FX_HEADS, FX_HEAD_DIM, tq), F32)],
        compiler_params=_params("parallel", "arbitrary"),
        name="fox_attn",
    )(zfx3, zfx3, zfx3, cum_col, *tables)


def _merge_kernel(x_ref, wg0_ref, wg1_ref, wg2_ref, b0_ref, b1_ref, b2_ref,
                  ya_ref, yb_ref, yc_ref, pa_ref, pb_ref, pc_ref, o_ref):
    x = x_ref[...]

    def gate(w_ref, b_ref):
        return _sigmoid(_dg(x, w_ref[...], NT_DIMS) + b_ref[...])

    acc = gate(wg0_ref, b0_ref) * _dot(ya_ref[...], pa_ref[...])
    acc = acc + gate(wg1_ref, b1_ref) * _dot(yb_ref[...], pb_ref[...])
    acc = acc + gate(wg2_ref, b2_ref) * _dot(yc_ref[...], pc_ref[...])
    o_ref[...] = acc.astype(o_ref.dtype)


def _merge(x16, w_gate, gate_b, ya, yb, yc, pa, pb, pc, layer, tm=1024, tn=512):
    m = ya.shape[0]
    per = D_MODEL // tn

    def yspec(width):
        return pl.BlockSpec((tm, width), lambda i, j: (i, 0))

    def wspec(width):
        return pl.BlockSpec((None, width, tn), lambda i, j: (layer, 0, j))

    def gspec(k):
        return pl.BlockSpec((None, tn, D_MODEL), lambda i, j: (layer, k * per + j, 0))

    def bspec(k):
        return pl.BlockSpec((1, tn), lambda i, j: (0, k * per + j))

    return pl.pallas_call(
        _merge_kernel,
        grid=(m // tm, D_MODEL // tn),
        in_specs=[yspec(D_MODEL), gspec(0), gspec(1), gspec(2), bspec(0), bspec(1), bspec(2),
                  yspec(GM_WIDTH), yspec(RW_WIDTH), yspec(FX_WIDTH),
                  wspec(GM_WIDTH), wspec(RW_WIDTH), wspec(FX_WIDTH)],
        out_specs=pl.BlockSpec((tm, tn), lambda i, j: (i, j)),
        out_shape=jax.ShapeDtypeStruct((m, D_MODEL), BF16),
        compiler_params=_params("parallel", "parallel"),
        name="merge",
    )(x16, w_gate, w_gate, w_gate, gate_b, gate_b, gate_b, ya, yb, yc, pa, pb, pc)


def _out_ln_kernel(m_ref, w_ref, x_ref, g_ref, b_ref, o32_ref, o16_ref):
    h = DEEPNORM_ALPHA * x_ref[...] + _dot(m_ref[...], w_ref[...])
    y = _layer_norm(h, g_ref[...], b_ref[...], LN_EPS)
    o32_ref[...] = y
    o16_ref[...] = y.astype(BF16)


def _out_ln(merged, w_out, layer, x, g, b, tm=512):
    m = x.shape[0]
    row = pl.BlockSpec((tm, D_MODEL), lambda i: (i, 0))
    vec = pl.BlockSpec((1, D_MODEL), lambda i: (0, 0))
    return pl.pallas_call(
        _out_ln_kernel,
        grid=(m // tm,),
        in_specs=[row, pl.BlockSpec((None, D_MODEL, D_MODEL), lambda i: (layer, 0, 0)), row, vec, vec],
        out_specs=[row, row],
        out_shape=[jax.ShapeDtypeStruct((m, D_MODEL), F32), jax.ShapeDtypeStruct((m, D_MODEL), BF16)],
        compiler_params=_params("parallel"),
        name="out_ln",
    )(merged, w_out, x, g, b)


def _router_kernel(x_ref, w_ref, b_ref, o_ref):
    logits = jnp.dot(x_ref[...], w_ref[...], preferred_element_type=F32,
                     precision=lax.Precision.HIGHEST) + b_ref[...]
    lane = lax.broadcasted_iota(jnp.int32, logits.shape, 1)
    logits = jnp.where(lane < MOE_EXPERTS, logits, -1e30)
    m1 = jnp.max(logits, axis=-1, keepdims=True)
    i1 = jnp.min(jnp.where(logits == m1, lane, 128), axis=-1, keepdims=True)
    rest = jnp.where(lane == i1, -1e30, logits)
    m2 = jnp.max(rest, axis=-1, keepdims=True)
    i2 = jnp.min(jnp.where(rest == m2, lane, 128), axis=-1, keepdims=True)
    e2 = jnp.exp(m2 - m1)
    w1 = 1.0 / (1.0 + e2)
    o_ref[...] = jnp.where(lane == i1, w1, 0.0) + jnp.where(lane == i2, e2 * w1, 0.0)


def _router(x1, w_pad, b_pad, tm=1024):
    m = x1.shape[0]
    return pl.pallas_call(
        _router_kernel,
        grid=(m // tm,),
        in_specs=[pl.BlockSpec((tm, D_MODEL), lambda i: (i, 0)),
                  pl.BlockSpec((D_MODEL, 128), lambda i: (0, 0)),
                  pl.BlockSpec((1, 128), lambda i: (0, 0))],
        out_specs=pl.BlockSpec((tm, 128), lambda i: (i, 0)),
        out_shape=jax.ShapeDtypeStruct((m, 128), F32),
        compiler_params=_params("parallel"),
        name="router",
    )(x1, w_pad, b_pad)


def _ffn_kernel(x_ref, c_ref, w1_ref, w3_ref, w2_ref, o_ref):
    @pl.when((pl.program_id(1) == 0) & (pl.program_id(2) == 0))
    def _():
        o_ref[...] = jnp.zeros_like(o_ref)

    x = x_ref[...]
    g = _dot(x, w1_ref[0])
    u = _dot(x, w3_ref[0])
    h = (g * _sigmoid(g)) * u * c_ref[0]
    o_ref[...] += _dot(h.astype(BF16), w2_ref[0])


def _ffn(xb, comb, w1, w3, w2, first, n_e, tm=1024, tf=512):
    m = xb.shape[0]
    f = w1.shape[-1]
    return pl.pallas_call(
        _ffn_kernel,
        grid=(m // tm, n_e, f // tf),
        in_specs=[pl.BlockSpec((tm, D_MODEL), lambda i, e, j: (i, 0)),
                  pl.BlockSpec((1, tm, 1), lambda i, e, j: (e, i, 0)),
                  pl.BlockSpec((1, D_MODEL, tf), lambda i, e, j: (first + e, 0, j)),
                  pl.BlockSpec((1, D_MODEL, tf), lambda i, e, j: (first + e, 0, j)),
                  pl.BlockSpec((1, tf, D_MODEL), lambda i, e, j: (first + e, j, 0))],
        out_specs=pl.BlockSpec((tm, D_MODEL), lambda i, e, j: (i, 0)),
        out_shape=jax.ShapeDtypeStruct((m, D_MODEL), F32),
        compiler_params=_params("parallel", "arbitrary", "arbitrary"),
        name="swiglu",
    )(xb, comb, w1, w3, w2)


MOE_BLK = 288
MOE_SUB = 1024
MOE_NSUB = 1
MOE_PASS = 4


def _moe_kernel(ntot_ref, bsub_ref, bloc_ref, x_ref, comb_ref, rank_ref, rankt_ref, w1_ref, w3_ref, w2_ref,
                o_ref, xs_s, acc_s, *, n_f, col_split):
    i, e, g, f = pl.program_id(0), pl.program_id(1), pl.program_id(2), pl.program_id(3)
    row = i * MOE_EXPERTS + e
    nb = jnp.clip(ntot_ref[row] - g * MOE_PASS, 0, MOE_PASS)

    @pl.when((e == 0) & (g == 0) & (f == 0))
    def _():
        o_ref[...] = jnp.zeros_like(o_ref)

    def sub_rows(b):
        sub = bsub_ref[row, g * MOE_PASS + b]
        return sub, pl.ds(pl.multiple_of(sub * MOE_SUB, MOE_SUB), MOE_SUB)

    @pl.when(f == 0)
    def _():
        slot = lax.broadcasted_iota(jnp.int32, (MOE_BLK, MOE_SUB), 0)

        def gather(b, carry):
            sub, rows = sub_rows(b)
            first = bloc_ref[row, g * MOE_PASS + b] * MOE_BLK
            rank_row = rankt_ref[sub, pl.ds(e, 1), :]
            onehot = jnp.where(rank_row == slot + first, 1.0, 0.0).astype(BF16)
            xs_s[b] = _dot(onehot, x_ref[rows, :]).astype(BF16)
            acc_s[b] = jnp.zeros(acc_s.shape[1:], F32)
            return carry

        lax.fori_loop(0, nb, gather, 0)

    def compute(b, carry):
        xb = xs_s[b]
        gate = _dot(xb, w1_ref[0])
        up = _dot(xb, w3_ref[0])
        h = (gate * _sigmoid(gate)) * up
        acc_s[b] += _dot(h.astype(BF16), w2_ref[0])
        return carry

    lax.fori_loop(0, nb, compute, 0)

    @pl.when(f == n_f - 1)
    def _():
        lane = lax.broadcasted_iota(jnp.int32, (MOE_SUB, 128), 1)
        sel = lane == e
        slot = lax.broadcasted_iota(jnp.int32, (MOE_SUB, MOE_BLK), 1)
        cw = D_MODEL // col_split

        def scatter(b, carry):
            _, rows = sub_rows(b)
            first = bloc_ref[row, g * MOE_PASS + b] * MOE_BLK
            c_col = jnp.sum(jnp.where(sel, comb_ref[rows, :], 0.0), axis=-1, keepdims=True)
            r_col = jnp.sum(jnp.where(sel, rank_ref[rows, :], 0), axis=-1, keepdims=True)
            onehot_t = jnp.where(r_col == slot + first, 1.0, 0.0).astype(BF16)
            for s in range(col_split):
                cs = slice(s * cw, (s + 1) * cw)
                o_ref[rows, cs] += _dot(onehot_t, acc_s[b, :, cs].astype(BF16)) * c_col
            return carry

        lax.fori_loop(0, nb, scatter, 0)


def _moe(xb, comb, plan, w1, w3, w2, layer, tf=1024):
    rank, rank_t, ntot, bsub, bloc = plan
    m = xb.shape[0]
    tm = MOE_SUB * MOE_NSUB
    _, n_e, _, f = w1.shape
    n_f = f // tf
    blocks_max = MOE_NSUB * pl.cdiv(MOE_SUB, MOE_BLK)
    n_pass = pl.cdiv(blocks_max, MOE_PASS)
    once = pl.Buffered(1)

    def wmap(sel):
        def index_map(i, e, g, j, ntot_ref, bsub_ref, bloc_ref):
            live = (g == 0) | (ntot_ref[i * MOE_EXPERTS + e] > g * MOE_PASS)
            jj = jnp.where(live, j, n_f - 1)
            return (layer, e, 0, jj) if sel == 0 else (layer, e, jj, 0)
        return index_map

    def tile(i, e, g, j, *_):
        return (i, 0)

    grid_spec = pltpu.PrefetchScalarGridSpec(
        num_scalar_prefetch=3,
        grid=(m // tm, n_e, n_pass, n_f),
        in_specs=[pl.BlockSpec((tm, D_MODEL), tile, pipeline_mode=once),
                  pl.BlockSpec((tm, 128), tile, pipeline_mode=once),
                  pl.BlockSpec((tm, 128), tile, pipeline_mode=once),
                  pl.BlockSpec((MOE_NSUB, 128, MOE_SUB), lambda i, e, g, j, *_: (i, 0, 0), pipeline_mode=once),
                  pl.BlockSpec((None, 1, D_MODEL, tf), wmap(0)),
                  pl.BlockSpec((None, 1, D_MODEL, tf), wmap(0)),
                  pl.BlockSpec((None, 1, tf, D_MODEL), wmap(1))],
        out_specs=pl.BlockSpec((tm, D_MODEL), tile, pipeline_mode=once),
        scratch_shapes=[pltpu.VMEM((MOE_PASS, MOE_BLK, D_MODEL), BF16),
                        pltpu.VMEM((MOE_PASS, MOE_BLK, D_MODEL), F32)],
    )
    return pl.pallas_call(
        functools.partial(_moe_kernel, n_f=n_f, col_split=2),
        grid_spec=grid_spec,
        out_shape=jax.ShapeDtypeStruct((m, D_MODEL), F32),
        compiler_params=_params("parallel", "arbitrary", "arbitrary", "arbitrary"),
        name="moe",
    )(ntot, bsub, bloc, xb, comb, rank, rank_t, w1, w3, w2)


def _moe_plan(comb):
    m = comb.shape[0]
    n_sub = m // MOE_SUB
    routed = (comb > 0.0).reshape(n_sub, MOE_SUB, 128).astype(jnp.int32)
    rank = jnp.cumsum(routed, axis=1) - routed
    rank = jnp.where(routed > 0, rank, -1)
    count = jnp.sum(routed, axis=1)[:, :MOE_EXPERTS]
    nblk = ((count + MOE_BLK - 1) // MOE_BLK).reshape(n_sub // MOE_NSUB, MOE_NSUB, MOE_EXPERTS)
    ends = jnp.swapaxes(jnp.cumsum(nblk, axis=1), 1, 2)
    starts = ends - jnp.swapaxes(nblk, 1, 2)
    ntot = ends[:, :, -1].reshape(-1)
    blocks_max = MOE_NSUB * pl.cdiv(MOE_SUB, MOE_BLK)
    bidx = jnp.arange(blocks_max)[None, None, :, None]
    bsub = jnp.sum((bidx >= ends[:, :, None, :]).astype(jnp.int32), axis=-1)
    bsub = jnp.minimum(bsub, MOE_NSUB - 1)
    sub_onehot = (bsub[..., None] == jnp.arange(MOE_NSUB)).astype(jnp.int32)
    bloc = bidx[..., 0] - jnp.sum(sub_onehot * starts[:, :, None, :], axis=-1)
    shape2 = (-1, blocks_max)
    return (rank.reshape(m, 128), jnp.swapaxes(rank, 1, 2), ntot.astype(jnp.int32),
            bsub.reshape(shape2).astype(jnp.int32), bloc.reshape(shape2).astype(jnp.int32))


def _ple_ln_kernel(xb_ref, wg_ref, bg_ref, p_ref, wp_ref, x1_ref, ff_ref, g_ref, b_ref, o32_ref, o16_ref):
    gate = _sigmoid(_dot(xb_ref[...], wg_ref[...]) + bg_ref[...])
    ple = gate * _dot(p_ref[...].astype(BF16), wp_ref[...])
    h = DEEPNORM_ALPHA * x1_ref[...] + ff_ref[...] + ple
    y = _layer_norm(h, g_ref[...], b_ref[...], LN_EPS)
    o32_ref[...] = y
    o16_ref[...] = y.astype(BF16)


def _ple_ln(x1b, wg, bg, p, wp, layer, x1, ff, g, b, tm=512):
    m = x1.shape[0]
    row = pl.BlockSpec((tm, D_MODEL), lambda i: (i, 0))
    vec = pl.BlockSpec((1, D_MODEL), lambda i: (0, 0))
    once = pl.Buffered(1)
    return pl.pallas_call(
        _ple_ln_kernel,
        grid=(m // tm,),
        in_specs=[row,
                  pl.BlockSpec((None, D_MODEL, D_MODEL), lambda i: (layer, 0, 0), pipeline_mode=once),
                  vec,
                  pl.BlockSpec((None, tm, PLE_DIM), lambda i: (layer, i, 0)),
                  pl.BlockSpec((None, PLE_DIM, D_MODEL), lambda i: (layer, 0, 0), pipeline_mode=once),
                  row, row, vec, vec],
        out_specs=[row, row],
        out_shape=[jax.ShapeDtypeStruct((m, D_MODEL), F32), jax.ShapeDtypeStruct((m, D_MODEL), BF16)],
        compiler_params=_params("parallel"),
        name="ple_ln",
    )(x1b, wg, bg, p, wp, x1, ff, g, b)


def _split_w_in(w_in):
    w_t = jnp.swapaxes(w_in, 1, 2)
    n_l, _, d = w_t.shape
    lora = w_t[:, COL_LORA:COL_FX]
    f = w_t[:, COL_F:COL_GATE]
    pad = jnp.zeros((n_l, MISC_W - lora.shape[1] - f.shape[1], d), w_t.dtype)
    misc = jnp.concatenate([lora, f, pad], axis=1)
    return (w_t[:, :COL_LORA].astype(BF16), misc.astype(BF16),
            w_t[:, COL_FX:COL_F].astype(BF16), w_t[:, COL_GATE:].astype(BF16))


def _pad_rows(w, lo, total):
    n_l, r, n = w.shape
    return jnp.concatenate([jnp.zeros((n_l, lo, n), w.dtype), w,
                            jnp.zeros((n_l, total - lo - r, n), w.dtype)], axis=1)


def kernel(x, p, w_in, gate_b, gm_ln_g, gm_ln_b, gm_ws, gm_bs, rw_mu, rw_w0, rw_wb, rw_a0, rw_ab, rw_gb, rw_kk, rw_ka, rw_rk, rw_lnx_g, rw_lnx_b, fx_fb, proj_a, proj_b, proj_c, w_out, ln1_g, ln1_b, ffn_w1, ffn_w3, ffn_w2, moe_router, moe_router_b, moe_w1, moe_w3, moe_w2, ple_gate_w, ple_gate_b, ple_proj, ln2_g, ln2_b):
    bsz, seq, d = x.shape
    m = bsz * seq
    n_l = w_in.shape[0]

    w_ar, w_misc, w_fx, w_gate = _split_w_in(w_in)
    mu_r = rw_mu[:, None, 0:RW_WIDTH]
    mu_k = rw_mu[:, None, RW_WIDTH:2 * RW_WIDTH]
    mu_v = rw_mu[:, None, 2 * RW_WIDTH:3 * RW_WIDTH]
    o = 3 * RW_WIDTH
    zl = lambda n: jnp.zeros((n_l, n), F32)
    n_lora = RW_LORA_W + RW_LORA_A + RW_LORA_G
    mu_m = jnp.concatenate([rw_mu[:, o:o + n_lora], zl(MISC_W - n_lora)], axis=-1)[:, None, :]
    wb_pad = _pad_rows(rw_wb, 0, 128).astype(BF16)
    ab_pad = _pad_rows(rw_ab, RW_LORA_W, 128).astype(BF16)
    gb_pad = _pad_rows(rw_gb, 0, 256).astype(BF16)
    fb_pad = jnp.concatenate([zl(FX_F_LOCAL), fx_fb, zl(128 - FX_F_LOCAL - FX_HEADS)], axis=-1)[:, None, :]
    head3 = lambda t: t.reshape(n_l, RW_HEADS, 1, RW_HEAD_DIM)
    gm_bs3 = gm_bs[..., None]
    router_w = jnp.concatenate([moe_router, jnp.zeros(moe_router.shape[:2] + (128 - MOE_EXPERTS,), F32)], axis=-1)
    router_b = jnp.concatenate([moe_router_b, jnp.zeros((moe_router_b.shape[0], 128 - MOE_EXPERTS), F32)],
                               axis=-1)[:, None, :]
    bf = lambda t: t.astype(BF16)
    proj_a16, proj_b16, proj_c16, w_out16 = bf(proj_a), bf(proj_b), bf(proj_c), bf(w_out)
    ffn_w1_16, ffn_w3_16, ffn_w2_16 = bf(ffn_w1), bf(ffn_w3), bf(ffn_w2)
    moe_w1_16, moe_w3_16, moe_w2_16 = bf(moe_w1), bf(moe_w3), bf(moe_w2)
    ple_gw16, ple_pw16 = bf(ple_gate_w), bf(ple_proj)
    ones_comb = jnp.ones((1, m, 1), F32)
    p3 = p.reshape(n_l, m, PLE_DIM)
    fox_tables = _fox_tables()

    x32 = x.reshape(m, d)
    x16 = x32.astype(BF16)
    for i in range(n_l):
        z = _matmul(x16, w_ar, i, BF16, 1024, 2048, "in_proj_ar")
        zm = _matmul(x16, w_misc, i, F32, 1024, MISC_W, "in_proj_misc")
        zfx = _matmul(x16, w_fx, i, BF16, 1024, 3 * FX_WIDTH, "in_proj_fx")
        z3 = z.reshape(bsz, seq, z.shape[-1])
        zm3 = zm.reshape(bsz, seq, MISC_W)
        y_a = _gmlp(z, gm_ln_g[i][None], gm_ln_b[i][None], gm_ws[i], gm_bs3[i])
        y_b = _rwkv(z3, zm3, mu_r[i], mu_k[i], mu_v[i], mu_m[i], rw_w0[i][None], wb_pad[i], rw_a0[i][None],
                    ab_pad[i], gb_pad[i], rw_kk[i][None], rw_ka[i][None], head3(rw_rk)[i],
                    head3(rw_lnx_g)[i], head3(rw_lnx_b)[i]).reshape(m, RW_WIDTH)
        cum_col = _fox_cum(zm3, fb_pad[i])
        y_c = _foxt(zfx.reshape(bsz, seq, 3 * FX_WIDTH), cum_col, fox_tables).reshape(m, FX_WIDTH)
        merged = _merge(x16, w_gate, gate_b[i][None], y_a, y_b, y_c, proj_a16, proj_b16, proj_c16, i)
        x1, x1b = _out_ln(merged, w_out16, i, x32, ln1_g[i][None], ln1_b[i][None])
        j = i // 2
        if i % 2 == 0:
            ff = _ffn(x1b, ones_comb, ffn_w1_16, ffn_w3_16, ffn_w2_16, j, 1)
        else:
            comb = _router(x1, router_w[j], router_b[j])
            ff = _moe(x1b, comb, _moe_plan(comb), moe_w1_16, moe_w3_16, moe_w2_16, j)
        x32, x16 = _ple_ln(x1b, ple_gw16, ple_gate_b[i][None], p3, ple_pw16, i, x1, ff,
                           ln2_g[i][None], ln2_b[i][None])
    return x32.reshape(bsz, seq, d)
```

```python
import functools
import math

import jax
import jax.numpy as jnp
from jax import lax
from jax.experimental import pallas as pl
from jax.experimental.pallas import tpu as pltpu

F32 = jnp.float32
BF16 = jnp.bfloat16

D_MODEL = 2048
DEPTH = 4
PLE_DIM = 256
GM_CHUNK = 128
GM_GROUPS = 4
GM_WIDTH = 512
RW_HEADS = 16
RW_HEAD_DIM = 64
RW_WIDTH = 1024
RW_LORA_W = 64
RW_LORA_A = 64
RW_LORA_G = 160
RW_GN_EPS = 64e-5
FX_HEADS = 8
FX_HEAD_DIM = 64
FX_WIDTH = 512
MOE_EXPERTS = 8
DEEPNORM_ALPHA = (2.0 * DEPTH) ** 0.25
LN_EPS = 1e-5

COL_A = 0
COL_R = 1024
COL_LORA = 4096
COL_FX = 4384
COL_F = 5920
COL_GATE = 5928
MISC_W = 384
MISC_F_LANE = COL_FX - COL_LORA
FX_F_LOCAL = MISC_F_LANE % 128

RW_CHUNK = 64
RW_QUAD = 2
RW_TB = 256
RW_UNROLL = 2
LOGW_SCALE = math.exp(-0.5)

VMEM_LIMIT = 56 * 1024 * 1024

NT_DIMS = (((1,), (1,)), ((), ()))
TN_DIMS = (((0,), (0,)), ((), ()))


def _params(*sem):
    return pltpu.CompilerParams(dimension_semantics=sem, vmem_limit_bytes=VMEM_LIMIT)


def _dot(a, b):
    return jnp.dot(a, b, preferred_element_type=F32)


def _dg(a, b, dims):
    return lax.dot_general(a, b, dims, preferred_element_type=F32)


def _sigmoid(x):
    return 1.0 / (1.0 + jnp.exp(-x))


def _layer_norm(h, g, b, eps):
    mu = jnp.mean(h, axis=-1, keepdims=True)
    d = h - mu
    var = jnp.mean(d * d, axis=-1, keepdims=True)
    return d * lax.rsqrt(var + eps) * g + b


def _mm_kernel(x_ref, w_ref, o_ref):
    o_ref[...] = _dg(x_ref[...], w_ref[...], NT_DIMS).astype(o_ref.dtype)


def _matmul(x, w_t, layer, out_dtype, tm, tn, name):
    m, k = x.shape
    n = w_t.shape[1]
    return pl.pallas_call(
        _mm_kernel,
        grid=(m // tm, n // tn),
        in_specs=[pl.BlockSpec((tm, k), lambda i, j: (i, 0)),
                  pl.BlockSpec((None, tn, k), lambda i, j: (layer, j, 0))],
        out_specs=pl.BlockSpec((tm, tn), lambda i, j: (i, j)),
        out_shape=jax.ShapeDtypeStruct((m, n), out_dtype),
        compiler_params=_params("parallel", "parallel"),
        name=name,
    )(x, w_t)


def _gmlp_kernel(z_ref, g_ref, b_ref, ws_ref, bs_ref, o_ref, *, rows):
    z = z_ref[...].astype(F32)
    za = 0.5 * z * (1.0 + jnp.tanh(math.sqrt(2.0 / math.pi) * (z + 0.044715 * (z * z * z))))
    u = za[:, :GM_WIDTH]
    v = _layer_norm(za[:, GM_WIDTH:], g_ref[...], b_ref[...], LN_EPS).astype(BF16)
    t_idx = lax.broadcasted_iota(jnp.int32, (GM_CHUNK, GM_CHUNK), 0)
    s_idx = lax.broadcasted_iota(jnp.int32, (GM_CHUNK, GM_CHUNK), 1)
    causal = t_idx >= s_idx
    for g in range(GM_GROUPS):
        w = jnp.where(causal, ws_ref[g], 0.0).astype(BF16)
        cs = slice(g * GM_CHUNK, (g + 1) * GM_CHUNK)
        for c in range(rows // GM_CHUNK):
            rs = slice(c * GM_CHUNK, (c + 1) * GM_CHUNK)
            s = _dot(w, v[rs, cs]) + bs_ref[g]
            o_ref[rs, cs] = (u[rs, cs] * s).astype(o_ref.dtype)


def _gmlp(z, ln_g, ln_b, ws, bs, rows=512):
    m = z.shape[0]
    return pl.pallas_call(
        functools.partial(_gmlp_kernel, rows=rows),
        grid=(m // rows,),
        in_specs=[pl.BlockSpec((rows, 2 * GM_WIDTH), lambda i: (i, COL_A // (2 * GM_WIDTH))),
                  pl.BlockSpec((1, GM_WIDTH), lambda i: (0, 0)),
                  pl.BlockSpec((1, GM_WIDTH), lambda i: (0, 0)),
                  pl.BlockSpec((GM_GROUPS, GM_CHUNK, GM_CHUNK), lambda i: (0, 0, 0)),
                  pl.BlockSpec((GM_GROUPS, GM_CHUNK, 1), lambda i: (0, 0, 0))],
        out_specs=pl.BlockSpec((rows, GM_WIDTH), lambda i: (i, 0)),
        out_shape=jax.ShapeDtypeStruct((m, GM_WIDTH), BF16),
        compiler_params=_params("parallel"),
        name="gmlp",
    )(z, ln_g, ln_b, ws, bs)


def _token_shift(z_ref, p_ref, mu_ref, first):
    h = z_ref[0].astype(F32)
    last = p_ref.shape[1] - 1
    prev_row = jnp.where(first, 0.0, p_ref[0, last:last + 1, :].astype(F32))
    hp = pltpu.roll(h, 1, axis=0)
    row = lax.broadcasted_iota(jnp.int32, h.shape, 0)
    hp = jnp.where(row == 0, prev_row, hp)
    return h + (hp - h) * mu_ref[...]


def _stage(dst_ref, val, n_chunks):
    for c in range(n_chunks):
        for h in range(RW_HEADS):
            dst_ref[c, h * RW_CHUNK:(h + 1) * RW_CHUNK, 0:RW_HEAD_DIM] = (
                val[c * RW_CHUNK:(c + 1) * RW_CHUNK, h * RW_HEAD_DIM:(h + 1) * RW_HEAD_DIM])


def _rwkv_kernel(zr_ref, zk_ref, zv_ref, zm_ref, pr_ref, pk_ref, pv_ref, pm_ref,
                 mur_ref, muk_ref, muv_ref, mum_ref, w0_ref, wb_ref, a0_ref, ab_ref, gb_ref,
                 kk_ref, ka_ref, rk_ref, lng_ref, lnb_ref, o_ref,
                 r_s, k_s, v_s, kk_s, a_s, lw_s, g_s, y_s, st_s, *, n_chunks):
    first = pl.program_id(1) == 0
    qrows = RW_QUAD * RW_CHUNK
    nd = RW_HEAD_DIM

    @pl.when(first)
    def _():
        st_s[...] = jnp.zeros_like(st_s)
        v_s[...] = jnp.zeros_like(v_s)

    m = _token_shift(zm_ref, pm_ref, mum_ref, first)
    m01 = m[:, 0:128]
    lw_lin = _dot(jnp.tanh(m01).astype(BF16), wb_ref[...]) + w0_ref[...]
    _stage(lw_s, -LOGW_SCALE * _sigmoid(lw_lin), n_chunks)
    a = _sigmoid(a0_ref[...] + _dot(m01.astype(BF16), ab_ref[...]))
    _stage(a_s, a, n_chunks)
    _stage(g_s, _dot(_sigmoid(m[:, 128:384]).astype(BF16), gb_ref[...]), n_chunks)
    k = _token_shift(zk_ref, pk_ref, muk_ref, first)
    _stage(kk_s, k * kk_ref[...], n_chunks)
    _stage(k_s, k * (1.0 + (a - 1.0) * ka_ref[...]), n_chunks)
    _stage(r_s, _token_shift(zr_ref, pr_ref, mur_ref, first), n_chunks)
    _stage(v_s, _token_shift(zv_ref, pv_ref, muv_ref, first), n_chunks)

    ri = lax.broadcasted_iota(jnp.int32, (qrows, qrows), 0)
    ci = lax.broadcasted_iota(jnp.int32, (qrows, qrows), 1)
    shift = RW_CHUNK.bit_length() - 1
    same = (ri >> shift) == (ci >> shift)
    strict = same & (ci < ri)
    incl = same & (ci <= ri)
    tri_bd = jnp.where(incl, 1.0, 0.0).astype(BF16)
    ones_sel = jnp.where(
        (lax.broadcasted_iota(jnp.int32, (qrows, RW_QUAD * 128), 0) >> shift)
        == (lax.broadcasted_iota(jnp.int32, (qrows, RW_QUAD * 128), 1) >> 7), 1.0, 0.0).astype(BF16)
    e2 = jnp.where(lax.broadcasted_iota(jnp.int32, (nd, 2 * nd), 1)
                   == lax.broadcasted_iota(jnp.int32, (nd, 2 * nd), 0) + nd, 1.0, 0.0).astype(BF16)

    def chunk_parallel(c):
        quads = range(RW_HEADS // RW_QUAD)
        rows = [slice(q * qrows, (q + 1) * qrows) for q in quads]
        r = [r_s[c, rows[q], :] for q in quads]
        kq = [k_s[c, rows[q], :] for q in quads]
        vpad = [v_s[c, rows[q], :] for q in quads]
        vpb = [vpad[q].astype(BF16) for q in quads]
        kkr = [kk_s[c, rows[q], :] for q in quads]
        lw = [lw_s[c, rows[q], :] for q in quads]
        hi = [lw[q].astype(BF16) for q in quads]
        lo = [(lw[q] - hi[q].astype(F32)).astype(BF16) for q in quads]
        cum = [_dot(tri_bd, hi[q]) + _dot(tri_bd, lo[q]) for q in quads]
        tot = [jnp.concatenate([jnp.broadcast_to(cum[q][(hh + 1) * RW_CHUNK - 1:(hh + 1) * RW_CHUNK, :], (RW_CHUNK, nd))
                                for hh in range(RW_QUAD)], axis=0) for q in quads]
        pc_t = [jnp.exp(_dg(hi[q], ones_sel, TN_DIMS) + _dg(lo[q], ones_sel, TN_DIMS)) for q in quads]
        nrm = [jnp.sqrt(jnp.sum(kkr[q] * kkr[q], axis=-1, keepdims=True)) for q in quads]
        kk = [kkr[q] / jnp.maximum(nrm[q], 1e-12) for q in quads]
        bt0 = [kk[q] * a_s[c, rows[q], :] for q in quads]
        cum_ex = [cum[q] - (hi[q].astype(F32) + lo[q].astype(F32)) for q in quads]
        e_neg = [jnp.exp(-cum[q]) for q in quads]
        alb = [(kk[q] * jnp.exp(cum_ex[q])).astype(BF16) for q in quads]
        rbb = [(r[q] * jnp.exp(cum[q])).astype(BF16) for q in quads]
        kh = [(kq[q] * e_neg[q]).astype(BF16) for q in quads]
        bh = [(bt0[q] * e_neg[q]).astype(BF16) for q in quads]
        e_tail = [jnp.exp(tot[q] - cum[q]) for q in quads]
        kt = [(kq[q] * e_tail[q]).astype(BF16) for q in quads]
        btl = [(bt0[q] * e_tail[q]).astype(BF16) for q in quads]
        lmat = [jnp.where(strict, _dg(alb[q], bh[q], NT_DIMS), 0.0).astype(BF16) for q in quads]
        a_ak = [jnp.where(strict, _dg(alb[q], kh[q], NT_DIMS), 0.0).astype(BF16) for q in quads]
        a_rb = [jnp.where(incl, _dg(rbb[q], bh[q], NT_DIMS), 0.0).astype(BF16) for q in quads]
        a_rk = [jnp.where(incl, _dg(rbb[q], kh[q], NT_DIMS), 0.0).astype(BF16) for q in quads]
        x = [_dot(a_ak[q], vpb[q]) + _dot(alb[q], e2) for q in quads]
        x = [x[q] - _dot(lmat[q], x[q].astype(BF16)) for q in quads]
        pw = lmat
        step = 2
        while step < RW_CHUNK:
            pw = [_dot(pw[q], pw[q]).astype(BF16) for q in quads]
            x = [x[q] + _dot(pw[q], x[q].astype(BF16)) for q in quads]
            step *= 2
        xb = [x[q].astype(BF16) for q in quads]
        yr = [_dot(a_rk[q], vpb[q]) - _dot(a_rb[q], xb[q]) + _dot(rbb[q], e2) for q in quads]
        yrb = [yr[q].astype(BF16) for q in quads]
        return r, kq, vpad, vpb, kt, btl, pc_t, x, xb, yr, yrb

    def chunk_state(c, parts):
        r, kq, vpad, vpb, kt, btl, pc_t, x, xb, yr, yrb = parts
        quads = range(RW_HEADS // RW_QUAD)
        heads = [(q, hh) for q in quads for hh in range(RW_QUAD)]
        hsl = [slice(hh * RW_CHUNK, (hh + 1) * RW_CHUNK) for hh in range(RW_QUAD)]
        h_aug = [st_s[q * RW_QUAD + hh] for q, hh in heads]
        hb = [h.astype(BF16) for h in h_aug]
        ys = [_dot(yrb[q][hsl[hh]], hb[n]) + yr[q][hsl[hh], 0:nd] for n, (q, hh) in enumerate(heads)]
        us = [_dot(xb[q][hsl[hh]], hb[n]) + x[q][hsl[hh], 0:nd] for n, (q, hh) in enumerate(heads)]
        kv = [_dg(kt[q][hsl[hh]], vpb[q][hsl[hh], 0:nd], TN_DIMS) for q, hh in heads]
        bu = [_dg(btl[q][hsl[hh]], us[n].astype(BF16), TN_DIMS) for n, (q, hh) in enumerate(heads)]
        for n, (q, hh) in enumerate(heads):
            st_s[n, nd:2 * nd, :] = pc_t[q][:, hh * 128:hh * 128 + nd] * h_aug[n][nd:2 * nd, :] + kv[n] - bu[n]
        for n, (q, hh) in enumerate(heads):
            y = ys[n]
            mu = jnp.mean(y, axis=-1, keepdims=True)
            d = y - mu
            var = jnp.mean(d * d, axis=-1, keepdims=True)
            yn = d * lax.rsqrt(var + RW_GN_EPS) * lng_ref[n] + lnb_ref[n]
            hs = hsl[hh]
            bonus = jnp.sum(r[q][hs] * kq[q][hs] * rk_ref[n], axis=-1, keepdims=True) * vpad[q][hs, 0:nd]
            orow = slice(n * RW_CHUNK, (n + 1) * RW_CHUNK)
            y_s[c, orow, :] = (yn + bonus) * g_s[c, orow, :]

    def chunk_body(i, carry):
        cs = [i * RW_UNROLL + j for j in range(RW_UNROLL)]
        parts = [chunk_parallel(c) for c in cs]
        for c, part in zip(cs, parts):
            chunk_state(c, part)
        return carry

    lax.fori_loop(0, n_chunks // RW_UNROLL, chunk_body, 0)

    for c in range(n_chunks):
        for h in range(RW_HEADS):
            rs = slice(c * RW_CHUNK, (c + 1) * RW_CHUNK)
            ls = slice(h * RW_HEAD_DIM, (h + 1) * RW_HEAD_DIM)
            o_ref[0, rs, ls] = y_s[c, h * RW_CHUNK:(h + 1) * RW_CHUNK, :].astype(o_ref.dtype)


def _rwkv(z3, zm3, mu_r, mu_k, mu_v, mu_m, w0, wb, a0, ab, gb, kk_p, ka_p, rk_p, lng, lnb):
    bsz, seq, _ = z3.shape
    tb = RW_TB
    n_chunks = tb // RW_CHUNK
    cb = COL_R // RW_WIDTH

    def zspec(col_blk, width):
        return pl.BlockSpec((1, tb, width), lambda b, i: (b, i, col_blk))

    zrows = 8 * 4 // z3.dtype.itemsize

    def pspec(col_blk, width, rows):
        return pl.BlockSpec((1, rows, width), lambda b, i: (b, jnp.maximum(i * (tb // rows) - 1, 0), col_blk))

    def full(shape):
        return pl.BlockSpec(shape, lambda b, i: (0,) * len(shape))

    stage = pltpu.VMEM((n_chunks, RW_HEADS * RW_CHUNK, RW_HEAD_DIM), F32)
    stage_wide = pltpu.VMEM((n_chunks, RW_HEADS * RW_CHUNK, 2 * RW_HEAD_DIM), F32)
    return pl.pallas_call(
        functools.partial(_rwkv_kernel, n_chunks=n_chunks),
        grid=(bsz, seq // tb),
        in_specs=[zspec(cb, RW_WIDTH), zspec(cb + 1, RW_WIDTH), zspec(cb + 2, RW_WIDTH),
                  zspec(0, MISC_W),
                  pspec(cb, RW_WIDTH, zrows), pspec(cb + 1, RW_WIDTH, zrows), pspec(cb + 2, RW_WIDTH, zrows),
                  pspec(0, MISC_W, 8),
                  full((1, RW_WIDTH)), full((1, RW_WIDTH)), full((1, RW_WIDTH)), full((1, MISC_W)),
                  full((1, RW_WIDTH)), full((128, RW_WIDTH)), full((1, RW_WIDTH)), full((128, RW_WIDTH)),
                  full((256, RW_WIDTH)),
                  full((1, RW_WIDTH)), full((1, RW_WIDTH)),
                  full((RW_HEADS, 1, RW_HEAD_DIM)), full((RW_HEADS, 1, RW_HEAD_DIM)),
                  full((RW_HEADS, 1, RW_HEAD_DIM))],
        out_specs=pl.BlockSpec((1, tb, RW_WIDTH), lambda b, i: (b, i, 0)),
        out_shape=jax.ShapeDtypeStruct((bsz, seq, RW_WIDTH), BF16),
        scratch_shapes=[stage, stage, stage_wide, stage, stage, stage, stage, stage,
                        pltpu.VMEM((RW_HEADS, 2 * RW_HEAD_DIM, RW_HEAD_DIM), F32)],
        compiler_params=_params("parallel", "arbitrary"),
        name="rwkv7",
    )(z3, z3, z3, zm3, z3, z3, z3, zm3, mu_r, mu_k, mu_v, mu_m, w0, wb, a0, ab, gb,
      kk_p, ka_p, rk_p, lng, lnb)


FX_CUM_BLOCK = 256
FX_HEAD_GROUP = 8


def _fox_cum_kernel(zf_ref, fb_ref, col_ref, *, seq):
    n = FX_CUM_BLOCK
    ri = lax.broadcasted_iota(jnp.int32, (n, n), 0)
    ci = lax.broadcasted_iota(jnp.int32, (n, n), 1)
    tri = jnp.where(ci <= ri, 1.0, 0.0).astype(BF16)
    carry = jnp.zeros((1, 128), F32)
    for j in range(seq // n):
        rs = slice(j * n, (j + 1) * n)
        x = zf_ref[0, rs, :] + fb_ref[...]
        lf = jnp.minimum(x, 0.0) - jnp.log(1.0 + jnp.exp(-jnp.abs(x)))
        hi = lf.astype(BF16)
        r1 = lf - hi.astype(F32)
        mid = r1.astype(BF16)
        lo = (r1 - mid.astype(F32)).astype(BF16)
        c = _dot(tri, hi) + _dot(tri, mid) + _dot(tri, lo) + carry
        col_ref[0, rs, :] = c
        carry = c[n - 1:n, :]


def _fox_cum(zm3, fb_pad):
    bsz, seq, _ = zm3.shape
    return pl.pallas_call(
        functools.partial(_fox_cum_kernel, seq=seq),
        grid=(bsz,),
        in_specs=[pl.BlockSpec((1, seq, 128), lambda b: (b, 0, MISC_F_LANE // 128)),
                  pl.BlockSpec((1, 128), lambda b: (0, 0))],
        out_specs=pl.BlockSpec((1, seq, 128), lambda b: (b, 0, 0)),
        out_shape=jax.ShapeDtypeStruct((bsz, seq, 128), F32),
        compiler_params=_params("parallel"),
        name="fox_cum",
    )(zm3, fb_pad)


FX_AUG = 128
FX_BIAS_LANE = 64


def _split3(x):
    hi = x.astype(BF16)
    r1 = x - hi.astype(F32)
    mid = r1.astype(BF16)
    lo = (r1 - mid.astype(F32)).astype(BF16)
    return jnp.concatenate([hi, mid, lo], axis=1)


def _foxt_kernel(q_ref, k_ref, v_ref, cc_ref, eall_ref, selq_ref, selk_ref, eye_ref, o_ref,
                 kaug_s, vt_s, m_s, l_s, acc_s, *, tq, tk, seq):
    i = pl.program_id(1)
    scale = FX_HEAD_DIM ** -0.5
    lane = lax.broadcasted_iota(jnp.int32, (1, FX_HEADS * FX_AUG), 1) % FX_AUG
    const_k = jnp.where((lane >= FX_BIAS_LANE) & (lane < FX_BIAS_LANE + 3), 1.0, 0.0)
    const_q = jnp.where((lane >= FX_BIAS_LANE + 3) & (lane < FX_BIAS_LANE + 6), -1.0, 0.0)
    hs_all = range(FX_HEADS)

    @pl.when(i == 0)
    def _():
        for j in range(seq // tk):
            rs = slice(j * tk, (j + 1) * tk)
            ck3 = _split3(cc_ref[0, rs, :])
            kaug_s[j] = (_dot(k_ref[0, rs, :], eall_ref[...]) + _dot(ck3, selk_ref[...]) + const_k).astype(BF16)
            vt_s[j] = _dg(eye_ref[...], v_ref[0, rs, :], NT_DIMS).astype(BF16)

    q0 = pl.multiple_of(i * tq, tq)
    cq3 = _split3(cc_ref[0, pl.ds(q0, tq), :])
    qaug = (_dot(q_ref[0] * scale, eall_ref[...]) + _dot(cq3, selq_ref[...]) + const_q).astype(BF16)
    qs = [qaug[:, h * FX_AUG:(h + 1) * FX_AUG] for h in hs_all]
    m_s[...] = jnp.full(m_s.shape, -1e30, F32)
    l_s[...] = jnp.zeros(l_s.shape, F32)
    acc_s[...] = jnp.zeros(acc_s.shape, F32)
    n_kv = (i * tq + tq + tk - 1) // tk
    key_idx = lax.broadcasted_iota(jnp.int32, (tk, tq), 0)
    qry_idx = i * tq + lax.broadcasted_iota(jnp.int32, (tk, tq), 1)

    def block(j, masked):
        kblk = kaug_s[j]
        vblk = vt_s[j]
        for g0 in range(0, FX_HEADS, FX_HEAD_GROUP):
            hs = range(g0, g0 + FX_HEAD_GROUP)
            st = {h: _dg(kblk[:, h * FX_AUG:(h + 1) * FX_AUG], qs[h], NT_DIMS) for h in hs}
            if masked:
                keep = key_idx + j * tk <= qry_idx
                st = {h: jnp.where(keep, st[h], -1e30) for h in hs}
            m_prev = {h: m_s[h:h + 1, :] for h in hs}
            m_new = {h: jnp.maximum(m_prev[h], jnp.max(st[h], axis=0, keepdims=True)) for h in hs}
            pt = {h: jnp.exp(st[h] - m_new[h]) for h in hs}
            alphas = {h: jnp.exp(m_prev[h] - m_new[h]) for h in hs}
            pv = {h: _dot(vblk[h * FX_HEAD_DIM:(h + 1) * FX_HEAD_DIM, :], pt[h].astype(BF16)) for h in hs}
            for h in hs:
                l_s[h:h + 1, :] = alphas[h] * l_s[h:h + 1, :] + jnp.sum(pt[h], axis=0, keepdims=True)
                acc_s[h] = alphas[h] * acc_s[h] + pv[h]
                m_s[h:h + 1, :] = m_new[h]

    def visible_body(j, carry):
        block(j, masked=False)
        return carry

    lax.fori_loop(0, n_kv - 1, visible_body, 0)
    block(n_kv - 1, masked=True)
    out_t = jnp.concatenate([acc_s[h] / l_s[h:h + 1, :] for h in hs_all], axis=0)
    o_ref[0] = out_t.T.astype(o_ref.dtype)


def _fox_tables():
    col = jnp.arange(FX_HEADS * FX_AUG)[None, :]
    head, d = col // FX_AUG, col % FX_AUG
    c = jnp.arange(FX_WIDTH)[:, None]
    e_all = ((c == head * FX_HEAD_DIM + d) & (d < FX_HEAD_DIM)).astype(BF16)
    r = jnp.arange(3 * 128)[:, None]
    piece, src_lane = r // 128, r % 128
    from_head = src_lane == FX_F_LOCAL + head
    selq = (from_head & (d == FX_BIAS_LANE + piece)).astype(BF16)
    selk = (from_head & (d == FX_BIAS_LANE + 3 + piece)).astype(BF16)
    return e_all, selq, selk, jnp.eye(FX_WIDTH, dtype=BF16)


def _foxt(zfx3, cum_col, tables, tq=512, tk=512):
    bsz, seq, _ = zfx3.shape
    assert tk % tq == 0 and seq % tk == 0

    def full(a):
        return pl.BlockSpec(a.shape, lambda b, i: (0,) * a.ndim)

    return pl.pallas_call(
        functools.partial(_foxt_kernel, tq=tq, tk=tk, seq=seq),
        grid=(bsz, seq // tq),
        in_specs=[pl.BlockSpec((1, tq, FX_WIDTH), lambda b, i: (b, i, 0)),
                  pl.BlockSpec((1, seq, FX_WIDTH), lambda b, i: (b, 0, 1)),
                  pl.BlockSpec((1, seq, FX_WIDTH), lambda b, i: (b, 0, 2)),
                  pl.BlockSpec((1, seq, 128), lambda b, i: (b, 0, 0))] + [full(t) for t in tables],
        out_specs=pl.BlockSpec((1, tq, FX_WIDTH), lambda b, i: (b, i, 0)),
        out_shape=jax.ShapeDtypeStruct((bsz, seq, FX_WIDTH), BF16),
        scratch_shapes=[pltpu.VMEM((seq // tk, tk, FX_HEADS * FX_AUG), BF16),
                        pltpu.VMEM((seq // tk, FX_WIDTH, tk), BF16),
                        pltpu.VMEM((FX_HEADS, tq), F32), pltpu.VMEM((FX_HEADS, tq), F32),
                        pltpu.VMEM((FX_HEADS, FX_HEAD_DIM, tq), F32)],
        compiler_params=_params("parallel", "arbitrary"),
        name="fox_attn",
    )(zfx3, zfx3, zfx3, cum_col, *tables)


def _merge_kernel(x_ref, wg0_ref, wg1_ref, wg2_ref, b0_ref, b1_ref, b2_ref,
                  ya_ref, yb_ref, yc_ref, pa_ref, pb_ref, pc_ref, o_ref):
    x = x_ref[...]

    def gate(w_ref, b_ref):
        return _sigmoid(_dg(x, w_ref[...], NT_DIMS) + b_ref[...])

    acc = gate(wg0_ref, b0_ref) * _dot(ya_ref[...], pa_ref[...])
    acc = acc + gate(wg1_ref, b1_ref) * _dot(yb_ref[...], pb_ref[...])
    acc = acc + gate(wg2_ref, b2_ref) * _dot(yc_ref[...], pc_ref[...])
    o_ref[...] = acc.astype(o_ref.dtype)


def _merge(x16, w_gate, gate_b, ya, yb, yc, pa, pb, pc, layer, tm=1024, tn=512):
    m = ya.shape[0]
    per = D_MODEL // tn

    def yspec(width):
        return pl.BlockSpec((tm, width), lambda i, j: (i, 0))

    def wspec(width):
        return pl.BlockSpec((None, width, tn), lambda i, j: (layer, 0, j))

    def gspec(k):
        return pl.BlockSpec((None, tn, D_MODEL), lambda i, j: (layer, k * per + j, 0))

    def bspec(k):
        return pl.BlockSpec((1, tn), lambda i, j: (0, k * per + j))

    return pl.pallas_call(
        _merge_kernel,
        grid=(m // tm, D_MODEL // tn),
        in_specs=[yspec(D_MODEL), gspec(0), gspec(1), gspec(2), bspec(0), bspec(1), bspec(2),
                  yspec(GM_WIDTH), yspec(RW_WIDTH), yspec(FX_WIDTH),
                  wspec(GM_WIDTH), wspec(RW_WIDTH), wspec(FX_WIDTH)],
        out_specs=pl.BlockSpec((tm, tn), lambda i, j: (i, j)),
        out_shape=jax.ShapeDtypeStruct((m, D_MODEL), BF16),
        compiler_params=_params("parallel", "parallel"),
        name="merge",
    )(x16, w_gate, w_gate, w_gate, gate_b, gate_b, gate_b, ya, yb, yc, pa, pb, pc)


def _out_ln_kernel(m_ref, w_ref, x_ref, g_ref, b_ref, o32_ref, o16_ref):
    h = DEEPNORM_ALPHA * x_ref[...] + _dot(m_ref[...], w_ref[...])
    y = _layer_norm(h, g_ref[...], b_ref[...], LN_EPS)
    o32_ref[...] = y
    o16_ref[...] = y.astype(BF16)


def _out_ln(merged, w_out, layer, x, g, b, tm=512):
    m = x.shape[0]
    row = pl.BlockSpec((tm, D_MODEL), lambda i: (i, 0))
    vec = pl.BlockSpec((1, D_MODEL), lambda i: (0, 0))
    return pl.pallas_call(
        _out_ln_kernel,
        grid=(m // tm,),
        in_specs=[row, pl.BlockSpec((None, D_MODEL, D_MODEL), lambda i: (layer, 0, 0)), row, vec, vec],
        out_specs=[row, row],
        out_shape=[jax.ShapeDtypeStruct((m, D_MODEL), F32), jax.ShapeDtypeStruct((m, D_MODEL), BF16)],
        compiler_params=_params("parallel"),
        name="out_ln",
    )(merged, w_out, x, g, b)


def _router_kernel(x_ref, w_ref, b_ref, o_ref):
    logits = jnp.dot(x_ref[...], w_ref[...], preferred_element_type=F32,
                     precision=lax.Precision.HIGHEST) + b_ref[...]
    lane = lax.broadcasted_iota(jnp.int32, logits.shape, 1)
    logits = jnp.where(lane < MOE_EXPERTS, logits, -1e30)
    m1 = jnp.max(logits, axis=-1, keepdims=True)
    i1 = jnp.min(jnp.where(logits == m1, lane, 128), axis=-1, keepdims=True)
    rest = jnp.where(lane == i1, -1e30, logits)
    m2 = jnp.max(rest, axis=-1, keepdims=True)
    i2 = jnp.min(jnp.where(rest == m2, lane, 128), axis=-1, keepdims=True)
    e2 = jnp.exp(m2 - m1)
    w1 = 1.0 / (1.0 + e2)
    o_ref[...] = jnp.where(lane == i1, w1, 0.0) + jnp.where(lane == i2, e2 * w1, 0.0)


def _router(x1, w_pad, b_pad, tm=1024):
    m = x1.shape[0]
    return pl.pallas_call(
        _router_kernel,
        grid=(m // tm,),
        in_specs=[pl.BlockSpec((tm, D_MODEL), lambda i: (i, 0)),
                  pl.BlockSpec((D_MODEL, 128), lambda i: (0, 0)),
                  pl.BlockSpec((1, 128), lambda i: (0, 0))],
        out_specs=pl.BlockSpec((tm, 128), lambda i: (i, 0)),
        out_shape=jax.ShapeDtypeStruct((m, 128), F32),
        compiler_params=_params("parallel"),
        name="router",
    )(x1, w_pad, b_pad)


def _ffn_kernel(x_ref, c_ref, w1_ref, w3_ref, w2_ref, o_ref):
    @pl.when((pl.program_id(1) == 0) & (pl.program_id(2) == 0))
    def _():
        o_ref[...] = jnp.zeros_like(o_ref)

    x = x_ref[...]
    g = _dot(x, w1_ref[0])
    u = _dot(x, w3_ref[0])
    h = (g * _sigmoid(g)) * u * c_ref[0]
    o_ref[...] += _dot(h.astype(BF16), w2_ref[0])


def _ffn(xb, comb, w1, w3, w2, first, n_e, tm=1024, tf=512):
    m = xb.shape[0]
    f = w1.shape[-1]
    return pl.pallas_call(
        _ffn_kernel,
        grid=(m // tm, n_e, f // tf),
        in_specs=[pl.BlockSpec((tm, D_MODEL), lambda i, e, j: (i, 0)),
                  pl.BlockSpec((1, tm, 1), lambda i, e, j: (e, i, 0)),
                  pl.BlockSpec((1, D_MODEL, tf), lambda i, e, j: (first + e, 0, j)),
                  pl.BlockSpec((1, D_MODEL, tf), lambda i, e, j: (first + e, 0, j)),
                  pl.BlockSpec((1, tf, D_MODEL), lambda i, e, j: (first + e, j, 0))],
        out_specs=pl.BlockSpec((tm, D_MODEL), lambda i, e, j: (i, 0)),
        out_shape=jax.ShapeDtypeStruct((m, D_MODEL), F32),
        compiler_params=_params("parallel", "arbitrary", "arbitrary"),
        name="swiglu",
    )(xb, comb, w1, w3, w2)


MOE_BLK = 288
MOE_SUB = 1024
MOE_NSUB = 1
MOE_PASS = 4


def _moe_kernel(ntot_ref, bsub_ref, bloc_ref, x_ref, comb_ref, rank_ref, rankt_ref, w1_ref, w3_ref, w2_ref,
                o_ref, xs_s, acc_s, *, n_f, col_split):
    i, e, g, f = pl.program_id(0), pl.program_id(1), pl.program_id(2), pl.program_id(3)
    row = i * MOE_EXPERTS + e
    nb = jnp.clip(ntot_ref[row] - g * MOE_PASS, 0, MOE_PASS)

    @pl.when((e == 0) & (g == 0) & (f == 0))
    def _():
        o_ref[...] = jnp.zeros_like(o_ref)

    def sub_rows(b):
        sub = bsub_ref[row, g * MOE_PASS + b]
        return sub, pl.ds(pl.multiple_of(sub * MOE_SUB, MOE_SUB), MOE_SUB)

    @pl.when(f == 0)
    def _():
        slot = lax.broadcasted_iota(jnp.int32, (MOE_BLK, MOE_SUB), 0)

        def gather(b, carry):
            sub, rows = sub_rows(b)
            first = bloc_ref[row, g * MOE_PASS + b] * MOE_BLK
            rank_row = rankt_ref[sub, pl.ds(e, 1), :]
            onehot = jnp.where(rank_row == slot + first, 1.0, 0.0).astype(BF16)
            xs_s[b] = _dot(onehot, x_ref[rows, :]).astype(BF16)
            acc_s[b] = jnp.zeros(acc_s.shape[1:], F32)
            return carry

        lax.fori_loop(0, nb, gather, 0)

    def compute(b, carry):
        xb = xs_s[b]
        gate = _dot(xb, w1_ref[0])
        up = _dot(xb, w3_ref[0])
        h = (gate * _sigmoid(gate)) * up
        acc_s[b] += _dot(h.astype(BF16), w2_ref[0])
        return carry

    lax.fori_loop(0, nb, compute, 0)

    @pl.when(f == n_f - 1)
    def _():
        lane = lax.broadcasted_iota(jnp.int32, (MOE_SUB, 128), 1)
        sel = lane == e
        slot = lax.broadcasted_iota(jnp.int32, (MOE_SUB, MOE_BLK), 1)
        cw = D_MODEL // col_split

        def scatter(b, carry):
            _, rows = sub_rows(b)
            first = bloc_ref[row, g * MOE_PASS + b] * MOE_BLK
            c_col = jnp.sum(jnp.where(sel, comb_ref[rows, :], 0.0), axis=-1, keepdims=True)
            r_col = jnp.sum(jnp.where(sel, rank_ref[rows, :], 0), axis=-1, keepdims=True)
            onehot_t = jnp.where(r_col == slot + first, 1.0, 0.0).astype(BF16)
            for s in range(col_split):
                cs = slice(s * cw, (s + 1) * cw)
                o_ref[rows, cs] += _dot(onehot_t, acc_s[b, :, cs].astype(BF16)) * c_col
            return carry

        lax.fori_loop(0, nb, scatter, 0)


def _moe(xb, comb, plan, w1, w3, w2, layer, tf=1024):
    rank, rank_t, ntot, bsub, bloc = plan
    m = xb.shape[0]
    tm = MOE_SUB * MOE_NSUB
    _, n_e, _, f = w1.shape
    n_f = f // tf
    blocks_max = MOE_NSUB * pl.cdiv(MOE_SUB, MOE_BLK)
    n_pass = pl.cdiv(blocks_max, MOE_PASS)
    once = pl.Buffered(1)

    def wmap(sel):
        def index_map(i, e, g, j, ntot_ref, bsub_ref, bloc_ref):
            live = (g == 0) | (ntot_ref[i * MOE_EXPERTS + e] > g * MOE_PASS)
            jj = jnp.where(live, j, n_f - 1)
            return (layer, e, 0, jj) if sel == 0 else (layer, e, jj, 0)
        return index_map

    def tile(i, e, g, j, *_):
        return (i, 0)

    grid_spec = pltpu.PrefetchScalarGridSpec(
        num_scalar_prefetch=3,
        grid=(m // tm, n_e, n_pass, n_f),
        in_specs=[pl.BlockSpec((tm, D_MODEL), tile, pipeline_mode=once),
                  pl.BlockSpec((tm, 128), tile, pipeline_mode=once),
                  pl.BlockSpec((tm, 128), tile, pipeline_mode=once),
                  pl.BlockSpec((MOE_NSUB, 128, MOE_SUB), lambda i, e, g, j, *_: (i, 0, 0), pipeline_mode=once),
                  pl.BlockSpec((None, 1, D_MODEL, tf), wmap(0)),
                  pl.BlockSpec((None, 1, D_MODEL, tf), wmap(0)),
                  pl.BlockSpec((None, 1, tf, D_MODEL), wmap(1))],
        out_specs=pl.BlockSpec((tm, D_MODEL), tile, pipeline_mode=once),
        scratch_shapes=[pltpu.VMEM((MOE_PASS, MOE_BLK, D_MODEL), BF16),
                        pltpu.VMEM((MOE_PASS, MOE_BLK, D_MODEL), F32)],
    )
    return pl.pallas_call(
        functools.partial(_moe_kernel, n_f=n_f, col_split=2),
        grid_spec=grid_spec,
        out_shape=jax.ShapeDtypeStruct((m, D_MODEL), F32),
        compiler_params=_params("parallel", "arbitrary", "arbitrary", "arbitrary"),
        name="moe",
    )(ntot, bsub, bloc, xb, comb, rank, rank_t, w1, w3, w2)


def _moe_plan(comb):
    m = comb.shape[0]
    n_sub = m // MOE_SUB
    routed = (comb > 0.0).reshape(n_sub, MOE_SUB, 128).astype(jnp.int32)
    rank = jnp.cumsum(routed, axis=1) - routed
    rank = jnp.where(routed > 0, rank, -1)
    count = jnp.sum(routed, axis=1)[:, :MOE_EXPERTS]
    nblk = ((count + MOE_BLK - 1) // MOE_BLK).reshape(n_sub // MOE_NSUB, MOE_NSUB, MOE_EXPERTS)
    ends = jnp.swapaxes(jnp.cumsum(nblk, axis=1), 1, 2)
    starts = ends - jnp.swapaxes(nblk, 1, 2)
    ntot = ends[:, :, -1].reshape(-1)
    blocks_max = MOE_NSUB * pl.cdiv(MOE_SUB, MOE_BLK)
    bidx = jnp.arange(blocks_max)[None, None, :, None]
    bsub = jnp.sum((bidx >= ends[:, :, None, :]).astype(jnp.int32), axis=-1)
    bsub = jnp.minimum(bsub, MOE_NSUB - 1)
    sub_onehot = (bsub[..., None] == jnp.arange(MOE_NSUB)).astype(jnp.int32)
    bloc = bidx[..., 0] - jnp.sum(sub_onehot * starts[:, :, None, :], axis=-1)
    shape2 = (-1, blocks_max)
    return (rank.reshape(m, 128), jnp.swapaxes(rank, 1, 2), ntot.astype(jnp.int32),
            bsub.reshape(shape2).astype(jnp.int32), bloc.reshape(shape2).astype(jnp.int32))


def _ple_ln_kernel(xb_ref, wg_ref, bg_ref, p_ref, wp_ref, x1_ref, ff_ref, g_ref, b_ref, o32_ref, o16_ref):
    gate = _sigmoid(_dot(xb_ref[...], wg_ref[...]) + bg_ref[...])
    ple = gate * _dot(p_ref[...].astype(BF16), wp_ref[...])
    h = DEEPNORM_ALPHA * x1_ref[...] + ff_ref[...] + ple
    y = _layer_norm(h, g_ref[...], b_ref[...], LN_EPS)
    o32_ref[...] = y
    o16_ref[...] = y.astype(BF16)


def _ple_ln(x1b, wg, bg, p, wp, layer, x1, ff, g, b, tm=512):
    m = x1.shape[0]
    row = pl.BlockSpec((tm, D_MODEL), lambda i: (i, 0))
    vec = pl.BlockSpec((1, D_MODEL), lambda i: (0, 0))
    once = pl.Buffered(1)
    return pl.pallas_call(
        _ple_ln_kernel,
        grid=(m // tm,),
        in_specs=[row,
                  pl.BlockSpec((None, D_MODEL, D_MODEL), lambda i: (layer, 0, 0), pipeline_mode=once),
                  vec,
                  pl.BlockSpec((None, tm, PLE_DIM), lambda i: (layer, i, 0)),
                  pl.BlockSpec((None, PLE_DIM, D_MODEL), lambda i: (layer, 0, 0), pipeline_mode=once),
                  row, row, vec, vec],
        out_specs=[row, row],
        out_shape=[jax.ShapeDtypeStruct((m, D_MODEL), F32), jax.ShapeDtypeStruct((m, D_MODEL), BF16)],
        compiler_params=_params("parallel"),
        name="ple_ln",
    )(x1b, wg, bg, p, wp, x1, ff, g, b)


def _split_w_in(w_in):
    w_t = jnp.swapaxes(w_in, 1, 2)
    n_l, _, d = w_t.shape
    lora = w_t[:, COL_LORA:COL_FX]
    f = w_t[:, COL_F:COL_GATE]
    pad = jnp.zeros((n_l, MISC_W - lora.shape[1] - f.shape[1], d), w_t.dtype)
    misc = jnp.concatenate([lora, f, pad], axis=1)
    return (w_t[:, :COL_LORA].astype(BF16), misc.astype(BF16),
            w_t[:, COL_FX:COL_F].astype(BF16), w_t[:, COL_GATE:].astype(BF16))


def _pad_rows(w, lo, total):
    n_l, r, n = w.shape
    return jnp.concatenate([jnp.zeros((n_l, lo, n), w.dtype), w,
                            jnp.zeros((n_l, total - lo - r, n), w.dtype)], axis=1)


def kernel(x, p, w_in, gate_b, gm_ln_g, gm_ln_b, gm_ws, gm_bs, rw_mu, rw_w0, rw_wb, rw_a0, rw_ab, rw_gb, rw_kk, rw_ka, rw_rk, rw_lnx_g, rw_lnx_b, fx_fb, proj_a, proj_b, proj_c, w_out, ln1_g, ln1_b, ffn_w1, ffn_w3, ffn_w2, moe_router, moe_router_b, moe_w1, moe_w3, moe_w2, ple_gate_w, ple_gate_b, ple_proj, ln2_g, ln2_b):
    bsz, seq, d = x.shape
    m = bsz * seq
    n_l = w_in.shape[0]

    w_ar, w_misc, w_fx, w_gate = _split_w_in(w_in)
    mu_r = rw_mu[:, None, 0:RW_WIDTH]
    mu_k = rw_mu[:, None, RW_WIDTH:2 * RW_WIDTH]
    mu_v = rw_mu[:, None, 2 * RW_WIDTH:3 * RW_WIDTH]
    o = 3 * RW_WIDTH
    zl = lambda n: jnp.zeros((n_l, n), F32)
    n_lora = RW_LORA_W + RW_LORA_A + RW_LORA_G
    mu_m = jnp.concatenate([rw_mu[:, o:o + n_lora], zl(MISC_W - n_lora)], axis=-1)[:, None, :]
    wb_pad = _pad_rows(rw_wb, 0, 128).astype(BF16)
    ab_pad = _pad_rows(rw_ab, RW_LORA_W, 128).astype(BF16)
    gb_pad = _pad_rows(rw_gb, 0, 256).astype(BF16)
    fb_pad = jnp.concatenate([zl(FX_F_LOCAL), fx_fb, zl(128 - FX_F_LOCAL - FX_HEADS)], axis=-1)[:, None, :]
    head3 = lambda t: t.reshape(n_l, RW_HEADS, 1, RW_HEAD_DIM)
    gm_bs3 = gm_bs[..., None]
    router_w = jnp.concatenate([moe_router, jnp.zeros(moe_router.shape[:2] + (128 - MOE_EXPERTS,), F32)], axis=-1)
    router_b = jnp.concatenate([moe_router_b, jnp.zeros((moe_router_b.shape[0], 128 - MOE_EXPERTS), F32)],
                               axis=-1)[:, None, :]
    bf = lambda t: t.astype(BF16)
    proj_a16, proj_b16, proj_c16, w_out16 = bf(proj_a), bf(proj_b), bf(proj_c), bf(w_out)
    ffn_w1_16, ffn_w3_16, ffn_w2_16 = bf(ffn_w1), bf(ffn_w3), bf(ffn_w2)
    moe_w1_16, moe_w3_16, moe_w2_16 = bf(moe_w1), bf(moe_w3), bf(moe_w2)
    ple_gw16, ple_pw16 = bf(ple_gate_w), bf(ple_proj)
    ones_comb = jnp.ones((1, m, 1), F32)
    p3 = p.reshape(n_l, m, PLE_DIM)
    fox_tables = _fox_tables()

    x32 = x.reshape(m, d)
    x16 = x32.astype(BF16)
    for i in range(n_l):
        z = _matmul(x16, w_ar, i, BF16, 1024, 2048, "in_proj_ar")
        zm = _matmul(x16, w_misc, i, F32, 1024, MISC_W, "in_proj_misc")
        zfx = _matmul(x16, w_fx, i, BF16, 1024, 3 * FX_WIDTH, "in_proj_fx")
        z3 = z.reshape(bsz, seq, z.shape[-1])
        zm3 = zm.reshape(bsz, seq, MISC_W)
        y_a = _gmlp(z, gm_ln_g[i][None], gm_ln_b[i][None], gm_ws[i], gm_bs3[i])
        y_b = _rwkv(z3, zm3, mu_r[i], mu_k[i], mu_v[i], mu_m[i], rw_w0[i][None], wb_pad[i], rw_a0[i][None],
                    ab_pad[i], gb_pad[i], rw_kk[i][None], rw_ka[i][None], head3(rw_rk)[i],
                    head3(rw_lnx_g)[i], head3(rw_lnx_b)[i]).reshape(m, RW_WIDTH)
        cum_col = _fox_cum(zm3, fb_pad[i])
        y_c = _foxt(zfx.reshape(bsz, seq, 3 * FX_WIDTH), cum_col, fox_tables).reshape(m, FX_WIDTH)
        merged = _merge(x16, w_gate, gate_b[i][None], y_a, y_b, y_c, proj_a16, proj_b16, proj_c16, i)
        x1, x1b = _out_ln(merged, w_out16, i, x32, ln1_g[i][None], ln1_b[i][None])
        j = i // 2
        if i % 2 == 0:
            ff = _ffn(x1b, ones_comb, ffn_w1_16, ffn_w3_16, ffn_w2_16, j, 1)
        else:
            comb = _router(x1, router_w[j], router_b[j])
            ff = _moe(x1b, comb, _moe_plan(comb), moe_w1_16, moe_w3_16, moe_w2_16, j)
        x32, x16 = _ple_ln(x1b, ple_gw16, ple_gate_b[i][None], p3, ple_pw16, i, x1, ff,
                           ln2_g[i][None], ln2_b[i][None])
    return x32.reshape(bsz, seq, d)
```

```python
import functools
import math

import jax
import jax.numpy as jnp
from jax import lax
from jax.experimental import pallas as pl
from jax.experimental.pallas import tpu as pltpu

F32 = jnp.float32
BF16 = jnp.bfloat16

D_MODEL = 2048
DEPTH = 4
PLE_DIM = 256
GM_CHUNK = 128
GM_GROUPS = 4
GM_WIDTH = 512
RW_HEADS = 16
RW_HEAD_DIM = 64
RW_WIDTH = 1024
RW_LORA_W = 64
RW_LORA_A = 64
RW_LORA_G = 160
RW_GN_EPS = 64e-5
FX_HEADS = 8
FX_HEAD_DIM = 64
FX_WIDTH = 512
MOE_EXPERTS = 8
DEEPNORM_ALPHA = (2.0 * DEPTH) ** 0.25
LN_EPS = 1e-5

COL_A = 0
COL_R = 1024
COL_LORA = 4096
COL_FX = 4384
COL_F = 5920
COL_GATE = 5928
MISC_W = 384
MISC_F_LANE = COL_FX - COL_LORA
FX_F_LOCAL = MISC_F_LANE % 128

RW_CHUNK = 64
RW_QUAD = 2
RW_TB = 256
RW_UNROLL = 4
LOGW_SCALE = math.exp(-0.5)

VMEM_LIMIT = 56 * 1024 * 1024

NT_DIMS = (((1,), (1,)), ((), ()))
TN_DIMS = (((0,), (0,)), ((), ()))


def _params(*sem):
    return pltpu.CompilerParams(dimension_semantics=sem, vmem_limit_bytes=VMEM_LIMIT)


def _dot(a, b):
    return jnp.dot(a, b, preferred_element_type=F32)


def _dg(a, b, dims):
    return lax.dot_general(a, b, dims, preferred_element_type=F32)


def _sigmoid(x):
    return 1.0 / (1.0 + jnp.exp(-x))


def _layer_norm(h, g, b, eps):
    mu = jnp.mean(h, axis=-1, keepdims=True)
    d = h - mu
    var = jnp.mean(d * d, axis=-1, keepdims=True)
    return d * lax.rsqrt(var + eps) * g + b


def _mm_kernel(x_ref, w_ref, o_ref):
    o_ref[...] = _dg(x_ref[...], w_ref[...], NT_DIMS).astype(o_ref.dtype)


def _matmul(x, w_t, layer, out_dtype, tm, tn, name):
    m, k = x.shape
    n = w_t.shape[1]
    return pl.pallas_call(
        _mm_kernel,
        grid=(m // tm, n // tn),
        in_specs=[pl.BlockSpec((tm, k), lambda i, j: (i, 0)),
                  pl.BlockSpec((None, tn, k), lambda i, j: (layer, j, 0))],
        out_specs=pl.BlockSpec((tm, tn), lambda i, j: (i, j)),
        out_shape=jax.ShapeDtypeStruct((m, n), out_dtype),
        compiler_params=_params("parallel", "parallel"),
        name=name,
    )(x, w_t)


def _gmlp_kernel(z_ref, g_ref, b_ref, ws_ref, bs_ref, o_ref, *, rows):
    z = z_ref[...].astype(F32)
    za = 0.5 * z * (1.0 + jnp.tanh(math.sqrt(2.0 / math.pi) * (z + 0.044715 * (z * z * z))))
    u = za[:, :GM_WIDTH]
    v = _layer_norm(za[:, GM_WIDTH:], g_ref[...], b_ref[...], LN_EPS).astype(BF16)
    t_idx = lax.broadcasted_iota(jnp.int32, (GM_CHUNK, GM_CHUNK), 0)
    s_idx = lax.broadcasted_iota(jnp.int32, (GM_CHUNK, GM_CHUNK), 1)
    causal = t_idx >= s_idx
    for g in range(GM_GROUPS):
        w = jnp.where(causal, ws_ref[g], 0.0).astype(BF16)
        cs = slice(g * GM_CHUNK, (g + 1) * GM_CHUNK)
        for c in range(rows // GM_CHUNK):
            rs = slice(c * GM_CHUNK, (c + 1) * GM_CHUNK)
            s = _dot(w, v[rs, cs]) + bs_ref[g]
            o_ref[rs, cs] = (u[rs, cs] * s).astype(o_ref.dtype)


def _gmlp(z, ln_g, ln_b, ws, bs, rows=512):
    m = z.shape[0]
    return pl.pallas_call(
        functools.partial(_gmlp_kernel, rows=rows),
        grid=(m // rows,),
        in_specs=[pl.BlockSpec((rows, 2 * GM_WIDTH), lambda i: (i, COL_A // (2 * GM_WIDTH))),
                  pl.BlockSpec((1, GM_WIDTH), lambda i: (0, 0)),
                  pl.BlockSpec((1, GM_WIDTH), lambda i: (0, 0)),
                  pl.BlockSpec((GM_GROUPS, GM_CHUNK, GM_CHUNK), lambda i: (0, 0, 0)),
                  pl.BlockSpec((GM_GROUPS, GM_CHUNK, 1), lambda i: (0, 0, 0))],
        out_specs=pl.BlockSpec((rows, GM_WIDTH), lambda i: (i, 0)),
        out_shape=jax.ShapeDtypeStruct((m, GM_WIDTH), BF16),
        compiler_params=_params("parallel"),
        name="gmlp",
    )(z, ln_g, ln_b, ws, bs)


def _token_shift(z_ref, p_ref, mu_ref, first):
    h = z_ref[0].astype(F32)
    last = p_ref.shape[1] - 1
    prev_row = jnp.where(first, 0.0, p_ref[0, last:last + 1, :].astype(F32))
    hp = pltpu.roll(h, 1, axis=0)
    row = lax.broadcasted_iota(jnp.int32, h.shape, 0)
    hp = jnp.where(row == 0, prev_row, hp)
    return h + (hp - h) * mu_ref[...]


def _stage(dst_ref, val, n_chunks):
    for c in range(n_chunks):
        for h in range(RW_HEADS):
            dst_ref[c, h * RW_CHUNK:(h + 1) * RW_CHUNK, 0:RW_HEAD_DIM] = (
                val[c * RW_CHUNK:(c + 1) * RW_CHUNK, h * RW_HEAD_DIM:(h + 1) * RW_HEAD_DIM])


def _rwkv_kernel(zr_ref, zk_ref, zv_ref, zm_ref, pr_ref, pk_ref, pv_ref, pm_ref,
                 mur_ref, muk_ref, muv_ref, mum_ref, w0_ref, wb_ref, a0_ref, ab_ref, gb_ref,
                 kk_ref, ka_ref, rk_ref, lng_ref, lnb_ref, o_ref,
                 r_s, k_s, v_s, kk_s, a_s, lw_s, g_s, y_s, st_s, *, n_chunks):
    first = pl.program_id(1) == 0
    qrows = RW_QUAD * RW_CHUNK
    nd = RW_HEAD_DIM

    @pl.when(first)
    def _():
        st_s[...] = jnp.zeros_like(st_s)
        v_s[...] = jnp.zeros_like(v_s)

    m = _token_shift(zm_ref, pm_ref, mum_ref, first)
    m01 = m[:, 0:128]
    lw_lin = _dot(jnp.tanh(m01).astype(BF16), wb_ref[...]) + w0_ref[...]
    _stage(lw_s, -LOGW_SCALE * _sigmoid(lw_lin), n_chunks)
    a = _sigmoid(a0_ref[...] + _dot(m01.astype(BF16), ab_ref[...]))
    _stage(a_s, a, n_chunks)
    _stage(g_s, _dot(_sigmoid(m[:, 128:384]).astype(BF16), gb_ref[...]), n_chunks)
    k = _token_shift(zk_ref, pk_ref, muk_ref, first)
    _stage(kk_s, k * kk_ref[...], n_chunks)
    _stage(k_s, k * (1.0 + (a - 1.0) * ka_ref[...]), n_chunks)
    _stage(r_s, _token_shift(zr_ref, pr_ref, mur_ref, first), n_chunks)
    _stage(v_s, _token_shift(zv_ref, pv_ref, muv_ref, first), n_chunks)

    ri = lax.broadcasted_iota(jnp.int32, (qrows, qrows), 0)
    ci = lax.broadcasted_iota(jnp.int32, (qrows, qrows), 1)
    shift = RW_CHUNK.bit_length() - 1
    same = (ri >> shift) == (ci >> shift)
    strict = same & (ci < ri)
    incl = same & (ci <= ri)
    tri_bd = jnp.where(incl, 1.0, 0.0).astype(BF16)
    ones_sel = jnp.where(
        (lax.broadcasted_iota(jnp.int32, (qrows, RW_QUAD * 128), 0) >> shift)
        == (lax.broadcasted_iota(jnp.int32, (qrows, RW_QUAD * 128), 1) >> 7), 1.0, 0.0).astype(BF16)
    e2 = jnp.where(lax.broadcasted_iota(jnp.int32, (nd, 2 * nd), 1)
                   == lax.broadcasted_iota(jnp.int32, (nd, 2 * nd), 0) + nd, 1.0, 0.0).astype(BF16)

    def chunk_parallel(c):
        quads = range(RW_HEADS // RW_QUAD)
        rows = [slice(q * qrows, (q + 1) * qrows) for q in quads]
        r = [r_s[c, rows[q], :] for q in quads]
        kq = [k_s[c, rows[q], :] for q in quads]
        vpad = [v_s[c, rows[q], :] for q in quads]
        vpb = [vpad[q].astype(BF16) for q in quads]
        kkr = [kk_s[c, rows[q], :] for q in quads]
        lw = [lw_s[c, rows[q], :] for q in quads]
        hi = [lw[q].astype(BF16) for q in quads]
        lo = [(lw[q] - hi[q].astype(F32)).astype(BF16) for q in quads]
        cum = [_dot(tri_bd, hi[q]) + _dot(tri_bd, lo[q]) for q in quads]
        tot = [jnp.concatenate([jnp.broadcast_to(cum[q][(hh + 1) * RW_CHUNK - 1:(hh + 1) * RW_CHUNK, :], (RW_CHUNK, nd))
                                for hh in range(RW_QUAD)], axis=0) for q in quads]
        pc_t = [jnp.exp(_dg(hi[q], ones_sel, TN_DIMS) + _dg(lo[q], ones_sel, TN_DIMS)) for q in quads]
        nrm = [jnp.sqrt(jnp.sum(kkr[q] * kkr[q], axis=-1, keepdims=True)) for q in quads]
        kk = [kkr[q] / jnp.maximum(nrm[q], 1e-12) for q in quads]
        bt0 = [kk[q] * a_s[c, rows[q], :] for q in quads]
        cum_ex = [cum[q] - (hi[q].astype(F32) + lo[q].astype(F32)) for q in quads]
        e_neg = [jnp.exp(-cum[q]) for q in quads]
        alb = [(kk[q] * jnp.exp(cum_ex[q])).astype(BF16) for q in quads]
        rbb = [(r[q] * jnp.exp(cum[q])).astype(BF16) for q in quads]
        kh = [(kq[q] * e_neg[q]).astype(BF16) for q in quads]
        bh = [(bt0[q] * e_neg[q]).astype(BF16) for q in quads]
        e_tail = [jnp.exp(tot[q] - cum[q]) for q in quads]
        kt = [(kq[q] * e_tail[q]).astype(BF16) for q in quads]
        btl = [(bt0[q] * e_tail[q]).astype(BF16) for q in quads]
        lmat = [jnp.where(strict, _dg(alb[q], bh[q], NT_DIMS), 0.0).astype(BF16) for q in quads]
        a_ak = [jnp.where(strict, _dg(alb[q], kh[q], NT_DIMS), 0.0).astype(BF16) for q in quads]
        a_rb = [jnp.where(incl, _dg(rbb[q], bh[q], NT_DIMS), 0.0).astype(BF16) for q in quads]
        a_rk = [jnp.where(incl, _dg(rbb[q], kh[q], NT_DIMS), 0.0).astype(BF16) for q in quads]
        x = [_dot(a_ak[q], vpb[q]) + _dot(alb[q], e2) for q in quads]
        x = [x[q] - _dot(lmat[q], x[q].astype(BF16)) for q in quads]
        pw = lmat
        step = 2
        while step < RW_CHUNK:
            pw = [_dot(pw[q], pw[q]).astype(BF16) for q in quads]
            x = [x[q] + _dot(pw[q], x[q].astype(BF16)) for q in quads]
            step *= 2
        xb = [x[q].astype(BF16) for q in quads]
        yr = [_dot(a_rk[q], vpb[q]) - _dot(a_rb[q], xb[q]) + _dot(rbb[q], e2) for q in quads]
        yrb = [yr[q].astype(BF16) for q in quads]
        return r, kq, vpad, vpb, kt, btl, pc_t, x, xb, yr, yrb

    def chunk_state(c, parts):
        r, kq, vpad, vpb, kt, btl, pc_t, x, xb, yr, yrb = parts
        quads = range(RW_HEADS // RW_QUAD)
        heads = [(q, hh) for q in quads for hh in range(RW_QUAD)]
        hsl = [slice(hh * RW_CHUNK, (hh + 1) * RW_CHUNK) for hh in range(RW_QUAD)]
        h_aug = [st_s[q * RW_QUAD + hh] for q, hh in heads]
        hb = [h.astype(BF16) for h in h_aug]
        ys = [_dot(yrb[q][hsl[hh]], hb[n]) + yr[q][hsl[hh], 0:nd] for n, (q, hh) in enumerate(heads)]
        us = [_dot(xb[q][hsl[hh]], hb[n]) + x[q][hsl[hh], 0:nd] for n, (q, hh) in enumerate(heads)]
        kv = [_dg(kt[q][hsl[hh]], vpb[q][hsl[hh], 0:nd], TN_DIMS) for q, hh in heads]
        bu = [_dg(btl[q][hsl[hh]], us[n].astype(BF16), TN_DIMS) for n, (q, hh) in enumerate(heads)]
        for n, (q, hh) in enumerate(heads):
            st_s[n, nd:2 * nd, :] = pc_t[q][:, hh * 128:hh * 128 + nd] * h_aug[n][nd:2 * nd, :] + kv[n] - bu[n]
        for n, (q, hh) in enumerate(heads):
            y = ys[n]
            mu = jnp.mean(y, axis=-1, keepdims=True)
            d = y - mu
            var = jnp.mean(d * d, axis=-1, keepdims=True)
            yn = d * lax.rsqrt(var + RW_GN_EPS) * lng_ref[n] + lnb_ref[n]
            hs = hsl[hh]
            bonus = jnp.sum(r[q][hs] * kq[q][hs] * rk_ref[n], axis=-1, keepdims=True) * vpad[q][hs, 0:nd]
            orow = slice(n * RW_CHUNK, (n + 1) * RW_CHUNK)
            y_s[c, orow, :] = (yn + bonus) * g_s[c, orow, :]

    def chunk_body(i, carry):
        cs = [i * RW_UNROLL + j for j in range(RW_UNROLL)]
        parts = [chunk_parallel(c) for c in cs]
        for c, part in zip(cs, parts):
            chunk_state(c, part)
        return carry

    lax.fori_loop(0, n_chunks // RW_UNROLL, chunk_body, 0)

    for c in range(n_chunks):
        for h in range(RW_HEADS):
            rs = slice(c * RW_CHUNK, (c + 1) * RW_CHUNK)
            ls = slice(h * RW_HEAD_DIM, (h + 1) * RW_HEAD_DIM)
            o_ref[0, rs, ls] = y_s[c, h * RW_CHUNK:(h + 1) * RW_CHUNK, :].astype(o_ref.dtype)


def _rwkv(z3, zm3, mu_r, mu_k, mu_v, mu_m, w0, wb, a0, ab, gb, kk_p, ka_p, rk_p, lng, lnb):
    bsz, seq, _ = z3.shape
    tb = RW_TB
    n_chunks = tb // RW_CHUNK
    cb = COL_R // RW_WIDTH

    def zspec(col_blk, width):
        return pl.BlockSpec((1, tb, width), lambda b, i: (b, i, col_blk))

    zrows = 8 * 4 // z3.dtype.itemsize

    def pspec(col_blk, width, rows):
        return pl.BlockSpec((1, rows, width), lambda b, i: (b, jnp.maximum(i * (tb // rows) - 1, 0), col_blk))

    def full(shape):
        return pl.BlockSpec(shape, lambda b, i: (0,) * len(shape))

    stage = pltpu.VMEM((n_chunks, RW_HEADS * RW_CHUNK, RW_HEAD_DIM), F32)
    stage_wide = pltpu.VMEM((n_chunks, RW_HEADS * RW_CHUNK, 2 * RW_HEAD_DIM), F32)
    return pl.pallas_call(
        functools.partial(_rwkv_kernel, n_chunks=n_chunks),
        grid=(bsz, seq // tb),
        in_specs=[zspec(cb, RW_WIDTH), zspec(cb + 1, RW_WIDTH), zspec(cb + 2, RW_WIDTH),
                  zspec(0, MISC_W),
                  pspec(cb, RW_WIDTH, zrows), pspec(cb + 1, RW_WIDTH, zrows), pspec(cb + 2, RW_WIDTH, zrows),
                  pspec(0, MISC_W, 8),
                  full((1, RW_WIDTH)), full((1, RW_WIDTH)), full((1, RW_WIDTH)), full((1, MISC_W)),
                  full((1, RW_WIDTH)), full((128, RW_WIDTH)), full((1, RW_WIDTH)), full((128, RW_WIDTH)),
                  full((256, RW_WIDTH)),
                  full((1, RW_WIDTH)), full((1, RW_WIDTH)),
                  full((RW_HEADS, 1, RW_HEAD_DIM)), full((RW_HEADS, 1, RW_HEAD_DIM)),
                  full((RW_HEADS, 1, RW_HEAD_DIM))],
        out_specs=pl.BlockSpec((1, tb, RW_WIDTH), lambda b, i: (b, i, 0)),
        out_shape=jax.ShapeDtypeStruct((bsz, seq, RW_WIDTH), BF16),
        scratch_shapes=[stage, stage, stage_wide, stage, stage, stage, stage, stage,
                        pltpu.VMEM((RW_HEADS, 2 * RW_HEAD_DIM, RW_HEAD_DIM), F32)],
        compiler_params=_params("parallel", "arbitrary"),
        name="rwkv7",
    )(z3, z3, z3, zm3, z3, z3, z3, zm3, mu_r, mu_k, mu_v, mu_m, w0, wb, a0, ab, gb,
      kk_p, ka_p, rk_p, lng, lnb)


FX_CUM_BLOCK = 256
FX_HEAD_GROUP = 8


def _fox_cum_kernel(zf_ref, fb_ref, col_ref, *, seq):
    n = FX_CUM_BLOCK
    ri = lax.broadcasted_iota(jnp.int32, (n, n), 0)
    ci = lax.broadcasted_iota(jnp.int32, (n, n), 1)
    tri = jnp.where(ci <= ri, 1.0, 0.0).astype(BF16)
    carry = jnp.zeros((1, 128), F32)
    for j in range(seq // n):
        rs = slice(j * n, (j + 1) * n)
        x = zf_ref[0, rs, :] + fb_ref[...]
        lf = jnp.minimum(x, 0.0) - jnp.log(1.0 + jnp.exp(-jnp.abs(x)))
        hi = lf.astype(BF16)
        r1 = lf - hi.astype(F32)
        mid = r1.astype(BF16)
        lo = (r1 - mid.astype(F32)).astype(BF16)
        c = _dot(tri, hi) + _dot(tri, mid) + _dot(tri, lo) + carry
        col_ref[0, rs, :] = c
        carry = c[n - 1:n, :]


def _fox_cum(zm3, fb_pad):
    bsz, seq, _ = zm3.shape
    return pl.pallas_call(
        functools.partial(_fox_cum_kernel, seq=seq),
        grid=(bsz,),
        in_specs=[pl.BlockSpec((1, seq, 128), lambda b: (b, 0, MISC_F_LANE // 128)),
                  pl.BlockSpec((1, 128), lambda b: (0, 0))],
        out_specs=pl.BlockSpec((1, seq, 128), lambda b: (b, 0, 0)),
        out_shape=jax.ShapeDtypeStruct((bsz, seq, 128), F32),
        compiler_params=_params("parallel"),
        name="fox_cum",
    )(zm3, fb_pad)


FX_AUG = 128
FX_BIAS_LANE = 64


def _split3(x):
    hi = x.astype(BF16)
    r1 = x - hi.astype(F32)
    mid = r1.astype(BF16)
    lo = (r1 - mid.astype(F32)).astype(BF16)
    return jnp.concatenate([hi, mid, lo], axis=1)


def _foxt_kernel(q_ref, k_ref, v_ref, cc_ref, eall_ref, selq_ref, selk_ref, eye_ref, o_ref,
                 kaug_s, vt_s, m_s, l_s, acc_s, *, tq, tk, seq):
    i = pl.program_id(1)
    scale = FX_HEAD_DIM ** -0.5
    lane = lax.broadcasted_iota(jnp.int32, (1, FX_HEADS * FX_AUG), 1) % FX_AUG
    const_k = jnp.where((lane >= FX_BIAS_LANE) & (lane < FX_BIAS_LANE + 3), 1.0, 0.0)
    const_q = jnp.where((lane >= FX_BIAS_LANE + 3) & (lane < FX_BIAS_LANE + 6), -1.0, 0.0)
    hs_all = range(FX_HEADS)

    @pl.when(i == 0)
    def _():
        for j in range(seq // tk):
            rs = slice(j * tk, (j + 1) * tk)
            ck3 = _split3(cc_ref[0, rs, :])
            kaug_s[j] = (_dot(k_ref[0, rs, :], eall_ref[...]) + _dot(ck3, selk_ref[...]) + const_k).astype(BF16)
            vt_s[j] = _dg(eye_ref[...], v_ref[0, rs, :], NT_DIMS).astype(BF16)

    q0 = pl.multiple_of(i * tq, tq)
    cq3 = _split3(cc_ref[0, pl.ds(q0, tq), :])
    qaug = (_dot(q_ref[0] * scale, eall_ref[...]) + _dot(cq3, selq_ref[...]) + const_q).astype(BF16)
    qs = [qaug[:, h * FX_AUG:(h + 1) * FX_AUG] for h in hs_all]
    m_s[...] = jnp.full(m_s.shape, -1e30, F32)
    l_s[...] = jnp.zeros(l_s.shape, F32)
    acc_s[...] = jnp.zeros(acc_s.shape, F32)
    n_kv = (i * tq + tq + tk - 1) // tk
    key_idx = lax.broadcasted_iota(jnp.int32, (tk, tq), 0)
    qry_idx = i * tq + lax.broadcasted_iota(jnp.int32, (tk, tq), 1)

    def block(j, masked):
        kblk = kaug_s[j]
        vblk = vt_s[j]
        for g0 in range(0, FX_HEADS, FX_HEAD_GROUP):
            hs = range(g0, g0 + FX_HEAD_GROUP)
            st = {h: _dg(kblk[:, h * FX_AUG:(h + 1) * FX_AUG], qs[h], NT_DIMS) for h in hs}
            if masked:
                keep = key_idx + j * tk <= qry_idx
                st = {h: jnp.where(keep, st[h], -1e30) for h in hs}
            m_prev = {h: m_s[h:h + 1, :] for h in hs}
            m_new = {h: jnp.maximum(m_prev[h], jnp.max(st[h], axis=0, keepdims=True)) for h in hs}
            pt = {h: jnp.exp(st[h] - m_new[h]) for h in hs}
            alphas = {h: jnp.exp(m_prev[h] - m_new[h]) for h in hs}
            pv = {h: _dot(vblk[h * FX_HEAD_DIM:(h + 1) * FX_HEAD_DIM, :], pt[h].astype(BF16)) for h in hs}
            for h in hs:
                l_s[h:h + 1, :] = alphas[h] * l_s[h:h + 1, :] + jnp.sum(pt[h], axis=0, keepdims=True)
                acc_s[h] = alphas[h] * acc_s[h] + pv[h]
                m_s[h:h + 1, :] = m_new[h]

    def visible_body(j, carry):
        block(j, masked=False)
        return carry

    lax.fori_loop(0, n_kv - 1, visible_body, 0)
    block(n_kv - 1, masked=True)
    out_t = jnp.concatenate([acc_s[h] / l_s[h:h + 1, :] for h in hs_all], axis=0)
    o_ref[0] = out_t.T.astype(o_ref.dtype)


def _fox_tables():
    col = jnp.arange(FX_HEADS * FX_AUG)[None, :]
    head, d = col // FX_AUG, col % FX_AUG
    c = jnp.arange(FX_WIDTH)[:, None]
    e_all = ((c == head * FX_HEAD_DIM + d) & (d < FX_HEAD_DIM)).astype(BF16)
    r = jnp.arange(3 * 128)[:, None]
    piece, src_lane = r // 128, r % 128
    from_head = src_lane == FX_F_LOCAL + head
    selq = (from_head & (d == FX_BIAS_LANE + piece)).astype(BF16)
    selk = (from_head & (d == FX_BIAS_LANE + 3 + piece)).astype(BF16)
    return e_all, selq, selk, jnp.eye(FX_WIDTH, dtype=BF16)


def _foxt(zfx3, cum_col, tables, tq=512, tk=512):
    bsz, seq, _ = zfx3.shape
    assert tk % tq == 0 and seq % tk == 0

    def full(a):
        return pl.BlockSpec(a.shape, lambda b, i: (0,) * a.ndim)

    return pl.pallas_call(
        functools.partial(_foxt_kernel, tq=tq, tk=tk, seq=seq),
        grid=(bsz, seq // tq),
        in_specs=[pl.BlockSpec((1, tq, FX_WIDTH), lambda b, i: (b, i, 0)),
                  pl.BlockSpec((1, seq, FX_WIDTH), lambda b, i: (b, 0, 1)),
                  pl.BlockSpec((1, seq, FX_WIDTH), lambda b, i: (b, 0, 2)),
                  pl.BlockSpec((1, seq, 128), lambda b, i: (b, 0, 0))] + [full(t) for t in tables],
        out_specs=pl.BlockSpec((1, tq, FX_WIDTH), lambda b, i: (b, i, 0)),
        out_shape=jax.ShapeDtypeStruct((bsz, seq, FX_WIDTH), BF16),
        scratch_shapes=[pltpu.VMEM((seq // tk, tk, FX_HEADS * FX_AUG), BF16),
                        pltpu.VMEM((seq // tk, FX_WIDTH, tk), BF16),
                        pltpu.VMEM((FX_HEADS, tq), F32), pltpu.VMEM((FX_HEADS, tq), F32),
                        pltpu.VMEM((FX_HEADS, FX_HEAD_DIM, tq), F32)],
        compiler_params=_params("parallel", "arbitrary"),
        name="fox_attn",
    )(zfx3, zfx3, zfx3, cum_col, *tables)


def _merge_kernel(x_ref, wg0_ref, wg1_ref, wg2_ref, b0_ref, b1_ref, b2_ref,
                  ya_ref, yb_ref, yc_ref, pa_ref, pb_ref, pc_ref, o_ref):
    x = x_ref[...]

    def gate(w_ref, b_ref):
        return _sigmoid(_dg(x, w_ref[...], NT_DIMS) + b_ref[...])

    acc = gate(wg0_ref, b0_ref) * _dot(ya_ref[...], pa_ref[...])
    acc = acc + gate(wg1_ref, b1_ref) * _dot(yb_ref[...], pb_ref[...])
    acc = acc + gate(wg2_ref, b2_ref) * _dot(yc_ref[...], pc_ref[...])
    o_ref[...] = acc.astype(o_ref.dtype)


def _merge(x16, w_gate, gate_b, ya, yb, yc, pa, pb, pc, layer, tm=1024, tn=512):
    m = ya.shape[0]
    per = D_MODEL // tn

    def yspec(width):
        return pl.BlockSpec((tm, width), lambda i, j: (i, 0))

    def wspec(width):
        return pl.BlockSpec((None, width, tn), lambda i, j: (layer, 0, j))

    def gspec(k):
        return pl.BlockSpec((None, tn, D_MODEL), lambda i, j: (layer, k * per + j, 0))

    def bspec(k):
        return pl.BlockSpec((1, tn), lambda i, j: (0, k * per + j))

    return pl.pallas_call(
        _merge_kernel,
        grid=(m // tm, D_MODEL // tn),
        in_specs=[yspec(D_MODEL), gspec(0), gspec(1), gspec(2), bspec(0), bspec(1), bspec(2),
                  yspec(GM_WIDTH), yspec(RW_WIDTH), yspec(FX_WIDTH),
                  wspec(GM_WIDTH), wspec(RW_WIDTH), wspec(FX_WIDTH)],
        out_specs=pl.BlockSpec((tm, tn), lambda i, j: (i, j)),
        out_shape=jax.ShapeDtypeStruct((m, D_MODEL), BF16),
        compiler_params=_params("parallel", "parallel"),
        name="merge",
    )(x16, w_gate, w_gate, w_gate, gate_b, gate_b, gate_b, ya, yb, yc, pa, pb, pc)


def _out_ln_kernel(m_ref, w_ref, x_ref, g_ref, b_ref, o32_ref, o16_ref):
    h = DEEPNORM_ALPHA * x_ref[...] + _dot(m_ref[...], w_ref[...])
    y = _layer_norm(h, g_ref[...], b_ref[...], LN_EPS)
    o32_ref[...] = y
    o16_ref[...] = y.astype(BF16)


def _out_ln(merged, w_out, layer, x, g, b, tm=512):
    m = x.shape[0]
    row = pl.BlockSpec((tm, D_MODEL), lambda i: (i, 0))
    vec = pl.BlockSpec((1, D_MODEL), lambda i: (0, 0))
    return pl.pallas_call(
        _out_ln_kernel,
        grid=(m // tm,),
        in_specs=[row, pl.BlockSpec((None, D_MODEL, D_MODEL), lambda i: (layer, 0, 0)), row, vec, vec],
        out_specs=[row, row],
        out_shape=[jax.ShapeDtypeStruct((m, D_MODEL), F32), jax.ShapeDtypeStruct((m, D_MODEL), BF16)],
        compiler_params=_params("parallel"),
        name="out_ln",
    )(merged, w_out, x, g, b)


def _router_kernel(x_ref, w_ref, b_ref, o_ref):
    logits = jnp.dot(x_ref[...], w_ref[...], preferred_element_type=F32,
                     precision=lax.Precision.HIGHEST) + b_ref[...]
    lane = lax.broadcasted_iota(jnp.int32, logits.shape, 1)
    logits = jnp.where(lane < MOE_EXPERTS, logits, -1e30)
    m1 = jnp.max(logits, axis=-1, keepdims=True)
    i1 = jnp.min(jnp.where(logits == m1, lane, 128), axis=-1, keepdims=True)
    rest = jnp.where(lane == i1, -1e30, logits)
    m2 = jnp.max(rest, axis=-1, keepdims=True)
    i2 = jnp.min(jnp.where(rest == m2, lane, 128), axis=-1, keepdims=True)
    e2 = jnp.exp(m2 - m1)
    w1 = 1.0 / (1.0 + e2)
    o_ref[...] = jnp.where(lane == i1, w1, 0.0) + jnp.where(lane == i2, e2 * w1, 0.0)


def _router(x1, w_pad, b_pad, tm=1024):
    m = x1.shape[0]
    return pl.pallas_call(
        _router_kernel,
        grid=(m // tm,),
        in_specs=[pl.BlockSpec((tm, D_MODEL), lambda i: (i, 0)),
                  pl.BlockSpec((D_MODEL, 128), lambda i: (0, 0)),
                  pl.BlockSpec((1, 128), lambda i: (0, 0))],
        out_specs=pl.BlockSpec((tm, 128), lambda i: (i, 0)),
        out_shape=jax.ShapeDtypeStruct((m, 128), F32),
        compiler_params=_params("parallel"),
        name="router",
    )(x1, w_pad, b_pad)


def _ffn_kernel(x_ref, c_ref, w1_ref, w3_ref, w2_ref, o_ref):
    @pl.when((pl.program_id(1) == 0) & (pl.program_id(2) == 0))
    def _():
        o_ref[...] = jnp.zeros_like(o_ref)

    x = x_ref[...]
    g = _dot(x, w1_ref[0])
    u = _dot(x, w3_ref[0])
    h = (g * _sigmoid(g)) * u * c_ref[0]
    o_ref[...] += _dot(h.astype(BF16), w2_ref[0])


def _ffn(xb, comb, w1, w3, w2, first, n_e, tm=1024, tf=512):
    m = xb.shape[0]
    f = w1.shape[-1]
    return pl.pallas_call(
        _ffn_kernel,
        grid=(m // tm, n_e, f // tf),
        in_specs=[pl.BlockSpec((tm, D_MODEL), lambda i, e, j: (i, 0)),
                  pl.BlockSpec((1, tm, 1), lambda i, e, j: (e, i, 0)),
                  pl.BlockSpec((1, D_MODEL, tf), lambda i, e, j: (first + e, 0, j)),
                  pl.BlockSpec((1, D_MODEL, tf), lambda i, e, j: (first + e, 0, j)),
                  pl.BlockSpec((1, tf, D_MODEL), lambda i, e, j: (first + e, j, 0))],
        out_specs=pl.BlockSpec((tm, D_MODEL), lambda i, e, j: (i, 0)),
        out_shape=jax.ShapeDtypeStruct((m, D_MODEL), F32),
        compiler_params=_params("parallel", "arbitrary", "arbitrary"),
        name="swiglu",
    )(xb, comb, w1, w3, w2)


MOE_BLK = 288
MOE_SUB = 1024
MOE_NSUB = 1
MOE_PASS = 4


def _moe_kernel(ntot_ref, bsub_ref, bloc_ref, x_ref, comb_ref, rank_ref, rankt_ref, w1_ref, w3_ref, w2_ref,
                o_ref, xs_s, acc_s, *, n_f, col_split):
    i, e, g, f = pl.program_id(0), pl.program_id(1), pl.program_id(2), pl.program_id(3)
    row = i * MOE_EXPERTS + e
    nb = jnp.clip(ntot_ref[row] - g * MOE_PASS, 0, MOE_PASS)

    @pl.when((e == 0) & (g == 0) & (f == 0))
    def _():
        o_ref[...] = jnp.zeros_like(o_ref)

    def sub_rows(b):
        sub = bsub_ref[row, g * MOE_PASS + b]
        return sub, pl.ds(pl.multiple_of(sub * MOE_SUB, MOE_SUB), MOE_SUB)

    @pl.when(f == 0)
    def _():
        slot = lax.broadcasted_iota(jnp.int32, (MOE_BLK, MOE_SUB), 0)

        def gather(b, carry):
            sub, rows = sub_rows(b)
            first = bloc_ref[row, g * MOE_PASS + b] * MOE_BLK
            rank_row = rankt_ref[sub, pl.ds(e, 1), :]
            onehot = jnp.where(rank_row == slot + first, 1.0, 0.0).astype(BF16)
            xs_s[b] = _dot(onehot, x_ref[rows, :]).astype(BF16)
            acc_s[b] = jnp.zeros(acc_s.shape[1:], F32)
            return carry

        lax.fori_loop(0, nb, gather, 0)

    def compute(b, carry):
        xb = xs_s[b]
        gate = _dot(xb, w1_ref[0])
        up = _dot(xb, w3_ref[0])
        h = (gate * _sigmoid(gate)) * up
        acc_s[b] += _dot(h.astype(BF16), w2_ref[0])
        return carry

    lax.fori_loop(0, nb, compute, 0)

    @pl.when(f == n_f - 1)
    def _():
        lane = lax.broadcasted_iota(jnp.int32, (MOE_SUB, 128), 1)
        sel = lane == e
        slot = lax.broadcasted_iota(jnp.int32, (MOE_SUB, MOE_BLK), 1)
        cw = D_MODEL // col_split

        def scatter(b, carry):
            _, rows = sub_rows(b)
            first = bloc_ref[row, g * MOE_PASS + b] * MOE_BLK
            c_col = jnp.sum(jnp.where(sel, comb_ref[rows, :], 0.0), axis=-1, keepdims=True)
            r_col = jnp.sum(jnp.where(sel, rank_ref[rows, :], 0), axis=-1, keepdims=True)
            onehot_t = jnp.where(r_col == slot + first, 1.0, 0.0).astype(BF16)
            for s in range(col_split):
                cs = slice(s * cw, (s + 1) * cw)
                o_ref[rows, cs] += _dot(onehot_t, acc_s[b, :, cs].astype(BF16)) * c_col
            return carry

        lax.fori_loop(0, nb, scatter, 0)


def _moe(xb, comb, plan, w1, w3, w2, layer, tf=1024):
    rank, rank_t, ntot, bsub, bloc = plan
    m = xb.shape[0]
    tm = MOE_SUB * MOE_NSUB
    _, n_e, _, f = w1.shape
    n_f = f // tf
    blocks_max = MOE_NSUB * pl.cdiv(MOE_SUB, MOE_BLK)
    n_pass = pl.cdiv(blocks_max, MOE_PASS)
    once = pl.Buffered(1)

    def wmap(sel):
        def index_map(i, e, g, j, ntot_ref, bsub_ref, bloc_ref):
            live = (g == 0) | (ntot_ref[i * MOE_EXPERTS + e] > g * MOE_PASS)
            jj = jnp.where(live, j, n_f - 1)
            return (layer, e, 0, jj) if sel == 0 else (layer, e, jj, 0)
        return index_map

    def tile(i, e, g, j, *_):
        return (i, 0)

    grid_spec = pltpu.PrefetchScalarGridSpec(
        num_scalar_prefetch=3,
        grid=(m // tm, n_e, n_pass, n_f),
        in_specs=[pl.BlockSpec((tm, D_MODEL), tile, pipeline_mode=once),
                  pl.BlockSpec((tm, 128), tile, pipeline_mode=once),
                  pl.BlockSpec((tm, 128), tile, pipeline_mode=once),
                  pl.BlockSpec((MOE_NSUB, 128, MOE_SUB), lambda i, e, g, j, *_: (i, 0, 0), pipeline_mode=once),
                  pl.BlockSpec((None, 1, D_MODEL, tf), wmap(0)),
                  pl.BlockSpec((None, 1, D_MODEL, tf), wmap(0)),
                  pl.BlockSpec((None, 1, tf, D_MODEL), wmap(1))],
        out_specs=pl.BlockSpec((tm, D_MODEL), tile, pipeline_mode=once),
        scratch_shapes=[pltpu.VMEM((MOE_PASS, MOE_BLK, D_MODEL), BF16),
                        pltpu.VMEM((MOE_PASS, MOE_BLK, D_MODEL), F32)],
    )
    return pl.pallas_call(
        functools.partial(_moe_kernel, n_f=n_f, col_split=2),
        grid_spec=grid_spec,
        out_shape=jax.ShapeDtypeStruct((m, D_MODEL), F32),
        compiler_params=_params("parallel", "arbitrary", "arbitrary", "arbitrary"),
        name="moe",
    )(ntot, bsub, bloc, xb, comb, rank, rank_t, w1, w3, w2)


def _moe_plan(comb):
    m = comb.shape[0]
    n_sub = m // MOE_SUB
    routed = (comb > 0.0).reshape(n_sub, MOE_SUB, 128).astype(jnp.int32)
    rank = jnp.cumsum(routed, axis=1) - routed
    rank = jnp.where(routed > 0, rank, -1)
    count = jnp.sum(routed, axis=1)[:, :MOE_EXPERTS]
    nblk = ((count + MOE_BLK - 1) // MOE_BLK).reshape(n_sub // MOE_NSUB, MOE_NSUB, MOE_EXPERTS)
    ends = jnp.swapaxes(jnp.cumsum(nblk, axis=1), 1, 2)
    starts = ends - jnp.swapaxes(nblk, 1, 2)
    ntot = ends[:, :, -1].reshape(-1)
    blocks_max = MOE_NSUB * pl.cdiv(MOE_SUB, MOE_BLK)
    bidx = jnp.arange(blocks_max)[None, None, :, None]
    bsub = jnp.sum((bidx >= ends[:, :, None, :]).astype(jnp.int32), axis=-1)
    bsub = jnp.minimum(bsub, MOE_NSUB - 1)
    sub_onehot = (bsub[..., None] == jnp.arange(MOE_NSUB)).astype(jnp.int32)
    bloc = bidx[..., 0] - jnp.sum(sub_onehot * starts[:, :, None, :], axis=-1)
    shape2 = (-1, blocks_max)
    return (rank.reshape(m, 128), jnp.swapaxes(rank, 1, 2), ntot.astype(jnp.int32),
            bsub.reshape(shape2).astype(jnp.int32), bloc.reshape(shape2).astype(jnp.int32))


def _ple_ln_kernel(xb_ref, wg_ref, bg_ref, p_ref, wp_ref, x1_ref, ff_ref, g_ref, b_ref, o32_ref, o16_ref):
    gate = _sigmoid(_dot(xb_ref[...], wg_ref[...]) + bg_ref[...])
    ple = gate * _dot(p_ref[...].astype(BF16), wp_ref[...])
    h = DEEPNORM_ALPHA * x1_ref[...] + ff_ref[...] + ple
    y = _layer_norm(h, g_ref[...], b_ref[...], LN_EPS)
    o32_ref[...] = y
    o16_ref[...] = y.astype(BF16)


def _ple_ln(x1b, wg, bg, p, wp, layer, x1, ff, g, b, tm=512):
    m = x1.shape[0]
    row = pl.BlockSpec((tm, D_MODEL), lambda i: (i, 0))
    vec = pl.BlockSpec((1, D_MODEL), lambda i: (0, 0))
    once = pl.Buffered(1)
    return pl.pallas_call(
        _ple_ln_kernel,
        grid=(m // tm,),
        in_specs=[row,
                  pl.BlockSpec((None, D_MODEL, D_MODEL), lambda i: (layer, 0, 0), pipeline_mode=once),
                  vec,
                  pl.BlockSpec((None, tm, PLE_DIM), lambda i: (layer, i, 0)),
                  pl.BlockSpec((None, PLE_DIM, D_MODEL), lambda i: (layer, 0, 0), pipeline_mode=once),
                  row, row, vec, vec],
        out_specs=[row, row],
        out_shape=[jax.ShapeDtypeStruct((m, D_MODEL), F32), jax.ShapeDtypeStruct((m, D_MODEL), BF16)],
        compiler_params=_params("parallel"),
        name="ple_ln",
    )(x1b, wg, bg, p, wp, x1, ff, g, b)


def _split_w_in(w_in):
    w_t = jnp.swapaxes(w_in, 1, 2)
    n_l, _, d = w_t.shape
    lora = w_t[:, COL_LORA:COL_FX]
    f = w_t[:, COL_F:COL_GATE]
    pad = jnp.zeros((n_l, MISC_W - lora.shape[1] - f.shape[1], d), w_t.dtype)
    misc = jnp.concatenate([lora, f, pad], axis=1)
    return (w_t[:, :COL_LORA].astype(BF16), misc.astype(BF16),
            w_t[:, COL_FX:COL_F].astype(BF16), w_t[:, COL_GATE:].astype(BF16))


def _pad_rows(w, lo, total):
    n_l, r, n = w.shape
    return jnp.concatenate([jnp.zeros((n_l, lo, n), w.dtype), w,
                            jnp.zeros((n_l, total - lo - r, n), w.dtype)], axis=1)


def kernel(x, p, w_in, gate_b, gm_ln_g, gm_ln_b, gm_ws, gm_bs, rw_mu, rw_w0, rw_wb, rw_a0, rw_ab, rw_gb, rw_kk, rw_ka, rw_rk, rw_lnx_g, rw_lnx_b, fx_fb, proj_a, proj_b, proj_c, w_out, ln1_g, ln1_b, ffn_w1, ffn_w3, ffn_w2, moe_router, moe_router_b, moe_w1, moe_w3, moe_w2, ple_gate_w, ple_gate_b, ple_proj, ln2_g, ln2_b):
    bsz, seq, d = x.shape
    m = bsz * seq
    n_l = w_in.shape[0]

    w_ar, w_misc, w_fx, w_gate = _split_w_in(w_in)
    mu_r = rw_mu[:, None, 0:RW_WIDTH]
    mu_k = rw_mu[:, None, RW_WIDTH:2 * RW_WIDTH]
    mu_v = rw_mu[:, None, 2 * RW_WIDTH:3 * RW_WIDTH]
    o = 3 * RW_WIDTH
    zl = lambda n: jnp.zeros((n_l, n), F32)
    n_lora = RW_LORA_W + RW_LORA_A + RW_LORA_G
    mu_m = jnp.concatenate([rw_mu[:, o:o + n_lora], zl(MISC_W - n_lora)], axis=-1)[:, None, :]
    wb_pad = _pad_rows(rw_wb, 0, 128).astype(BF16)
    ab_pad = _pad_rows(rw_ab, RW_LORA_W, 128).astype(BF16)
    gb_pad = _pad_rows(rw_gb, 0, 256).astype(BF16)
    fb_pad = jnp.concatenate([zl(FX_F_LOCAL), fx_fb, zl(128 - FX_F_LOCAL - FX_HEADS)], axis=-1)[:, None, :]
    head3 = lambda t: t.reshape(n_l, RW_HEADS, 1, RW_HEAD_DIM)
    gm_bs3 = gm_bs[..., None]
    router_w = jnp.concatenate([moe_router, jnp.zeros(moe_router.shape[:2] + (128 - MOE_EXPERTS,), F32)], axis=-1)
    router_b = jnp.concatenate([moe_router_b, jnp.zeros((moe_router_b.shape[0], 128 - MOE_EXPERTS), F32)],
                               axis=-1)[:, None, :]
    bf = lambda t: t.astype(BF16)
    proj_a16, proj_b16, proj_c16, w_out16 = bf(proj_a), bf(proj_b), bf(proj_c), bf(w_out)
    ffn_w1_16, ffn_w3_16, ffn_w2_16 = bf(ffn_w1), bf(ffn_w3), bf(ffn_w2)
    moe_w1_16, moe_w3_16, moe_w2_16 = bf(moe_w1), bf(moe_w3), bf(moe_w2)
    ple_gw16, ple_pw16 = bf(ple_gate_w), bf(ple_proj)
    ones_comb = jnp.ones((1, m, 1), F32)
    p3 = p.reshape(n_l, m, PLE_DIM)
    fox_tables = _fox_tables()

    x32 = x.reshape(m, d)
    x16 = x32.astype(BF16)
    for i in range(n_l):
        z = _matmul(x16, w_ar, i, BF16, 1024, 2048, "in_proj_ar")
        zm = _matmul(x16, w_misc, i, F32, 1024, MISC_W, "in_proj_misc")
        zfx = _matmul(x16, w_fx, i, BF16, 1024, 3 * FX_WIDTH, "in_proj_fx")
        z3 = z.reshape(bsz, seq, z.shape[-1])
        zm3 = zm.reshape(bsz, seq, MISC_W)
        y_a = _gmlp(z, gm_ln_g[i][None], gm_ln_b[i][None], gm_ws[i], gm_bs3[i])
        y_b = _rwkv(z3, zm3, mu_r[i], mu_k[i], mu_v[i], mu_m[i], rw_w0[i][None], wb_pad[i], rw_a0[i][None],
                    ab_pad[i], gb_pad[i], rw_kk[i][None], rw_ka[i][None], head3(rw_rk)[i],
                    head3(rw_lnx_g)[i], head3(rw_lnx_b)[i]).reshape(m, RW_WIDTH)
        cum_col = _fox_cum(zm3, fb_pad[i])
        y_c = _foxt(zfx.reshape(bsz, seq, 3 * FX_WIDTH), cum_col, fox_tables).reshape(m, FX_WIDTH)
        merged = _merge(x16, w_gate, gate_b[i][None], y_a, y_b, y_c, proj_a16, proj_b16, proj_c16, i)
        x1, x1b = _out_ln(merged, w_out16, i, x32, ln1_g[i][None], ln1_b[i][None])
        j = i // 2
        if i % 2 == 0:
            ff = _ffn(x1b, ones_comb, ffn_w1_16, ffn_w3_16, ffn_w2_16, j, 1)
        else:
            comb = _router(x1, router_w[j], router_b[j])
            ff = _moe(x1b, comb, _moe_plan(comb), moe_w1_16, moe_w3_16, moe_w2_16, j)
        x32, x16 = _ple_ln(x1b, ple_gw16, ple_gate_b[i][None], p3, ple_pw16, i, x1, ff,
                           ln2_g[i][None], ln2_b[i][None])
    return x32.reshape(bsz, seq, d)
```
